```python
import math
import jax
import jax.numpy as jnp
from jax import lax
import numpy as np

D_MODEL = 1024
BATCH = 8
SEQ = 4096
DEPTH = 4

N_MIXERS = 4
HEAD_DIM = 64
ROPE_THETA = 10000.0
Q_BLOCK = 128
EPS = 1e-6
NEG_INF = -1e30
TINY = 1e-30

FOX_HEADS = D_MODEL // HEAD_DIM
FOX_IN = 3 * FOX_HEADS * HEAD_DIM + FOX_HEADS

NSA_HEADS = D_MODEL // HEAD_DIM
NSA_KV_GROUPS = 4
NSA_HPG = NSA_HEADS // NSA_KV_GROUPS
NSA_BLOCK = 64
NSA_TOPN = 16
NSA_WINDOW = 512
NSA_CMP_HIDDEN = 2 * HEAD_DIM
NSA_SEL_CHUNK = 32
NSA_IN = NSA_HEADS * HEAD_DIM + 6 * NSA_KV_GROUPS * HEAD_DIM + 3 * NSA_HEADS

DIL_PAIRS = ((128, 1), (512, 4), (2048, 16))
DIL_HEADS = D_MODEL // HEAD_DIM
DIL_CHUNK = 64
DIL_IN = len(DIL_PAIRS) * 3 * DIL_HEADS * HEAD_DIM

DIFF_HEADS = D_MODEL // (2 * HEAD_DIM)
DIFF_IN = 6 * DIFF_HEADS * HEAD_DIM

N_EXPERTS = 16
N_EXPERT_GROUPS = 4
EXPERTS_PER_GROUP = N_EXPERTS // N_EXPERT_GROUPS
TOP_K = 2
EXPERT_FF = 512
MOE_BLOCK = 128

LAYERS_PER_MIXER = tuple(len(range(m, DEPTH, N_MIXERS)) for m in range(N_MIXERS))
N_FOX, N_NSA, N_DIL, N_DIFF = LAYERS_PER_MIXER

kernel_name = 'hybrid_fox_nsa_dilated_diff_moe'


def rms_norm(x, gain):
    xf = x.astype(jnp.float32)
    y = xf * lax.rsqrt(jnp.mean(xf * xf, axis=-1, keepdims=True) + EPS)
    return (y * gain.astype(jnp.float32)).astype(x.dtype)


def rope_tables(positions, dim):
    inv = ROPE_THETA ** (-jnp.arange(0, dim, 2, dtype=jnp.float32) / dim)
    ang = positions.astype(jnp.float32)[:, None] * inv[None, :]
    return jnp.cos(ang), jnp.sin(ang)


def apply_rope(x, cos, sin):
    x1, x2 = jnp.split(x.astype(jnp.float32), 2, axis=-1)
    return jnp.concatenate([x1 * cos - x2 * sin, x1 * sin + x2 * cos], axis=-1).astype(x.dtype)


def masked_softmax(s, mask):
    s = jnp.where(mask, s, NEG_INF)
    m = jnp.max(s, axis=-1, keepdims=True)
    e = jnp.where(mask, jnp.exp(s - m), 0.0)
    den = jnp.maximum(jnp.sum(e, axis=-1, keepdims=True), TINY)
    return e / den, (m + jnp.log(den))[..., 0]


def split_cols(t, sizes):
    outs, off = [], 0
    for n in sizes:
        outs.append(t[..., off:off + n])
        off += n
    return outs


def to_heads(t, n_heads):
    b, s, _ = t.shape
    return t.reshape(b, s, n_heads, -1).transpose(0, 2, 1, 3)


def from_heads(t):
    b, h, s, d = t.shape
    return t.transpose(0, 2, 1, 3).reshape(b, s, h * d)


def to_blocks(t, n, axis):
    shp = t.shape
    t = t.reshape(shp[:axis] + (n, shp[axis] // n) + shp[axis + 1:])
    return jnp.moveaxis(t, axis, 0)


def from_blocks(t, axis):
    t = jnp.moveaxis(t, 0, axis)
    shp = t.shape
    return t.reshape(shp[:axis] + (shp[axis] * shp[axis + 1],) + shp[axis + 2:])


def sweep_blocks(fn, n_blocks, *blocked):
    return lax.map(lambda a: fn(*a), (jnp.arange(n_blocks),) + tuple(blocked))


def fox_mixer(u, w_in, b_f, q_gain, k_gain, w_out):
    B, S, _ = u.shape
    H, nb, scale = FOX_HEADS, S // Q_BLOCK, HEAD_DIM ** -0.5
    q, k, v, f = split_cols(u @ w_in, [H * HEAD_DIM] * 3 + [H])
    q = rms_norm(to_heads(q, H), q_gain)
    k = rms_norm(to_heads(k, H), k_gain)
    v = to_heads(v, H)
    log_f = jax.nn.log_sigmoid((f + b_f).astype(jnp.float32))
    cum = jnp.cumsum(log_f, axis=1).transpose(0, 2, 1)
    kpos = jnp.arange(S)

    def block(i, qi, ci):
        s = jnp.einsum('bhqd,bhkd->bhqk', qi, k).astype(jnp.float32) * scale
        s = s + ci[..., None] - cum[:, :, None, :]
        qpos = i * Q_BLOCK + jnp.arange(Q_BLOCK)
        p, _ = masked_softmax(s, kpos[None, :] <= qpos[:, None])
        return jnp.einsum('bhqk,bhkd->bhqd', p.astype(v.dtype), v)

    o = from_blocks(sweep_blocks(block, nb, to_blocks(q, nb, 2), to_blocks(cum, nb, 2)), 2)
    return from_heads(o) @ w_out


def nsa_mixer(u, cos, sin, w_in, q_gain, k_gain, cmp_pos, cmp_w1, cmp_w2, w_out):
    B, S, _ = u.shape
    H, G, P, L, HD = NSA_HEADS, NSA_KV_GROUPS, NSA_HPG, NSA_BLOCK, HEAD_DIM
    NB = S // L
    n_sel = min(NSA_TOPN, NB)
    scale = HD ** -0.5
    q, kc, vc, ks, vs, kw, vw, gl = split_cols(u @ w_in, [H * HD] + [G * HD] * 6 + [3 * H])
    q = apply_rope(rms_norm(q.reshape(B, S, G, P, HD).transpose(0, 2, 3, 1, 4), q_gain), cos, sin)
    ks = apply_rope(rms_norm(to_heads(ks, G), k_gain[1]), cos, sin)
    kw = apply_rope(rms_norm(to_heads(kw, G), k_gain[2]), cos, sin)
    vs, vw = to_heads(vs, G), to_heads(vw, G)

    def compress(t, which):
        blk = t.reshape(B, G, NB, L, HD) + cmp_pos[which]
        hid = jax.nn.silu(blk.reshape(B, G, NB, L * HD) @ cmp_w1[which])
        return hid @ cmp_w2[which]

    kc = apply_rope(rms_norm(compress(to_heads(kc, G), 0), k_gain[0]), cos[L - 1::L], sin[L - 1::L])
    vc = compress(to_heads(vc, G), 1)
    t_pos = jnp.arange(S)
    j_blk = jnp.arange(NB)
    s_c = jnp.einsum('bgpsd,bgnd->bgpsn', q, kc).astype(jnp.float32) * scale
    p_c, _ = masked_softmax(s_c, (j_blk[None, :] + 1) * L <= t_pos[:, None] + 1)
    o_cmp = jnp.einsum('bgpsn,bgnd->bgpsd', p_c.astype(vc.dtype), vc)

    cur = t_pos // L
    imp = jnp.sum(p_c, axis=2)
    forced = (j_blk[None, :] == 0) | (j_blk[None, :] == cur[:, None]) | (j_blk[None, :] == cur[:, None] - 1)
    future = j_blk[None, :] > cur[:, None]
    imp = jnp.where(future, -1.0, jnp.where(forced, P + 1.0, imp))
    _, sel = lax.top_k(imp, n_sel)

    C = NSA_SEL_CHUNK
    nch = S // C
    ks_b = ks.reshape(B, G, NB, L, HD)
    vs_b = vs.reshape(B, G, NB, L, HD)
    bi = jnp.arange(B)[:, None, None, None]
    gi = jnp.arange(G)[None, :, None, None]

    def sel_block(i, qi, si):
        kg = ks_b[bi, gi, si].reshape(B, G, C, n_sel * L, HD)
        vg = vs_b[bi, gi, si].reshape(B, G, C, n_sel * L, HD)
        s = jnp.einsum('bgpcd,bgcmd->bgpcm', qi, kg).astype(jnp.float32) * scale
        tok = (si[..., None] * L + jnp.arange(L)).reshape(B, G, C, n_sel * L)
        qpos = i * C + jnp.arange(C)
        p, _ = masked_softmax(s, (tok <= qpos[:, None])[:, :, None])
        return jnp.einsum('bgpcm,bgcmd->bgpcd', p.astype(vg.dtype), vg)

    o_sel = from_blocks(sweep_blocks(sel_block, nch, to_blocks(q, nch, 3), to_blocks(sel, nch, 2)), 3)

    W = NSA_WINDOW
    nb = S // Q_BLOCK
    span = W + Q_BLOCK
    kw_p = jnp.pad(kw, ((0, 0), (0, 0), (W, 0), (0, 0)))
    vw_p = jnp.pad(vw, ((0, 0), (0, 0), (W, 0), (0, 0)))

    def win_block(i, qi):
        kb = lax.dynamic_slice_in_dim(kw_p, i * Q_BLOCK, span, axis=2)
        vb = lax.dynamic_slice_in_dim(vw_p, i * Q_BLOCK, span, axis=2)
        s = jnp.einsum('bgpqd,bgkd->bgpqk', qi, kb).astype(jnp.float32) * scale
        qpos = i * Q_BLOCK + jnp.arange(Q_BLOCK)
        kpos = i * Q_BLOCK - W + jnp.arange(span)
        dist = qpos[:, None] - kpos[None, :]
        p, _ = masked_softmax(s, (dist >= 0) & (dist < W) & (kpos[None, :] >= 0))
        return jnp.einsum('bgpqk,bgkd->bgpqd', p.astype(vb.dtype), vb)

    o_win = from_blocks(sweep_blocks(win_block, nb, to_blocks(q, nb, 3)), 3)

    g = jax.nn.sigmoid(gl.astype(jnp.float32)).reshape(B, S, 3, G, P).transpose(2, 0, 3, 4, 1)[..., None].astype(u.dtype)
    o = g[0] * o_cmp + g[1] * o_sel + g[2] * o_win
    return o.transpose(0, 3, 1, 2, 4).reshape(B, S, H * HD) @ w_out


def dilated_group(q, k, v, window, dilation):
    B, H, S, HD = q.shape
    C = DIL_CHUNK
    nb = S // C
    n_keys = window // dilation + 1
    span = C + window
    scale = HD ** -0.5
    k_p = jnp.pad(k, ((0, 0), (0, 0), (window, 0), (0, 0)))
    v_p = jnp.pad(v, ((0, 0), (0, 0), (window, 0), (0, 0)))
    ql = np.arange(C)[:, None]
    mm = np.arange(n_keys)[None, :]
    local = jnp.asarray(ql + window - mm * dilation)
    rel = jnp.asarray(ql - mm * dilation)

    def block(i, qi):
        kb = lax.dynamic_slice_in_dim(k_p, i * C, span, axis=2)[:, :, local]
        vb = lax.dynamic_slice_in_dim(v_p, i * C, span, axis=2)[:, :, local]
        s = jnp.einsum('bhqd,bhqnd->bhqn', qi, kb).astype(jnp.float32) * scale
        p, lse = masked_softmax(s, i * C + rel >= 0)
        return jnp.einsum('bhqn,bhqnd->bhqd', p.astype(vb.dtype), vb), lse

    o, lse = sweep_blocks(block, nb, to_blocks(q, nb, 2))
    return from_blocks(o, 2), from_blocks(lse, 2)


def dilated_mixer(u, cos, sin, w_in, q_gain, k_gain, w_out):
    B, S, _ = u.shape
    NG = len(DIL_PAIRS)
    proj = (u @ w_in).reshape(B, S, NG, 3, DIL_HEADS, HEAD_DIM).transpose(2, 3, 0, 4, 1, 5)
    outs, lses = [], []
    for gi, (window, dilation) in enumerate(DIL_PAIRS):
        q = apply_rope(rms_norm(proj[gi, 0], q_gain), cos, sin)
        k = apply_rope(rms_norm(proj[gi, 1], k_gain), cos, sin)
        o, lse = dilated_group(q, k, proj[gi, 2], window, dilation)
        outs.append(o)
        lses.append(lse)
    alpha = jax.nn.softmax(jnp.stack(lses, axis=0), axis=0)
    o = jnp.sum(alpha[..., None].astype(u.dtype) * jnp.stack(outs, axis=0), axis=0)
    return from_heads(o) @ w_out


def diff_mixer(u, cos, sin, w_in, q_gain, k_gain, lambdas, sub_gain, w_out, layer_idx):
    B, S, _ = u.shape
    H, HD = DIFF_HEADS, HEAD_DIM
    nb = S // Q_BLOCK
    scale = HD ** -0.5
    q, k, v = split_cols(u @ w_in, [2 * H * HD] * 3)
    q = apply_rope(rms_norm(to_heads(q, 2 * H), q_gain), cos, sin).reshape(B, H, 2, S, HD)
    k = apply_rope(rms_norm(to_heads(k, 2 * H), k_gain), cos, sin).reshape(B, H, 2, S, HD)
    v = to_heads(v, H)
    lam_init = 0.8 - 0.6 * math.exp(-0.3 * layer_idx)
    lam = (jnp.exp(jnp.sum(lambdas[0] * lambdas[1]).astype(jnp.float32))
           - jnp.exp(jnp.sum(lambdas[2] * lambdas[3]).astype(jnp.float32)) + lam_init)
    kpos = jnp.arange(S)

    def block(i, qi):
        s = jnp.einsum('bhcqd,bhckd->bhcqk', qi, k).astype(jnp.float32) * scale
        qpos = i * Q_BLOCK + jnp.arange(Q_BLOCK)
        p, _ = masked_softmax(s, kpos[None, :] <= qpos[:, None])
        a = p[:, :, 0] - lam * p[:, :, 1]
        return jnp.einsum('bhqk,bhkd->bhqd', a.astype(v.dtype), v)

    o = from_blocks(sweep_blocks(block, nb, to_blocks(q, nb, 3)), 2)
    o = rms_norm(o, sub_gain) * (1.0 - lam_init)
    return from_heads(o) @ w_out


def expert_dispatch(xt, eidx, w_gate, w_up, w_down):
    T, D = xt.shape
    NP = T * TOP_K
    E = w_gate.shape[0]
    R = NP + E * MOE_BLOCK
    NBK = R // MOE_BLOCK
    e_flat = eidx.reshape(-1)
    order = jnp.argsort(e_flat)
    e_sorted = e_flat[order]
    counts = jnp.bincount(e_flat, length=E)
    padded = (counts + MOE_BLOCK - 1) // MOE_BLOCK * MOE_BLOCK
    pad_end = jnp.cumsum(padded)
    pad_start = pad_end - padded
    start = jnp.cumsum(counts) - counts
    dest = pad_start[e_sorted] + (jnp.arange(NP) - start[e_sorted])
    row_token = jnp.full((R,), T, jnp.int32).at[dest].set((order // TOP_K).astype(jnp.int32))
    x_pad = jnp.concatenate([xt, jnp.zeros((1, D), xt.dtype)], axis=0)
    xs = x_pad[row_token].reshape(NBK, MOE_BLOCK, D)
    blk_expert = jnp.minimum(jnp.searchsorted(pad_end, jnp.arange(NBK) * MOE_BLOCK, side='right'), E - 1)

    def run(args):
        xb, e = args
        hid = jax.nn.silu(xb @ w_gate[e]) * (xb @ w_up[e])
        return hid @ w_down[e]

    ys = lax.map(run, (xs, blk_expert)).reshape(R, D)
    dest_orig = jnp.zeros((NP,), jnp.int32).at[order].set(dest.astype(jnp.int32))
    return ys[dest_orig].reshape(T, TOP_K, D)


def moe_ffn(u, router_w, router_b, w_gate, w_up, w_down):
    B, S, D = u.shape
    T = B * S
    xt = u.reshape(T, D)
    scores = jax.nn.sigmoid((xt @ router_w).astype(jnp.float32))
    sel = (scores + router_b.astype(jnp.float32)).reshape(T, N_EXPERT_GROUPS, EXPERTS_PER_GROUP)
    grp_score = jnp.sum(lax.top_k(sel, TOP_K)[0], axis=-1)
    grp = jnp.argmax(grp_score, axis=-1)
    in_grp = jnp.take_along_axis(sel, grp[:, None, None], axis=1)[:, 0]
    _, local = lax.top_k(in_grp, TOP_K)
    eidx = grp[:, None] * EXPERTS_PER_GROUP + local
    w = jnp.take_along_axis(scores, eidx, axis=1)
    w = w / jnp.sum(w, axis=-1, keepdims=True)
    y = expert_dispatch(xt, eidx, w_gate, w_up, w_down)
    return jnp.einsum('tk,tkd->td', w.astype(u.dtype), y).reshape(B, S, D)


def setup_inputs(seed: int = 0) -> dict:
    key = jax.random.key(seed)
    keys = jax.random.split(key, 40)
    counter = [0]

    def nxt():
        k = keys[counter[0]]
        counter[0] += 1
        return k

    def nrm(shape, scale):
        return scale * jax.random.normal(nxt(), shape, jnp.float32)

    def gain(shape):
        return 1.0 + 0.1 * jax.random.normal(nxt(), shape, jnp.float32)

    D, HD, L = D_MODEL, HEAD_DIM, NSA_BLOCK
    return {
        'x': nrm((BATCH, SEQ, D), 1.0),
        'c': nrm((BATCH, D), 1.0),
        'fox_w_in': nrm((N_FOX, D, FOX_IN), D ** -0.5),
        'fox_b_f': 2.0 + nrm((N_FOX, FOX_HEADS), 0.5),
        'fox_q_gain': gain((N_FOX, HD)),
        'fox_k_gain': gain((N_FOX, HD)),
        'fox_w_out': nrm((N_FOX, FOX_HEADS * HD, D), (FOX_HEADS * HD) ** -0.5),
        'nsa_w_in': nrm((N_NSA, D, NSA_IN), D ** -0.5),
        'nsa_q_gain': gain((N_NSA, HD)),
        'nsa_k_gain': gain((N_NSA, 3, HD)),
        'nsa_cmp_pos': nrm((N_NSA, 2, L, HD), 0.1),
        'nsa_cmp_w1': nrm((N_NSA, 2, L * HD, NSA_CMP_HIDDEN), (L * HD) ** -0.5),
        'nsa_cmp_w2': nrm((N_NSA, 2, NSA_CMP_HIDDEN, HD), NSA_CMP_HIDDEN ** -0.5),
        'nsa_w_out': nrm((N_NSA, NSA_HEADS * HD, D), (NSA_HEADS * HD) ** -0.5),
        'dil_w_in': nrm((N_DIL, D, DIL_IN), D ** -0.5),
        'dil_q_gain': gain((N_DIL, HD)),
        'dil_k_gain': gain((N_DIL, HD)),
        'dil_w_out': nrm((N_DIL, DIL_HEADS * HD, D), (DIL_HEADS * HD) ** -0.5),
        'diff_w_in': nrm((N_DIFF, D, DIFF_IN), D ** -0.5),
        'diff_q_gain': gain((N_DIFF, HD)),
        'diff_k_gain': gain((N_DIFF, HD)),
        'diff_lambda': nrm((N_DIFF, 4, HD), 0.1),
        'diff_sub_gain': gain((N_DIFF, 2 * HD)),
        'diff_w_out': nrm((N_DIFF, 2 * DIFF_HEADS * HD, D), (2 * DIFF_HEADS * HD) ** -0.5),
        'norm_gain': gain((DEPTH, 2, D)),
        'ada_w': nrm((DEPTH, D, 6 * D), 0.5 * D ** -0.5),
        'ada_b': nrm((DEPTH, 6 * D), 0.02),
        'router_w': nrm((D, N_EXPERTS), D ** -0.5),
        'router_b': nrm((N_EXPERTS,), 0.01),
        'moe_w_gate': nrm((DEPTH, N_EXPERTS, D, EXPERT_FF), D ** -0.5),
        'moe_w_up': nrm((DEPTH, N_EXPERTS, D, EXPERT_FF), D ** -0.5),
        'moe_w_down': nrm((DEPTH, N_EXPERTS, EXPERT_FF, D), EXPERT_FF ** -0.5),
    }


def reference(x, c, fox_w_in, fox_b_f, fox_q_gain, fox_k_gain, fox_w_out,
              nsa_w_in, nsa_q_gain, nsa_k_gain, nsa_cmp_pos, nsa_cmp_w1, nsa_cmp_w2, nsa_w_out,
              dil_w_in, dil_q_gain, dil_k_gain, dil_w_out,
              diff_w_in, diff_q_gain, diff_k_gain, diff_lambda, diff_sub_gain, diff_w_out,
              norm_gain, ada_w, ada_b, router_w, router_b, moe_w_gate, moe_w_up, moe_w_down):
    S = x.shape[1]
    cos, sin = rope_tables(jnp.arange(S), HEAD_DIM)
    cond = jax.nn.silu(c)
    h = x
    for i in range(DEPTH):
        mod = cond @ ada_w[i] + ada_b[i]
        sh1, sc1, g1, sh2, sc2, g2 = jnp.split(mod[:, None, :], 6, axis=-1)
        u = rms_norm(h, norm_gain[i, 0]) * (1.0 + sc1) + sh1
        kind, j = i % N_MIXERS, i // N_MIXERS
        if kind == 0:
            y = fox_mixer(u, fox_w_in[j], fox_b_f[j], fox_q_gain[j], fox_k_gain[j], fox_w_out[j])
        elif kind == 1:
            y = nsa_mixer(u, cos, sin, nsa_w_in[j], nsa_q_gain[j], nsa_k_gain[j], nsa_cmp_pos[j],
                          nsa_cmp_w1[j], nsa_cmp_w2[j], nsa_w_out[j])
        elif kind == 2:
            y = dilated_mixer(u, cos, sin, dil_w_in[j], dil_q_gain[j], dil_k_gain[j], dil_w_out[j])
        else:
            y = diff_mixer(u, cos, sin, diff_w_in[j], diff_q_gain[j], diff_k_gain[j], diff_lambda[j],
                           diff_sub_gain[j], diff_w_out[j], i)
        h = h + g1 * y
        u = rms_norm(h, norm_gain[i, 1]) * (1.0 + sc2) + sh2
        h = h + g2 * moe_ffn(u, router_w, router_b, moe_w_gate[i], moe_w_up[i], moe_w_down[i])
    return h
```

```python
import functools
import math

import jax
import jax.numpy as jnp
from jax import lax
from jax.experimental import pallas as pl
from jax.experimental.pallas import tpu as pltpu

F32 = jnp.float32
BF16 = jnp.bfloat16
I32 = jnp.int32

D_MODEL = 1024
HEAD_DIM = 64
LANES = 128
N_HEADS = D_MODEL // HEAD_DIM
N_PAIRS = D_MODEL // LANES
ROPE_THETA = 10000.0
EPS = 1e-6
NEG_INF = -1e30
TINY = 1e-30

NSA_GROUPS = 4
NSA_HPG = N_HEADS // NSA_GROUPS
NSA_BLOCK = 64
NSA_TOPN = 16
NSA_WINDOW = 512
DIL_PAIRS = ((128, 1), (512, 4), (2048, 16))

N_EXPERTS = 16
N_EXPERT_GROUPS = 4
EXPERTS_PER_GROUP = 4
TOP_K = 2
EXPERT_FF = 512
MOE_ROWS = 256

VMEM_LIMIT = 52 * 1024 * 1024


def _cp(*sem, vmem=VMEM_LIMIT):
    return pltpu.CompilerParams(dimension_semantics=sem, vmem_limit_bytes=vmem)


def _split3(a):
    hi = a.astype(BF16)
    r1 = a - hi.astype(F32)
    mid = r1.astype(BF16)
    lo = (r1 - mid.astype(F32)).astype(BF16)
    return hi, mid, lo


def _dot(a, b):
    return jnp.dot(a, b, preferred_element_type=F32)


def _dot_nt(a, b):
    return lax.dot_general(a, b, (((1,), (1,)), ((), ())), preferred_element_type=F32)


def _dot_f32(a, b, dot=_dot):
    ah, am, al = _split3(a)
    bh, bm, bl = _split3(b)
    return (dot(ah, bh) + (dot(ah, bm) + dot(am, bh))
            + (dot(ah, bl) + dot(al, bh) + dot(am, bm)))


def _dot_f32_exact_rhs(a, b_bf16):
    ah, am, al = _split3(a)
    return _dot(ah, b_bf16) + _dot(am, b_bf16) + _dot(al, b_bf16)


def _sigmoid(x):
    return 1.0 / (1.0 + jnp.exp(-x))


def _silu(x):
    return x * _sigmoid(x)


def _ada_kernel(c_ref, w_ref, b_ref, o_ref):
    c = c_ref[...]
    o_ref[0] = _dot_f32(_silu(c), w_ref[0]) + b_ref[0]


def ada_modulation(c, ada_w, ada_b):
    depth, d, n = ada_w.shape
    b = c.shape[0]
    tn = 1024
    return pl.pallas_call(
        _ada_kernel,
        grid=(depth, n // tn),
        in_specs=[pl.BlockSpec((b, d), lambda i, j: (0, 0)),
                  pl.BlockSpec((1, d, tn), lambda i, j: (i, 0, j)),
                  pl.BlockSpec((1, 1, tn), lambda i, j: (i, 0, j))],
        out_specs=pl.BlockSpec((1, b, tn), lambda i, j: (i, 0, j)),
        out_shape=jax.ShapeDtypeStruct((depth, b, n), F32),
        compiler_params=_cp("parallel", "parallel"),
        name="ada_modulation",
    )(c, ada_w, ada_b.reshape(depth, 1, n))


def _modulated_norm(x, gain, mod, shift_row, scale_row):
    ms = jnp.mean(x * x, axis=-1, keepdims=True)
    y = x * lax.rsqrt(ms + EPS) * gain
    return y * (1.0 + mod[scale_row:scale_row + 1, :]) + mod[shift_row:shift_row + 1, :]


def _norm_proj_kernel(h_ref, mod_ref, g_ref, w_ref, *rest, has_tail):
    u = _modulated_norm(h_ref[0], g_ref[...], mod_ref[0], 0, 1).astype(BF16)
    if has_tail:
        wt_ref, main_ref, tail_ref = rest
        tail_ref[0] = _dot(u, wt_ref[...])
    else:
        (main_ref,) = rest
    main_ref[0] = _dot(u, w_ref[...]).astype(main_ref.dtype)


def norm_proj(h, mod, gain, w_main, w_tail=None, *, tn=None, tm=512):
    b, s, d = h.shape
    n = w_main.shape[1]
    tn = tn or n
    tm = min(tm, s)
    in_specs = [pl.BlockSpec((1, tm, d), lambda j, bi, si: (bi, si, 0)),
                pl.BlockSpec((1, 6, d), lambda j, bi, si: (bi, 0, 0)),
                pl.BlockSpec((1, d), lambda j, bi, si: (0, 0)),
                pl.BlockSpec((d, tn), lambda j, bi, si: (0, j))]
    out_specs = [pl.BlockSpec((1, tm, tn), lambda j, bi, si: (bi, si, j))]
    out_shape = [jax.ShapeDtypeStruct((b, s, n), BF16)]
    args = [h, mod, gain.reshape(1, d), w_main]
    if w_tail is not None:
        in_specs.append(pl.BlockSpec((d, LANES), lambda j, bi, si: (0, 0)))
        out_specs.append(pl.BlockSpec((1, tm, LANES), lambda j, bi, si: (bi, si, 0)))
        out_shape.append(jax.ShapeDtypeStruct((b, s, LANES), F32))
        args.append(w_tail)
    outs = pl.pallas_call(
        functools.partial(_norm_proj_kernel, has_tail=w_tail is not None),
        grid=(n // tn, b, s // tm),
        in_specs=in_specs, out_specs=out_specs, out_shape=out_shape,
        compiler_params=_cp("parallel", "parallel", "parallel"),
        name="norm_proj",
    )(*args)
    return outs if w_tail is not None else outs[0]


def _lane_iota(rows):
    return lax.broadcasted_iota(I32, (rows, LANES), 1)


def _head_block_diag():
    r = lax.broadcasted_iota(I32, (LANES, LANES), 0) // HEAD_DIM
    c = lax.broadcasted_iota(I32, (LANES, LANES), 1) // HEAD_DIM
    return (r == c).astype(BF16)


def _head_norm_rope(x, gain, cos, sin, bd, rope):
    y = x * x
    hi = y.astype(BF16)
    lo = (y - hi.astype(F32)).astype(BF16)
    seg = _dot(hi, bd) + _dot(lo, bd)
    xn = x * lax.rsqrt(seg * (1.0 / HEAD_DIM) + EPS) * gain
    if rope:
        first_half = (_lane_iota(x.shape[0]) % HEAD_DIM) < HEAD_DIM // 2
        partner = jnp.where(first_half, pltpu.roll(xn, LANES - HEAD_DIM // 2, 1),
                            pltpu.roll(xn, HEAD_DIM // 2, 1))
        xn = xn * cos + partner * sin
    return xn


def _prep_kernel(cb_ref, x_ref, g_ref, cos_ref, sin_ref, o_ref, *, rope):
    del cb_ref
    bd = _head_block_diag()
    cos = cos_ref[...]
    sin = sin_ref[...]
    for c in range(x_ref.shape[2] // LANES):
        sl = slice(c * LANES, (c + 1) * LANES)
        x = x_ref[0, :, sl].astype(F32)
        o_ref[0, :, sl] = _head_norm_rope(x, g_ref[0, :, sl], cos, sin, bd, rope).astype(o_ref.dtype)


def head_prep(src, gains, col_blocks, cos, sin, *, rope, ts=512):
    b, s, _ = src.shape
    n = len(col_blocks)
    ts = min(ts, s)
    cb = jnp.asarray(col_blocks, I32)

    def x_map(bi, si, ci, cb_ref):
        return (bi, si, cb_ref[ci])

    return pl.pallas_call(
        functools.partial(_prep_kernel, rope=rope),
        grid_spec=pltpu.PrefetchScalarGridSpec(
            num_scalar_prefetch=1,
            grid=(b, s // ts, n),
            in_specs=[pl.BlockSpec((1, ts, D_MODEL), x_map),
                      pl.BlockSpec((1, 1, D_MODEL), lambda bi, si, ci, cb_ref: (ci, 0, 0)),
                      pl.BlockSpec((ts, LANES), lambda bi, si, ci, cb_ref: (si, 0)),
                      pl.BlockSpec((ts, LANES), lambda bi, si, ci, cb_ref: (si, 0))],
            out_specs=pl.BlockSpec((1, ts, D_MODEL), lambda bi, si, ci, cb_ref: (bi, si, ci))),
        out_shape=jax.ShapeDtypeStruct((b, s, n * D_MODEL), BF16),
        compiler_params=_cp("parallel", "parallel", "arbitrary"),
        name="head_prep",
    )(cb, src, gains, cos, sin)


def rope_lane_tables(s):
    inv = ROPE_THETA ** (-jnp.arange(0, HEAD_DIM, 2, dtype=F32) / HEAD_DIM)
    ang = jnp.arange(s).astype(F32)[:, None] * inv[None, :]
    cos, sin = jnp.cos(ang), jnp.sin(ang)
    return jnp.tile(cos, (1, 4)), jnp.tile(jnp.concatenate([-sin, sin], axis=-1), (1, 2)), cos, sin


def _tile_gain(g, scale=1.0):
    return jnp.tile(g.astype(F32) * scale, N_HEADS).reshape(1, D_MODEL)


def _fox_cum_kernel(f_ref, b_ref, o_ref, carry_ref):
    si = pl.program_id(1)
    ts = f_ref.shape[1]

    @pl.when(si == 0)
    def _():
        carry_ref[...] = jnp.zeros_like(carry_ref)

    z = f_ref[0] + b_ref[...]
    log_f = -(jnp.maximum(-z, 0.0) + jnp.log1p(jnp.exp(-jnp.abs(z))))
    r = lax.broadcasted_iota(I32, (ts, ts), 0)
    c = lax.broadcasted_iota(I32, (ts, ts), 1)
    upper = (r <= c).astype(BF16)
    cum = _dot_f32_exact_rhs(log_f.T, upper) + carry_ref[:, 0:1]
    o_ref[0] = cum[0:N_HEADS, :]
    carry_ref[...] = jnp.broadcast_to(cum[:, ts - 1:ts], carry_ref.shape)


def fox_cumulative_gate(tail, b_f, *, ts=256):
    b, s, _ = tail.shape
    ts = min(ts, s)
    bias = jnp.zeros((1, LANES), F32).at[0, :N_HEADS].set(b_f.astype(F32))
    return pl.pallas_call(
        _fox_cum_kernel,
        grid=(b, s // ts),
        in_specs=[pl.BlockSpec((1, ts, LANES), lambda bi, si: (bi, si, 0)),
                  pl.BlockSpec((1, LANES), lambda bi, si: (0, 0))],
        out_specs=pl.BlockSpec((1, N_HEADS, ts), lambda bi, si: (bi, 0, si)),
        out_shape=jax.ShapeDtypeStruct((b, N_HEADS, s), F32),
        scratch_shapes=[pltpu.VMEM((LANES, LANES), F32)],
        compiler_params=_cp("parallel", "arbitrary"),
        name="fox_cumulative_gate",
    )(tail, bias)


def _flash_kernel(*refs, tq, tk, window, mode, fin, has_bias, has_lse, lam_init):
    it = iter(refs)
    q_ref, k_ref, v_ref = next(it), next(it), next(it)
    kb_ref = next(it) if has_bias else None
    if fin == "diff":
        lam_ref, sg_ref = next(it), next(it)
    o_ref = next(it)
    lse_ref = next(it) if has_lse else None
    m_sc, l_sc, acc_sc = next(it), next(it), next(it)

    q_start = pl.program_id(2) * tq
    low_half = _lane_iota(tq) < HEAD_DIM
    if mode == "pair":
        q = q_ref[0]
        zero = jnp.zeros_like(q)
        qh = (jnp.where(low_half, q, zero), jnp.where(low_half, zero, q))
    else:
        qh = (q_ref[0, :, 0:LANES], q_ref[0, :, LANES:2 * LANES])

    m_sc[...] = jnp.full(m_sc.shape, NEG_INF, F32)
    l_sc[...] = jnp.zeros(l_sc.shape, F32)
    acc_sc[...] = jnp.zeros(acc_sc.shape, F32)

    last_blk = (q_start + (tq - 1)) // tk
    first_blk = (jnp.maximum(q_start - (window - 1), 0) // tk) if window else 0
    row = q_start + lax.broadcasted_iota(I32, (tq, tk), 0)
    col0 = lax.broadcasted_iota(I32, (tq, tk), 1)

    def body(kv, carry):
        ks = pl.multiple_of(kv * tk, tk)
        kblk = k_ref[0, pl.ds(ks, tk), :]
        vblk = v_ref[0, pl.ds(ks, tk), :]
        col = ks + col0
        mask = col <= row
        if window:
            mask = mask & ((row - col) < window)
        if has_bias:
            kb = kb_ref[0, 0, kv]
        for h in range(2):
            kk = kblk if mode == "pair" else kblk[:, h * LANES:(h + 1) * LANES]
            s = _dot_nt(qh[h], kk)
            if has_bias:
                s = s - kb[h:h + 1, :]
            s = jnp.where(mask, s, NEG_INF)
            m_old = m_sc[h][:, 0:1]
            m_new = jnp.maximum(m_old, jnp.max(s, axis=-1, keepdims=True))
            p = jnp.where(mask, jnp.exp(s - m_new), 0.0)
            alpha = jnp.exp(m_old - m_new)
            l_new = alpha * l_sc[h][:, 0:1] + jnp.sum(p, axis=-1, keepdims=True)
            acc_sc[h] = alpha * acc_sc[h] + _dot(p.astype(BF16), vblk)
            m_sc[h] = jnp.broadcast_to(m_new, (tq, LANES))
            l_sc[h] = jnp.broadcast_to(l_new, (tq, LANES))
        return carry

    lax.fori_loop(first_blk, last_blk + 1, body, 0)

    l0 = jnp.maximum(l_sc[0][:, 0:1], TINY)
    l1 = jnp.maximum(l_sc[1][:, 0:1], TINY)
    o0 = acc_sc[0] * (1.0 / l0)
    o1 = acc_sc[1] * (1.0 / l1)
    if fin == "select":
        o = jnp.where(low_half, o0, o1)
    else:
        lam_rows = lam_ref[...]
        lam = (jnp.exp(jnp.sum(lam_rows[0:1] * lam_rows[1:2], axis=-1, keepdims=True))
               - jnp.exp(jnp.sum(lam_rows[2:3] * lam_rows[3:4], axis=-1, keepdims=True)) + lam_init)
        o = o0 - lam * o1
        ms = jnp.mean(o * o, axis=-1, keepdims=True)
        o = o * lax.rsqrt(ms + EPS) * sg_ref[...] * (1.0 - lam_init)
    o_ref[0] = o.astype(o_ref.dtype)
    if has_lse:
        lse0 = m_sc[0][:, 0:1] + jnp.log(l0)
        lse1 = m_sc[1][:, 0:1] + jnp.log(l1)
        lse_ref[0] = jnp.where(low_half, lse0, lse1)


def flash_attention(q, k, v, *, q_blk, k_blk, v_blk, n_batch, n_inner, out_cols, out_blk,
                    mode="pair", fin="select", window=0, key_bias=None, lam=None, sub_gain=None,
                    lam_init=0.0, with_lse=False, tq=256, tk=256):
    nb, s, _ = q.shape
    tq, tk = min(tq, s), min(tk, s)
    wq = LANES if mode == "pair" else 2 * LANES
    in_specs = [pl.BlockSpec((1, tq, wq), lambda b, j, i: (b, i, q_blk(j))),
                pl.BlockSpec((1, s, wq), lambda b, j, i: (b, 0, k_blk(j))),
                pl.BlockSpec((1, s, LANES), lambda b, j, i: (b, 0, v_blk(j)))]
    args = [q, k, v]
    if key_bias is not None:
        in_specs.append(pl.BlockSpec((1, 1, s // tk, 2, tk), lambda b, j, i: (b, j, 0, 0, 0)))
        args.append(key_bias)
    if fin == "diff":
        in_specs += [pl.BlockSpec((4, LANES), lambda b, j, i: (0, 0)),
                     pl.BlockSpec((1, LANES), lambda b, j, i: (0, 0))]
        args += [lam, sub_gain]
    out_specs = [pl.BlockSpec((1, tq, LANES), lambda b, j, i: (b, i, out_blk(j)))]
    out_shape = [jax.ShapeDtypeStruct((nb, s, out_cols), BF16)]
    if with_lse:
        out_specs.append(pl.BlockSpec((1, tq, LANES), lambda b, j, i: (b, i, out_blk(j))))
        out_shape.append(jax.ShapeDtypeStruct((nb, s, out_cols), F32))
    outs = pl.pallas_call(
        functools.partial(_flash_kernel, tq=tq, tk=tk, window=window, mode=mode, fin=fin,
                          has_bias=key_bias is not None, has_lse=with_lse, lam_init=lam_init),
        grid=(n_batch, n_inner, s // tq),
        in_specs=in_specs, out_specs=out_specs, out_shape=out_shape,
        scratch_shapes=[pltpu.VMEM((2, tq, LANES), F32)] * 3,
        compiler_params=_cp("parallel", "parallel", "arbitrary"),
        name="flash_" + mode + "_" + fin,
    )(*args)
    return outs if with_lse else outs[0]


def _nsa_prep_kernel(q_ref, ks_ref, vs_ref, kw_ref, vw_ref, g_ref, cos_ref, sin_ref,
                     qo_ref, ks2_ref, vs2_ref, kw2_ref, vw2_ref):
    ts = q_ref.shape[1]
    bd = _head_block_diag()
    cos, sin = cos_ref[...], sin_ref[...]
    lane = _lane_iota(ts)
    low_half = lane < HEAD_DIM
    for c in range(N_PAIRS):
        sl = slice(c * LANES, (c + 1) * LANES)
        qo_ref[0, :, sl] = _head_norm_rope(q_ref[0, :, sl].astype(F32), g_ref[0:1, :], cos, sin, bd,
                                           True).astype(qo_ref.dtype)
    t = pl.program_id(1) * ts + lax.broadcasted_iota(I32, (ts, LANES), 0)
    blk_onehot = ((t // NSA_BLOCK) == (lane % HEAD_DIM)).astype(F32)
    zeros = jnp.zeros((ts, LANES), F32)

    def spread(x, fill, out_ref, c):
        xr = pltpu.roll(x, HEAD_DIM, 1)
        base = 2 * c * 2 * LANES
        out_ref[0, :, base:base + LANES] = jnp.where(low_half, x, fill).astype(out_ref.dtype)
        out_ref[0, :, base + LANES:base + 2 * LANES] = jnp.where(low_half, fill, xr).astype(out_ref.dtype)
        out_ref[0, :, base + 2 * LANES:base + 3 * LANES] = jnp.where(low_half, xr, fill).astype(out_ref.dtype)
        out_ref[0, :, base + 3 * LANES:base + 4 * LANES] = jnp.where(low_half, fill, x).astype(out_ref.dtype)

    def dup(x, out_ref, c):
        xr = pltpu.roll(x, HEAD_DIM, 1)
        out_ref[0, :, 2 * c * LANES:(2 * c + 1) * LANES] = jnp.where(low_half, x, xr).astype(out_ref.dtype)
        out_ref[0, :, (2 * c + 1) * LANES:(2 * c + 2) * LANES] = jnp.where(low_half, xr, x).astype(out_ref.dtype)

    for c in range(NSA_GROUPS // 2):
        sl = slice(c * LANES, (c + 1) * LANES)
        ks = _head_norm_rope(ks_ref[0, :, sl].astype(F32), g_ref[1:2, :], cos, sin, bd, True)
        kw = _head_norm_rope(kw_ref[0, :, sl].astype(F32), g_ref[2:3, :], cos, sin, bd, True)
        spread(ks, blk_onehot, ks2_ref, c)
        spread(kw, zeros, kw2_ref, c)
        dup(vs_ref[0, :, sl].astype(F32), vs2_ref, c)
        dup(vw_ref[0, :, sl].astype(F32), vw2_ref, c)


def nsa_prep(main, gains, cos, sin, *, ts=512):
    b, s, _ = main.shape
    ts = min(ts, s)
    gw = NSA_GROUPS * HEAD_DIM

    def kv_spec(i):
        return pl.BlockSpec((1, ts, gw), lambda bi, si: (bi, si, i))

    return pl.pallas_call(
        _nsa_prep_kernel,
        grid=(b, s // ts),
        in_specs=[pl.BlockSpec((1, ts, D_MODEL), lambda bi, si: (bi, si, 0)),
                  kv_spec(6), kv_spec(7), kv_spec(8), kv_spec(9),
                  pl.BlockSpec((8, LANES), lambda bi, si: (0, 0)),
                  pl.BlockSpec((ts, LANES), lambda bi, si: (si, 0)),
                  pl.BlockSpec((ts, LANES), lambda bi, si: (si, 0))],
        out_specs=[pl.BlockSpec((1, ts, D_MODEL), lambda bi, si: (bi, si, 0)),
                   pl.BlockSpec((1, ts, NSA_GROUPS * 2 * LANES), lambda bi, si: (bi, si, 0)),
                   pl.BlockSpec((1, ts, NSA_GROUPS * LANES), lambda bi, si: (bi, si, 0)),
                   pl.BlockSpec((1, ts, NSA_GROUPS * 2 * LANES), lambda bi, si: (bi, si, 0)),
                   pl.BlockSpec((1, ts, NSA_GROUPS * LANES), lambda bi, si: (bi, si, 0))],
        out_shape=[jax.ShapeDtypeStruct((b, s, D_MODEL), BF16),
                   jax.ShapeDtypeStruct((b, s, NSA_GROUPS * 2 * LANES), BF16),
                   jax.ShapeDtypeStruct((b, s, NSA_GROUPS * LANES), BF16),
                   jax.ShapeDtypeStruct((b, s, NSA_GROUPS * 2 * LANES), BF16),
                   jax.ShapeDtypeStruct((b, s, NSA_GROUPS * LANES), BF16)],
        compiler_params=_cp("parallel", "parallel"),
        name="nsa_prep",
    )(main, main, main, main, main, gains, cos, sin)


def _nsa_compress_kernel(x_ref, pos_ref, w1_ref, w2_ref, g_ref, cos_ref, sin_ref, o_ref, *, is_key):
    x = (x_ref[...].astype(F32) + pos_ref[...]).astype(BF16)
    hid = _silu(_dot(x, w1_ref[...]))
    y = _dot(hid.astype(BF16), w2_ref[...])
    if is_key:
        ms = jnp.mean(y * y, axis=-1, keepdims=True)
        y = y * lax.rsqrt(ms + EPS) * g_ref[...]
        r = lax.broadcasted_iota(I32, (HEAD_DIM, HEAD_DIM), 0)
        c = lax.broadcasted_iota(I32, (HEAD_DIM, HEAD_DIM), 1)
        swap = (((r + HEAD_DIM // 2) % HEAD_DIM) == c).astype(BF16)
        y = y * cos_ref[...] + _dot_f32_exact_rhs(y, swap) * sin_ref[...]
    o_ref[...] = y


def nsa_compress(x, pos, w1, w2, gain, cos_blk, sin_blk, *, is_key):
    rows, k = x.shape
    nb = cos_blk.shape[0]
    hid = w1.shape[1]
    return pl.pallas_call(
        functools.partial(_nsa_compress_kernel, is_key=is_key),
        grid=(rows // nb,),
        in_specs=[pl.BlockSpec((nb, k), lambda i: (i, 0)),
                  pl.BlockSpec((1, k), lambda i: (0, 0)),
                  pl.BlockSpec((k, hid), lambda i: (0, 0)),
                  pl.BlockSpec((hid, HEAD_DIM), lambda i: (0, 0)),
                  pl.BlockSpec((1, HEAD_DIM), lambda i: (0, 0)),
                  pl.BlockSpec((nb, HEAD_DIM), lambda i: (0, 0)),
                  pl.BlockSpec((nb, HEAD_DIM), lambda i: (0, 0))],
        out_specs=pl.BlockSpec((nb, HEAD_DIM), lambda i: (i, 0)),
        out_shape=jax.ShapeDtypeStruct((rows, HEAD_DIM), F32),
        compiler_params=_cp("parallel"),
        name="nsa_compress",
    )(x, pos, w1, w2, gain, cos_blk, sin_blk)


def _nsa_cmp_kernel(q_ref, kc_ref, vc_ref, o_ref, qa_ref, *, n_sel):
    tq = q_ref.shape[1]
    lane = _lane_iota(tq)
    low_half = lane < HEAD_DIM
    blk = lane % HEAD_DIM
    t = pl.program_id(2) * tq + lax.broadcasted_iota(I32, (tq, LANES), 0)
    cmask = (blk + 1) * NSA_BLOCK <= t + 1
    kc = kc_ref[0, 0]
    vc = vc_ref[0, 0]
    imp = jnp.zeros((tq, LANES), F32)
    qblk = [q_ref[0, :, 0:LANES], q_ref[0, :, LANES:2 * LANES]]
    zero = jnp.zeros_like(qblk[0])
    outs = [None, None]
    for p in range(NSA_HPG):
        in_low = (p % 2) == 0
        qb = qblk[p // 2]
        qm = jnp.where(low_half, qb, zero) if in_low else jnp.where(low_half, zero, qb)
        s = jnp.where(cmask, _dot_nt(qm, kc), NEG_INF)
        m = jnp.max(s, axis=-1, keepdims=True)
        e = jnp.where(cmask, jnp.exp(s - m), 0.0)
        den = jnp.maximum(0.5 * jnp.sum(e, axis=-1, keepdims=True), TINY)
        pc = e / den
        imp = imp + pc
        o = _dot(pc.astype(BF16), vc)
        prev = outs[p // 2]
        outs[p // 2] = o if prev is None else jnp.where(low_half, prev, o)
    o_ref[0, :, 0:LANES] = outs[0].astype(o_ref.dtype)
    o_ref[0, :, LANES:2 * LANES] = outs[1].astype(o_ref.dtype)

    cur = t // NSA_BLOCK
    forced = (blk == 0) | (blk == cur) | (blk == cur - 1)
    x = jnp.where(blk > cur, -1.0, jnp.where(forced, NSA_HPG + 1.0, imp))
    blk_f = blk.astype(F32)
    selected = jnp.zeros((tq, LANES), jnp.bool_)
    for _ in range(n_sel):
        mx = jnp.max(x, axis=-1, keepdims=True)
        first = jnp.min(jnp.where(x == mx, blk_f, float(LANES)), axis=-1, keepdims=True)
        hit = blk_f == first
        selected = selected | hit
        x = jnp.where(hit, -2.0, x)
    sel_bias = jnp.where(selected, 0.0, NEG_INF).astype(qa_ref.dtype)
    for p in range(NSA_HPG):
        qb = qblk[p // 2]
        qa = jnp.where(low_half, qb, sel_bias) if p % 2 == 0 else jnp.where(low_half, sel_bias, qb)
        qa_ref[0, :, p * LANES:(p + 1) * LANES] = qa


def nsa_compressed_attention(q, kc2, vc2, *, n_sel, tq=256):
    b, s, _ = q.shape
    tq = min(tq, s)
    gq = NSA_HPG * HEAD_DIM
    return pl.pallas_call(
        functools.partial(_nsa_cmp_kernel, n_sel=n_sel),
        grid=(b, NSA_GROUPS, s // tq),
        in_specs=[pl.BlockSpec((1, tq, gq), lambda bi, g, i: (bi, i, g)),
                  pl.BlockSpec((1, 1, LANES, LANES), lambda bi, g, i: (bi, g, 0, 0)),
                  pl.BlockSpec((1, 1, LANES, LANES), lambda bi, g, i: (bi, g, 0, 0))],
        out_specs=[pl.BlockSpec((1, tq, gq), lambda bi, g, i: (bi, i, g)),
                   pl.BlockSpec((1, tq, 2 * gq), lambda bi, g, i: (bi, i, g))],
        out_shape=[jax.ShapeDtypeStruct((b, s, D_MODEL), BF16),
                   jax.ShapeDtypeStruct((b, s, 2 * D_MODEL), BF16)],
        compiler_params=_cp("parallel", "parallel", "parallel"),
        name="nsa_compressed_attention",
    )(q, kc2, vc2)


def _route(logits_t, rb):
    scores = _sigmoid(logits_t)
    sel = scores + rb
    rows = [sel[i:i + 1, :] for i in range(N_EXPERTS)]
    srows = [scores[i:i + 1, :] for i in range(N_EXPERTS)]
    best = grp = None
    for g in range(N_EXPERT_GROUPS):
        a, b, c, d = rows[4 * g:4 * g + 4]
        hi1, lo1, hi2, lo2 = jnp.maximum(a, b), jnp.minimum(a, b), jnp.maximum(c, d), jnp.minimum(c, d)
        gs = jnp.maximum(hi1, hi2) + jnp.maximum(jnp.minimum(hi1, hi2), jnp.maximum(lo1, lo2))
        if g == 0:
            best, grp = gs, jnp.zeros(gs.shape, I32)
        else:
            better = gs > best
            grp = jnp.where(better, g, grp)
            best = jnp.where(better, gs, best)

    def pick(vals, i):
        out = vals[i]
        for g in range(1, N_EXPERT_GROUPS):
            out = jnp.where(grp == g, vals[4 * g + i], out)
        return out

    v = [pick(rows, i) for i in range(EXPERTS_PER_GROUP)]
    w = [pick(srows, i) for i in range(EXPERTS_PER_GROUP)]
    l1, b1, w1 = jnp.zeros(grp.shape, I32), v[0], w[0]
    for i in range(1, EXPERTS_PER_GROUP):
        better = v[i] > b1
        l1 = jnp.where(better, i, l1)
        b1 = jnp.where(better, v[i], b1)
        w1 = jnp.where(better, w[i], w1)
    have = jnp.zeros(grp.shape, jnp.bool_)
    l2, b2, w2 = jnp.zeros(grp.shape, I32), jnp.zeros_like(b1), jnp.zeros_like(w1)
    for i in range(EXPERTS_PER_GROUP):
        valid = l1 != i
        better = valid & (jnp.logical_not(have) | (v[i] > b2))
        l2 = jnp.where(better, i, l2)
        b2 = jnp.where(better, v[i], b2)
        w2 = jnp.where(better, w[i], w2)
        have = have | valid
    wsum = w1 + w2
    return (grp * EXPERTS_PER_GROUP + l1, grp * EXPERTS_PER_GROUP + l2), (w1 / wsum, w2 / wsum)


def _outproj_router_kernel(*refs, mix):
    it = iter(refs)
    if mix == "plain":
        x = next(it)[0]
    elif mix == "nsa":
        o_refs = [next(it), next(it), next(it)]
        gl = _sigmoid(next(it)[0])
        g_hi = gl.astype(BF16)
        g_lo = (gl - g_hi.astype(F32)).astype(BF16)
        r = lax.broadcasted_iota(I32, (LANES, D_MODEL), 0)
        c = lax.broadcasted_iota(I32, (LANES, D_MODEL), 1) // HEAD_DIM
        x = None
        for i in range(3):
            expand = (r == c + i * N_HEADS).astype(BF16)
            term = (_dot(g_hi, expand) + _dot(g_lo, expand)) * o_refs[i][0].astype(F32)
            x = term if x is None else x + term
        x = x.astype(BF16)
    else:
        o_refs = [next(it), next(it), next(it)]
        lse = [next(it)[0], next(it)[0], next(it)[0]]
        mx = jnp.maximum(jnp.maximum(lse[0], lse[1]), lse[2])
        ex = [jnp.exp(l - mx) for l in lse]
        den = ex[0] + ex[1] + ex[2]
        x = ((ex[0] / den) * o_refs[0][0].astype(F32) + (ex[1] / den) * o_refs[1][0].astype(F32)
             + (ex[2] / den) * o_refs[2][0].astype(F32)).astype(BF16)
    h_ref, mod_ref, w_ref, g2_ref, rw_ref, rb_ref = (next(it) for _ in range(6))
    ho_ref, u_ref, e_ref, wt_ref = (next(it) for _ in range(4))
    mod = mod_ref[0]
    h_new = h_ref[0] + mod[2:3, :] * _dot(x, w_ref[...])
    ho_ref[0] = h_new
    u = _modulated_norm(h_new, g2_ref[...], mod, 3, 4)
    u_ref[0] = u
    logits_t = _dot_f32(rw_ref[...], u, dot=_dot_nt)
    eidx, wts = _route(logits_t, rb_ref[:, 0:1])
    for k in range(TOP_K):
        e_ref[0, k:k + 1, :] = eidx[k]
        wt_ref[0, k:k + 1, :] = wts[k]


def outproj_router(attn_inputs, h, mod, w_out, gain2, router_wt, router_b, *, mix, tm=256):
    b, s, d = h.shape
    tm = min(tm, s)
    row_spec = pl.BlockSpec((1, tm, d), lambda bi, si: (bi, si, 0))
    in_specs, args = [], []
    for a in attn_inputs:
        in_specs.append(pl.BlockSpec((1, tm, a.shape[2]), lambda bi, si: (bi, si, 0)))
        args.append(a)
    in_specs += [row_spec,
                 pl.BlockSpec((1, 6, d), lambda bi, si: (bi, 0, 0)),
                 pl.BlockSpec((d, d), lambda bi, si: (0, 0)),
                 pl.BlockSpec((1, d), lambda bi, si: (0, 0)),
                 pl.BlockSpec((N_EXPERTS, d), lambda bi, si: (0, 0)),
                 pl.BlockSpec((N_EXPERTS, LANES), lambda bi, si: (0, 0))]
    args += [h, mod, w_out, gain2.reshape(1, d), router_wt, router_b]
    return pl.pallas_call(
        functools.partial(_outproj_router_kernel, mix=mix),
        grid=(b, s // tm),
        in_specs=in_specs,
        out_specs=[row_spec, row_spec,
                   pl.BlockSpec((1, TOP_K, tm), lambda bi, si: (bi, 0, si)),
                   pl.BlockSpec((1, TOP_K, tm), lambda bi, si: (bi, 0, si))],
        out_shape=[jax.ShapeDtypeStruct((b, s, d), F32), jax.ShapeDtypeStruct((b, s, d), F32),
                   jax.ShapeDtypeStruct((b, TOP_K, s), I32), jax.ShapeDtypeStruct((b, TOP_K, s), F32)],
        compiler_params=_cp("parallel", "parallel"),
        name="outproj_router_" + mix,
    )(*args)


def _row_copy(src_hbm, row, dst, slot, r, sem):
    return pltpu.make_async_copy(src_hbm.at[pl.ds(row, 1), :], dst.at[slot, pl.ds(r, 1), :], sem.at[slot])


def _moe_ffn_kernel(be_ref, tok_cur, tok_next, u_hbm, wg_ref, wu_ref, wd_ref, y_ref, xbuf, sem):
    i = pl.program_id(0)
    n = pl.num_programs(0)
    rows = xbuf.shape[1]
    slot = i % 2

    def issue(tok_ref, s):
        def one(r, carry):
            _row_copy(u_hbm, tok_ref[r], xbuf, s, r, sem).start()
            return carry
        lax.fori_loop(0, rows, one, 0)

    @pl.when(i == 0)
    def _():
        issue(tok_cur, 0)

    @pl.when(i + 1 < n)
    def _():
        issue(tok_next, 1 - slot)

    def wait_one(r, carry):
        _row_copy(u_hbm, 0, xbuf, slot, r, sem).wait()
        return carry
    lax.fori_loop(0, rows, wait_one, 0)

    x = xbuf[slot].astype(BF16)
    hid = _silu(_dot(x, wg_ref[0])) * _dot(x, wu_ref[0])
    y_ref[...] = _dot(hid.astype(BF16), wd_ref[0])


def moe_expert_ffn(u_rows, row_token, blk_expert, w_gate, w_up, w_down):
    t, d = u_rows.shape
    r_total = row_token.shape[0]
    nblk = r_total // MOE_ROWS
    ff = w_gate.shape[2]
    smem_blk = functools.partial(pl.BlockSpec, (MOE_ROWS,), memory_space=pltpu.SMEM)
    return pl.pallas_call(
        _moe_ffn_kernel,
        grid_spec=pltpu.PrefetchScalarGridSpec(
            num_scalar_prefetch=1,
            grid=(nblk,),
            in_specs=[smem_blk(lambda i, be: (i,)),
                      smem_blk(lambda i, be: (jnp.minimum(i + 1, nblk - 1),)),
                      pl.BlockSpec(memory_space=pl.ANY),
                      pl.BlockSpec((1, d, ff), lambda i, be: (be[i], 0, 0)),
                      pl.BlockSpec((1, d, ff), lambda i, be: (be[i], 0, 0)),
                      pl.BlockSpec((1, ff, d), lambda i, be: (be[i], 0, 0))],
            out_specs=pl.BlockSpec((MOE_ROWS, d), lambda i, be: (i, 0)),
            scratch_shapes=[pltpu.VMEM((2, MOE_ROWS, d), F32), pltpu.SemaphoreType.DMA((2,))]),
        out_shape=jax.ShapeDtypeStruct((r_total, d), F32),
        compiler_params=_cp("arbitrary"),
        name="moe_expert_ffn",
    )(blk_expert, row_token, row_token, u_rows, w_gate, w_up, w_down)


def _pair_copy(src_hbm, row, dst, slot, k, r, sem):
    return pltpu.make_async_copy(src_hbm.at[pl.ds(row, 1), :], dst.at[slot, k, pl.ds(r, 1), :], sem.at[slot])


def _moe_combine_kernel(d_cur, d_next, ys_hbm, h_ref, mod_ref, w_ref, o_ref, buf, sem):
    i = pl.program_id(0)
    n = pl.num_programs(0)
    tm = buf.shape[2]
    slot = i % 2

    def issue(d_ref, s):
        def one(r, carry):
            _pair_copy(ys_hbm, d_ref[0, r], buf, s, 0, r, sem).start()
            _pair_copy(ys_hbm, d_ref[1, r], buf, s, 1, r, sem).start()
            return carry
        lax.fori_loop(0, tm, one, 0)

    @pl.when(i == 0)
    def _():
        issue(d_cur, 0)

    @pl.when(i + 1 < n)
    def _():
        issue(d_next, 1 - slot)

    def wait_one(r, carry):
        _pair_copy(ys_hbm, 0, buf, slot, 0, r, sem).wait()
        _pair_copy(ys_hbm, 0, buf, slot, 1, r, sem).wait()
        return carry
    lax.fori_loop(0, tm, wait_one, 0)

    w = w_ref[...]
    y = w[:, 0:1] * buf[slot, 0] + w[:, 1:2] * buf[slot, 1]
    o_ref[...] = h_ref[...] + mod_ref[0, 5:6, :] * y


def moe_combine(ys, dest, h, mod, wts, *, tm=256):
    b, s, d = h.shape
    t = b * s
    tm = min(tm, s)
    per_b = s // tm
    n = t // tm
    smem_blk = functools.partial(pl.BlockSpec, (TOP_K, tm), memory_space=pltpu.SMEM)
    out = pl.pallas_call(
        _moe_combine_kernel,
        grid=(n,),
        in_specs=[smem_blk(lambda i: (0, i)),
                  smem_blk(lambda i: (0, jnp.minimum(i + 1, n - 1))),
                  pl.BlockSpec(memory_space=pl.ANY),
                  pl.BlockSpec((tm, d), lambda i: (i, 0)),
                  pl.BlockSpec((1, 6, d), lambda i: (i // per_b, 0, 0)),
                  pl.BlockSpec((tm, TOP_K), lambda i: (i, 0))],
        out_specs=pl.BlockSpec((tm, d), lambda i: (i, 0)),
        out_shape=jax.ShapeDtypeStruct((t, d), F32),
        scratch_shapes=[pltpu.VMEM((2, TOP_K, tm, d), F32), pltpu.SemaphoreType.DMA((2,))],
        compiler_params=_cp("arbitrary"),
        name="moe_combine",
    )(dest, dest, ys, h.reshape(t, d), mod, wts)
    return out.reshape(b, s, d)


def moe_layer(h, u, eidx, wts, mod, w_gate, w_up, w_down):
    b, s, d = h.shape
    t = b * s
    e_flat = eidx.transpose(0, 2, 1).reshape(-1)
    n_pairs = t * TOP_K
    r_total = n_pairs + N_EXPERTS * MOE_ROWS
    nblk = r_total // MOE_ROWS
    onehot = (e_flat[:, None] == jnp.arange(N_EXPERTS, dtype=I32)[None, :]).astype(I32)
    csum = jnp.cumsum(onehot, axis=0)
    counts = csum[-1]
    rank = jnp.take_along_axis(csum, e_flat[:, None], axis=1)[:, 0] - 1
    padded = (counts + MOE_ROWS - 1) // MOE_ROWS * MOE_ROWS
    pad_end = jnp.cumsum(padded)
    pad_start = pad_end - padded
    dest = (pad_start[e_flat] + rank).astype(I32)
    row_token = jnp.zeros((r_total,), I32).at[dest].set(jnp.arange(n_pairs, dtype=I32) // TOP_K)
    blk_expert = jnp.minimum(jnp.searchsorted(pad_end, jnp.arange(nblk, dtype=I32) * MOE_ROWS, side="right"),
                             N_EXPERTS - 1).astype(I32)
    ys = moe_expert_ffn(u.reshape(t, d), row_token, blk_expert, w_gate, w_up, w_down)
    dest2 = dest.reshape(t, TOP_K).T
    w_tok = wts.transpose(0, 2, 1).reshape(t, TOP_K)
    return moe_combine(ys, dest2, h, mod, w_tok)


def _pad_cols(w, n):
    return jnp.pad(w, ((0, 0), (0, n - w.shape[1])))


def fox_attention(h, mod, gain, w_in, b_f, q_gain, k_gain, cos, sin):
    b, s, _ = h.shape
    n_main = 3 * D_MODEL
    main, tail = norm_proj(h, mod, gain, w_in[:, :n_main].astype(BF16),
                           _pad_cols(w_in[:, n_main:], LANES).astype(BF16))
    gains = jnp.stack([_tile_gain(q_gain, HEAD_DIM ** -0.5), _tile_gain(k_gain)])
    qk = head_prep(main, gains, (0, 1), cos, sin, rope=False)
    tk = min(256, s)
    cum = fox_cumulative_gate(tail, b_f)
    key_bias = cum.reshape(b, N_PAIRS, 2, s // tk, tk).transpose(0, 1, 3, 2, 4)
    return flash_attention(qk, qk, main, q_blk=lambda j: j, k_blk=lambda j: N_PAIRS + j,
                           v_blk=lambda j: 2 * N_PAIRS + j, n_batch=b, n_inner=N_PAIRS,
                           out_cols=D_MODEL, out_blk=lambda j: j, key_bias=key_bias, tk=tk)


def diff_attention(h, mod, gain, w_in, q_gain, k_gain, lambdas, sub_gain, layer_idx, cos, sin):
    b, s, _ = h.shape
    main = norm_proj(h, mod, gain, w_in.astype(BF16))
    gains = jnp.stack([_tile_gain(q_gain, HEAD_DIM ** -0.5), _tile_gain(k_gain)])
    qk = head_prep(main, gains, (0, 1), cos, sin, rope=True)
    lam_init = 0.8 - 0.6 * math.exp(-0.3 * layer_idx)
    lam = jnp.pad(lambdas.astype(F32), ((0, 0), (0, LANES - HEAD_DIM)))
    return flash_attention(qk, qk, main, q_blk=lambda j: j, k_blk=lambda j: N_PAIRS + j,
                           v_blk=lambda j: 2 * N_PAIRS + j, n_batch=b, n_inner=N_PAIRS,
                           out_cols=D_MODEL, out_blk=lambda j: j, fin="diff", lam=lam,
                           sub_gain=sub_gain.astype(F32).reshape(1, LANES), lam_init=lam_init)


def dilated_attention(h, mod, gain, w_in, q_gain, k_gain, cos, sin):
    b, s, _ = h.shape
    ng = len(DIL_PAIRS)
    main = norm_proj(h, mod, gain, w_in.astype(BF16), tn=3 * D_MODEL)
    gq, gk = _tile_gain(q_gain, HEAD_DIM ** -0.5), _tile_gain(k_gain)
    qk = head_prep(main, jnp.stack([gq, gk] * ng), tuple(3 * g + j for g in range(ng) for j in range(2)),
                   cos, sin, rope=True)
    qk_blocks, main_blocks = 2 * ng * N_PAIRS, 3 * ng * N_PAIRS
    outs, lses = [], []
    for g, (window, dil) in enumerate(DIL_PAIRS):
        sd = s // dil
        o, lse = flash_attention(
            qk.reshape(b, sd, -1), qk.reshape(b, sd, -1), main.reshape(b, sd, -1),
            q_blk=lambda j, g=g: (j // N_PAIRS) * qk_blocks + 2 * g * N_PAIRS + j % N_PAIRS,
            k_blk=lambda j, g=g: (j // N_PAIRS) * qk_blocks + (2 * g + 1) * N_PAIRS + j % N_PAIRS,
            v_blk=lambda j, g=g: (j // N_PAIRS) * main_blocks + (3 * g + 2) * N_PAIRS + j % N_PAIRS,
            n_batch=b, n_inner=dil * N_PAIRS, out_cols=dil * D_MODEL, out_blk=lambda j: j,
            window=window // dil + 1, with_lse=True, tq=256, tk=128)
        outs.append(o.reshape(b, s, D_MODEL))
        lses.append(lse.reshape(b, s, D_MODEL))
    return outs + lses


def nsa_attention(h, mod, gain, w_in, q_gain, k_gain, cmp_pos, cmp_w1, cmp_w2, cos, sin, cos_h, sin_h):
    b, s, _ = h.shape
    nblk = s // NSA_BLOCK
    n_main = D_MODEL + 6 * NSA_GROUPS * HEAD_DIM
    main, tail = norm_proj(h, mod, gain, w_in[:, :n_main].astype(BF16),
                           _pad_cols(w_in[:, n_main:], LANES).astype(BF16))
    gains = jnp.zeros((8, LANES), F32)
    gains = gains.at[0].set(jnp.tile(q_gain.astype(F32) * HEAD_DIM ** -0.5, 2))
    gains = gains.at[1].set(jnp.tile(k_gain[1].astype(F32), 2)).at[2].set(jnp.tile(k_gain[2].astype(F32), 2))
    q, ks2, vs2, kw2, vw2 = nsa_prep(main, gains, cos, sin)

    def to_block_rows(col0):
        x = main[:, :, col0:col0 + NSA_GROUPS * HEAD_DIM].reshape(b, nblk, NSA_BLOCK, NSA_GROUPS, HEAD_DIM)
        return x.transpose(0, 3, 1, 2, 4).reshape(b * NSA_GROUPS * nblk, NSA_BLOCK * HEAD_DIM)

    cos_b = jnp.tile(cos_h[NSA_BLOCK - 1::NSA_BLOCK], (1, 2))
    sin_b = jnp.concatenate([-sin_h[NSA_BLOCK - 1::NSA_BLOCK], sin_h[NSA_BLOCK - 1::NSA_BLOCK]], axis=-1)
    kc = nsa_compress(to_block_rows(D_MODEL), cmp_pos[0].reshape(1, -1), cmp_w1[0].astype(BF16),
                      cmp_w2[0].astype(BF16), k_gain[0].astype(F32).reshape(1, HEAD_DIM), cos_b, sin_b, is_key=True)
    vc = nsa_compress(to_block_rows(D_MODEL + NSA_GROUPS * HEAD_DIM), cmp_pos[1].reshape(1, -1),
                      cmp_w1[1].astype(BF16), cmp_w2[1].astype(BF16),
                      k_gain[0].astype(F32).reshape(1, HEAD_DIM), cos_b, sin_b, is_key=False)
    kc = kc.reshape(b, NSA_GROUPS, nblk, HEAD_DIM)
    vc = vc.reshape(b, NSA_GROUPS, nblk, HEAD_DIM)
    pad_rows = ((0, 0), (0, 0), (0, HEAD_DIM - nblk), (0, 0))
    kc = jnp.pad(kc, pad_rows)
    vc = jnp.pad(vc, pad_rows)
    kc2 = jnp.tile(kc, (1, 1, 2, 2)).astype(BF16)
    vc2 = jnp.concatenate([jnp.tile(vc, (1, 1, 1, 2)), jnp.zeros_like(jnp.tile(vc, (1, 1, 1, 2)))],
                          axis=2).astype(BF16)
    o_cmp, q_aug = nsa_compressed_attention(q, kc2, vc2, n_sel=min(NSA_TOPN, nblk))
    common = dict(q_blk=lambda j: j, k_blk=lambda j: j // 2, v_blk=lambda j: j // 2, n_batch=b,
                  n_inner=N_PAIRS, out_cols=D_MODEL, out_blk=lambda j: j, mode="aug")
    o_sel = flash_attention(q_aug, ks2, vs2, **common)
    o_win = flash_attention(q_aug, kw2, vw2, window=NSA_WINDOW, **common)
    return [o_cmp, o_sel, o_win, tail]


def kernel(x, c, fox_w_in, fox_b_f, fox_q_gain, fox_k_gain, fox_w_out, nsa_w_in, nsa_q_gain, nsa_k_gain, nsa_cmp_pos, nsa_cmp_w1, nsa_cmp_w2, nsa_w_out, dil_w_in, dil_q_gain, dil_k_gain, dil_w_out, diff_w_in, diff_q_gain, diff_k_gain, diff_lambda, diff_sub_gain, diff_w_out, norm_gain, ada_w, ada_b, router_w, router_b, moe_w_gate, moe_w_up, moe_w_down):
    b, s, d = x.shape
    depth = norm_gain.shape[0]
    cos, sin, cos_h, sin_h = rope_lane_tables(s)
    mods = ada_modulation(c, ada_w, ada_b).reshape(depth, b, 6, d)
    router_wt = router_w.T.astype(F32)
    router_bb = jnp.broadcast_to(router_b.astype(F32)[:, None], (N_EXPERTS, LANES))
    h = x
    for i in range(depth):
        mod = mods[i]
        kind, j = i % 4, i // 4
        g1 = norm_gain[i, 0]
        if kind == 0:
            attn = [fox_attention(h, mod, g1, fox_w_in[j], fox_b_f[j], fox_q_gain[j], fox_k_gain[j], cos, sin)]
            w_out, mix = fox_w_out[j], "plain"
        elif kind == 1:
            attn = nsa_attention(h, mod, g1, nsa_w_in[j], nsa_q_gain[j], nsa_k_gain[j], nsa_cmp_pos[j],
                                 nsa_cmp_w1[j], nsa_cmp_w2[j], cos, sin, cos_h, sin_h)
            w_out, mix = nsa_w_out[j], "nsa"
        elif kind == 2:
            attn = dilated_attention(h, mod, g1, dil_w_in[j], dil_q_gain[j], dil_k_gain[j], cos, sin)
            w_out, mix = dil_w_out[j], "dil"
        else:
            attn = [diff_attention(h, mod, g1, diff_w_in[j], diff_q_gain[j], diff_k_gain[j], diff_lambda[j],
                                   diff_sub_gain[j], i, cos, sin)]
            w_out, mix = diff_w_out[j], "plain"
        h, u, eidx, wts = outproj_router(attn, h, mod, w_out.astype(BF16), norm_gain[i, 1], router_wt,
                                         router_bb, mix=mix)
        h = moe_layer(h, u, eidx, wts, mod, moe_w_gate[i].astype(BF16), moe_w_up[i].astype(BF16),
                      moe_w_down[i].astype(BF16))
    return h
```

```python
import functools
import math

import jax
import jax.numpy as jnp
from jax import lax
from jax.experimental import pallas as pl
from jax.experimental.pallas import tpu as pltpu

F32 = jnp.float32
BF16 = jnp.bfloat16
I32 = jnp.int32

D_MODEL = 1024
HEAD_DIM = 64
LANES = 128
N_HEADS = D_MODEL // HEAD_DIM
N_PAIRS = D_MODEL // LANES
ROPE_THETA = 10000.0
EPS = 1e-6
NEG_INF = -1e30
TINY = 1e-30
M_INIT = -1e29
LOG2E = 1.4426950408889634
LN2 = 0.6931471805599453
Q_SCALE = HEAD_DIM ** -0.5 * LOG2E

NSA_GROUPS = 4
NSA_HPG = N_HEADS // NSA_GROUPS
NSA_BLOCK = 64
NSA_TOPN = 16
NSA_WINDOW = 512
DIL_PAIRS = ((128, 1), (512, 4), (2048, 16))

N_EXPERTS = 16
N_EXPERT_GROUPS = 4
EXPERTS_PER_GROUP = 4
TOP_K = 2
EXPERT_FF = 512
MOE_ROWS = 256
FLASH_TILE = 512

VMEM_LIMIT = 52 * 1024 * 1024


def _cp(*sem, vmem=VMEM_LIMIT):
    return pltpu.CompilerParams(dimension_semantics=sem, vmem_limit_bytes=vmem)


def _split3(a):
    hi = a.astype(BF16)
    r1 = a - hi.astype(F32)
    mid = r1.astype(BF16)
    lo = (r1 - mid.astype(F32)).astype(BF16)
    return hi, mid, lo


def _dot(a, b):
    return jnp.dot(a, b, preferred_element_type=F32)


def _dot_nt(a, b):
    return lax.dot_general(a, b, (((1,), (1,)), ((), ())), preferred_element_type=F32)


def _dot_f32(a, b, dot=_dot):
    ah, am, al = _split3(a)
    bh, bm, bl = _split3(b)
    return (dot(ah, bh) + (dot(ah, bm) + dot(am, bh))
            + (dot(ah, bl) + dot(al, bh) + dot(am, bm)))


def _dot_f32_exact_rhs(a, b_bf16):
    ah, am, al = _split3(a)
    return _dot(ah, b_bf16) + _dot(am, b_bf16) + _dot(al, b_bf16)


def _sigmoid(x):
    return 1.0 / (1.0 + jnp.exp(-x))


def _silu(x):
    return x * _sigmoid(x)


def _ada_kernel(c_ref, w_ref, b_ref, o_ref):
    c = c_ref[...]
    o_ref[0] = _dot_f32(_silu(c), w_ref[0]) + b_ref[0]


def ada_modulation(c, ada_w, ada_b):
    depth, d, n = ada_w.shape
    b = c.shape[0]
    tn = 1024
    return pl.pallas_call(
        _ada_kernel,
        grid=(depth, n // tn),
        in_specs=[pl.BlockSpec((b, d), lambda i, j: (0, 0)),
                  pl.BlockSpec((1, d, tn), lambda i, j: (i, 0, j)),
                  pl.BlockSpec((1, 1, tn), lambda i, j: (i, 0, j))],
        out_specs=pl.BlockSpec((1, b, tn), lambda i, j: (i, 0, j)),
        out_shape=jax.ShapeDtypeStruct((depth, b, n), F32),
        compiler_params=_cp("parallel", "parallel"),
        name="ada_modulation",
    )(c, ada_w, ada_b.reshape(depth, 1, n))


def _modulated_norm(x, gain, mod, shift_row, scale_row):
    ms = jnp.mean(x * x, axis=-1, keepdims=True)
    y = x * lax.rsqrt(ms + EPS) * gain
    return y * (1.0 + mod[scale_row:scale_row + 1, :]) + mod[shift_row:shift_row + 1, :]


def _norm_proj_kernel(h_ref, mod_ref, g_ref, w_ref, *rest, has_tail):
    u = _modulated_norm(h_ref[0], g_ref[...], mod_ref[0], 0, 1).astype(BF16)
    if has_tail:
        wt_ref, main_ref, tail_ref = rest
        tail_ref[0] = _dot(u, wt_ref[...])
    else:
        (main_ref,) = rest
    main_ref[0] = _dot(u, w_ref[...]).astype(main_ref.dtype)


def norm_proj(h, mod, gain, w_main, w_tail=None, *, tn=None, tm=512):
    b, s, d = h.shape
    n = w_main.shape[1]
    tn = tn or n
    tm = min(tm, s)
    in_specs = [pl.BlockSpec((1, tm, d), lambda j, bi, si: (bi, si, 0)),
                pl.BlockSpec((1, 6, d), lambda j, bi, si: (bi, 0, 0)),
                pl.BlockSpec((1, d), lambda j, bi, si: (0, 0)),
                pl.BlockSpec((d, tn), lambda j, bi, si: (0, j))]
    out_specs = [pl.BlockSpec((1, tm, tn), lambda j, bi, si: (bi, si, j))]
    out_shape = [jax.ShapeDtypeStruct((b, s, n), BF16)]
    args = [h, mod, gain.reshape(1, d), w_main]
    if w_tail is not None:
        in_specs.append(pl.BlockSpec((d, LANES), lambda j, bi, si: (0, 0)))
        out_specs.append(pl.BlockSpec((1, tm, LANES), lambda j, bi, si: (bi, si, 0)))
        out_shape.append(jax.ShapeDtypeStruct((b, s, LANES), F32))
        args.append(w_tail)
    outs = pl.pallas_call(
        functools.partial(_norm_proj_kernel, has_tail=w_tail is not None),
        grid=(n // tn, b, s // tm),
        in_specs=in_specs, out_specs=out_specs, out_shape=out_shape,
        compiler_params=_cp("parallel", "parallel", "parallel"),
        name="norm_proj",
    )(*args)
    return outs if w_tail is not None else outs[0]


def _lane_iota(rows):
    return lax.broadcasted_iota(I32, (rows, LANES), 1)


def _head_block_diag():
    r = lax.broadcasted_iota(I32, (LANES, LANES), 0) // HEAD_DIM
    c = lax.broadcasted_iota(I32, (LANES, LANES), 1) // HEAD_DIM
    return (r == c).astype(BF16)


def _head_norm_rope(x, gain, cos, sin, bd, rope):
    y = x * x
    hi = y.astype(BF16)
    lo = (y - hi.astype(F32)).astype(BF16)
    seg = _dot(hi, bd) + _dot(lo, bd)
    xn = x * lax.rsqrt(seg * (1.0 / HEAD_DIM) + EPS) * gain
    if rope:
        first_half = (_lane_iota(x.shape[0]) % HEAD_DIM) < HEAD_DIM // 2
        partner = jnp.where(first_half, pltpu.roll(xn, LANES - HEAD_DIM // 2, 1),
                            pltpu.roll(xn, HEAD_DIM // 2, 1))
        xn = xn * cos + partner * sin
    return xn


def _prep_kernel(cb_ref, x_ref, g_ref, cos_ref, sin_ref, o_ref, *, rope):
    del cb_ref
    bd = _head_block_diag()
    cos = cos_ref[...]
    sin = sin_ref[...]
    for c in range(x_ref.shape[2] // LANES):
        sl = slice(c * LANES, (c + 1) * LANES)
        x = x_ref[0, :, sl].astype(F32)
        o_ref[0, :, sl] = _head_norm_rope(x, g_ref[0, :, sl], cos, sin, bd, rope).astype(o_ref.dtype)


def head_prep(src, gains, col_blocks, cos, sin, *, rope, ts=512):
    b, s, _ = src.shape
    n = len(col_blocks)
    ts = min(ts, s)
    cb = jnp.asarray(col_blocks, I32)

    def x_map(bi, si, ci, cb_ref):
        return (bi, si, cb_ref[ci])

    return pl.pallas_call(
        functools.partial(_prep_kernel, rope=rope),
        grid_spec=pltpu.PrefetchScalarGridSpec(
            num_scalar_prefetch=1,
            grid=(b, s // ts, n),
            in_specs=[pl.BlockSpec((1, ts, D_MODEL), x_map),
                      pl.BlockSpec((1, 1, D_MODEL), lambda bi, si, ci, cb_ref: (ci, 0, 0)),
                      pl.BlockSpec((ts, LANES), lambda bi, si, ci, cb_ref: (si, 0)),
                      pl.BlockSpec((ts, LANES), lambda bi, si, ci, cb_ref: (si, 0))],
            out_specs=pl.BlockSpec((1, ts, D_MODEL), lambda bi, si, ci, cb_ref: (bi, si, ci))),
        out_shape=jax.ShapeDtypeStruct((b, s, n * D_MODEL), BF16),
        compiler_params=_cp("parallel", "parallel", "arbitrary"),
        name="head_prep",
    )(cb, src, gains, cos, sin)


def rope_lane_tables(s):
    inv = ROPE_THETA ** (-jnp.arange(0, HEAD_DIM, 2, dtype=F32) / HEAD_DIM)
    ang = jnp.arange(s).astype(F32)[:, None] * inv[None, :]
    cos, sin = jnp.cos(ang), jnp.sin(ang)
    return jnp.tile(cos, (1, 4)), jnp.tile(jnp.concatenate([-sin, sin], axis=-1), (1, 2)), cos, sin


def _tile_gain(g, scale=1.0):
    return jnp.tile(g.astype(F32) * scale, N_HEADS).reshape(1, D_MODEL)


def _fox_cum_kernel(f_ref, b_ref, o_ref, carry_ref):
    si = pl.program_id(1)
    ts = f_ref.shape[1]

    @pl.when(si == 0)
    def _():
        carry_ref[...] = jnp.zeros_like(carry_ref)

    z = f_ref[0] + b_ref[...]
    log_f = -(jnp.maximum(-z, 0.0) + jnp.log1p(jnp.exp(-jnp.abs(z))))
    r = lax.broadcasted_iota(I32, (ts, ts), 0)
    c = lax.broadcasted_iota(I32, (ts, ts), 1)
    upper = (r <= c).astype(BF16)
    cum = _dot_f32_exact_rhs(log_f.T, upper) + carry_ref[:, 0:1]
    o_ref[0] = cum[0:N_HEADS, :] * LOG2E
    carry_ref[...] = jnp.broadcast_to(cum[:, ts - 1:ts], carry_ref.shape)


def fox_cumulative_gate(tail, b_f, *, ts=256):
    b, s, _ = tail.shape
    ts = min(ts, s)
    bias = jnp.zeros((1, LANES), F32).at[0, :N_HEADS].set(b_f.astype(F32))
    return pl.pallas_call(
        _fox_cum_kernel,
        grid=(b, s // ts),
        in_specs=[pl.BlockSpec((1, ts, LANES), lambda bi, si: (bi, si, 0)),
                  pl.BlockSpec((1, LANES), lambda bi, si: (0, 0))],
        out_specs=pl.BlockSpec((1, N_HEADS, ts), lambda bi, si: (bi, 0, si)),
        out_shape=jax.ShapeDtypeStruct((b, N_HEADS, s), F32),
        scratch_shapes=[pltpu.VMEM((LANES, LANES), F32)],
        compiler_params=_cp("parallel", "arbitrary"),
        name="fox_cumulative_gate",
    )(tail, bias)


def _flash_kernel(*refs, tq, tk, window, mode, fin, has_bias, has_lse, lam_init):
    it = iter(refs)
    q_ref, k_ref, v_ref = next(it), next(it), next(it)
    kb_ref = next(it) if has_bias else None
    if fin == "diff":
        lam_ref, sg_ref = next(it), next(it)
    o_ref = next(it)
    lse_ref = next(it) if has_lse else None
    m_sc, l_sc, acc_sc = next(it), next(it), next(it)

    q_start = pl.program_id(2) * tq
    lane = _lane_iota(tq)
    low_half = lane < HEAD_DIM
    if mode == "pair":
        q = q_ref[0]
        zero = jnp.zeros_like(q)
        qh = (jnp.where(low_half, q, zero), jnp.where(low_half, zero, q))
    else:
        qh = (q_ref[0, :, 0:LANES], q_ref[0, :, LANES:2 * LANES])

    m_sc[...] = jnp.full(m_sc.shape, M_INIT, F32)
    l_sc[...] = jnp.zeros(l_sc.shape, F32)
    acc_sc[...] = jnp.zeros(acc_sc.shape, F32)
    row = q_start + lax.broadcasted_iota(I32, (tq, LANES), 0)
    n_chunk = tk // LANES

    def step(kv, masked):
        ks = pl.multiple_of(kv * tk, tk)
        kblk = k_ref[0, pl.ds(ks, tk), :]
        vblk = v_ref[0, pl.ds(ks, tk), :]
        if masked:
            masks = []
            for c in range(n_chunk):
                col = ks + c * LANES + lane
                mk = col <= row
                if window:
                    mk = mk & ((row - col) < window)
                masks.append(mk)
        if has_bias:
            kb = kb_ref[0, 0, kv]
        for h in range(2):
            kk = kblk if mode == "pair" else kblk[:, h * LANES:(h + 1) * LANES]
            s = _dot_nt(qh[h], kk)
            if has_bias:
                s = s - kb[h:h + 1, :]
            chunks = [s[:, c * LANES:(c + 1) * LANES] for c in range(n_chunk)]
            if masked:
                chunks = [jnp.where(mk, ch, -jnp.inf) for mk, ch in zip(masks, chunks)]
            mb = functools.reduce(jnp.maximum, chunks)
            m_old = m_sc[h]
            m_new = jnp.maximum(m_old, jnp.broadcast_to(jnp.max(mb, axis=-1, keepdims=True), (tq, LANES)))
            alpha = jnp.exp2(m_old - m_new)
            ps = [jnp.exp2(ch - m_new) for ch in chunks]
            l_sc[h] = alpha * l_sc[h] + functools.reduce(jnp.add, ps)
            p = ps[0] if n_chunk == 1 else jnp.concatenate(ps, axis=1)
            acc_sc[h] = alpha * acc_sc[h] + _dot(p.astype(BF16), vblk)
            m_sc[h] = m_new

    def loop(lo, hi, masked):
        def body(kv, carry):
            step(kv, masked)
            return carry
        lax.fori_loop(lo, hi, body, 0)

    last_blk = (q_start + (tq - 1)) // tk
    full_hi = (q_start + 1) // tk
    if window:
        first_blk = jnp.maximum(q_start - (window - 1), 0) // tk
        full_lo = jnp.maximum(q_start + (tq - 1) - window + tk, 0) // tk
        full_lo = jnp.maximum(jnp.minimum(full_lo, full_hi), first_blk)
        loop(first_blk, full_lo, True)
    else:
        full_lo = 0
    loop(full_lo, full_hi, False)
    loop(jnp.maximum(full_hi, full_lo), last_blk + 1, True)

    l0 = jnp.maximum(jnp.sum(l_sc[0], axis=-1, keepdims=True), TINY)
    l1 = jnp.maximum(jnp.sum(l_sc[1], axis=-1, keepdims=True), TINY)
    o0 = acc_sc[0] * (1.0 / l0)
    o1 = acc_sc[1] * (1.0 / l1)
    if fin == "select":
        o = jnp.where(low_half, o0, o1)
    else:
        lam_rows = lam_ref[...]
        lam = (jnp.exp(jnp.sum(lam_rows[0:1] * lam_rows[1:2], axis=-1, keepdims=True))
               - jnp.exp(jnp.sum(lam_rows[2:3] * lam_rows[3:4], axis=-1, keepdims=True)) + lam_init)
        o = o0 - lam * o1
        ms = jnp.mean(o * o, axis=-1, keepdims=True)
        o = o * lax.rsqrt(ms + EPS) * sg_ref[...] * (1.0 - lam_init)
    o_ref[0] = o.astype(o_ref.dtype)
    if has_lse:
        lse0 = m_sc[0] * LN2 + jnp.log(l0)
        lse1 = m_sc[1] * LN2 + jnp.log(l1)
        lse_ref[0] = jnp.where(low_half, lse0, lse1)


def flash_attention(q, k, v, *, q_blk, k_blk, v_blk, n_batch, n_inner, out_cols, out_blk,
                    mode="pair", fin="select", window=0, key_bias=None, lam=None, sub_gain=None,
                    lam_init=0.0, with_lse=False, tq=FLASH_TILE, tk=FLASH_TILE):
    nb, s, _ = q.shape
    tq, tk = min(tq, s), min(tk, s)
    wq = LANES if mode == "pair" else 2 * LANES
    in_specs = [pl.BlockSpec((1, tq, wq), lambda b, j, i: (b, i, q_blk(j))),
                pl.BlockSpec((1, s, wq), lambda b, j, i: (b, 0, k_blk(j))),
                pl.BlockSpec((1, s, LANES), lambda b, j, i: (b, 0, v_blk(j)))]
    args = [q, k, v]
    if key_bias is not None:
        in_specs.append(pl.BlockSpec((1, 1, s // tk, 2, tk), lambda b, j, i: (b, j, 0, 0, 0)))
        args.append(key_bias)
    if fin == "diff":
        in_specs += [pl.BlockSpec((4, LANES), lambda b, j, i: (0, 0)),
                     pl.BlockSpec((1, LANES), lambda b, j, i: (0, 0))]
        args += [lam, sub_gain]
    out_specs = [pl.BlockSpec((1, tq, LANES), lambda b, j, i: (b, i, out_blk(j)))]
    out_shape = [jax.ShapeDtypeStruct((nb, s, out_cols), BF16)]
    if with_lse:
        out_specs.append(pl.BlockSpec((1, tq, LANES), lambda b, j, i: (b, i, out_blk(j))))
        out_shape.append(jax.ShapeDtypeStruct((nb, s, out_cols), F32))
    outs = pl.pallas_call(
        functools.partial(_flash_kernel, tq=tq, tk=tk, window=window, mode=mode, fin=fin,
                          has_bias=key_bias is not None, has_lse=with_lse, lam_init=lam_init),
        grid=(n_batch, n_inner, s // tq),
        in_specs=in_specs, out_specs=out_specs, out_shape=out_shape,
        scratch_shapes=[pltpu.VMEM((2, tq, LANES), F32)] * 3,
        compiler_params=_cp("parallel", "parallel", "arbitrary"),
        name="flash_" + mode + "_" + fin,
    )(*args)
    return outs if with_lse else outs[0]


BAND_SUB = LANES


def _band_kernel(q_ref, k_ref, v_ref, o_ref, lse_ref, *, tq, span, window):
    seq = k_ref.shape[1]
    sub = min(BAND_SUB, tq)
    q_start = pl.program_id(2) * tq
    lane = _lane_iota(sub)
    low_half = lane < HEAD_DIM
    low_half_kv = _lane_iota(span) < HEAD_DIM
    for r in range(tq // sub):
        q0 = q_start + r * sub
        ks = pl.multiple_of(jnp.minimum(jnp.maximum(q0 - sub, 0), seq - span), sub)
        kblk = k_ref[0, pl.ds(ks, span), :]
        vblk = v_ref[0, pl.ds(ks, span), :]
        ones = jnp.ones_like(vblk)
        q = q_ref[0, r * sub:(r + 1) * sub, :]
        zero = jnp.zeros_like(q)
        row = q0 + lax.broadcasted_iota(I32, (sub, LANES), 0)
        masks = []
        for c in range(span // LANES):
            col = ks + c * LANES + lane
            masks.append((col <= row) & ((row - col) < window))
        outs, lses = [], []
        for h in range(2):
            own = low_half if h == 0 else jnp.logical_not(low_half)
            own_kv = low_half_kv if h == 0 else jnp.logical_not(low_half_kv)
            s = _dot_nt(jnp.where(own, q, zero), kblk)
            chunks = [jnp.where(mk, s[:, c * LANES:(c + 1) * LANES], -jnp.inf) for c, mk in enumerate(masks)]
            m = jnp.broadcast_to(jnp.max(functools.reduce(jnp.maximum, chunks), axis=-1, keepdims=True),
                                 (sub, LANES))
            ps = [jnp.exp2(ch - m) for ch in chunks]
            p = ps[0] if len(ps) == 1 else jnp.concatenate(ps, axis=1)
            acc = _dot(p.astype(BF16), jnp.where(own_kv, vblk, ones))
            l_own = jnp.maximum(pltpu.roll(acc, HEAD_DIM, 1), TINY)
            outs.append(acc * (1.0 / l_own))
            lses.append(m * LN2 + jnp.log(l_own))
        o_ref[0, r * sub:(r + 1) * sub, :] = jnp.where(low_half, outs[0], outs[1]).astype(o_ref.dtype)
        lse_ref[0, r * sub:(r + 1) * sub, :] = jnp.where(low_half, lses[0], lses[1])


def band_attention(q, k, v, *, q_blk, k_blk, v_blk, n_batch, n_inner, out_cols, out_blk, window, tq=512):
    nb, s, _ = q.shape
    tq = min(tq, s)
    span = min(2 * BAND_SUB, s)
    assert window <= span - min(BAND_SUB, tq) + 1 or span == s
    blk = lambda f, rows: pl.BlockSpec((1, rows, LANES), f)
    return pl.pallas_call(
        functools.partial(_band_kernel, tq=tq, span=span, window=window),
        grid=(n_batch, n_inner, s // tq),
        in_specs=[blk(lambda b, j, i: (b, i, q_blk(j)), tq),
                  blk(lambda b, j, i: (b, 0, k_blk(j)), s),
                  blk(lambda b, j, i: (b, 0, v_blk(j)), s)],
        out_specs=[blk(lambda b, j, i: (b, i, out_blk(j)), tq), blk(lambda b, j, i: (b, i, out_blk(j)), tq)],
        out_shape=[jax.ShapeDtypeStruct((nb, s, out_cols), BF16), jax.ShapeDtypeStruct((nb, s, out_cols), F32)],
        compiler_params=_cp("parallel", "parallel", "parallel"),
        name="band_attention",
    )(q, k, v)


def _nsa_prep_kernel(q_ref, ks_ref, vs_ref, kw_ref, vw_ref, g_ref, cos_ref, sin_ref,
                     qo_ref, ks2_ref, vs2_ref, kw2_ref, vw2_ref):
    ts = q_ref.shape[1]
    bd = _head_block_diag()
    cos, sin = cos_ref[...], sin_ref[...]
    lane = _lane_iota(ts)
    low_half = lane < HEAD_DIM
    for c in range(N_PAIRS):
        sl = slice(c * LANES, (c + 1) * LANES)
        qo_ref[0, :, sl] = _head_norm_rope(q_ref[0, :, sl].astype(F32), g_ref[0:1, :], cos, sin, bd,
                                           True).astype(qo_ref.dtype)
    t = pl.program_id(1) * ts + lax.broadcasted_iota(I32, (ts, LANES), 0)
    blk_onehot = ((t // NSA_BLOCK) == (lane % HEAD_DIM)).astype(F32)
    zeros = jnp.zeros((ts, LANES), F32)

    def spread(x, fill, out_ref, c):
        xr = pltpu.roll(x, HEAD_DIM, 1)
        base = 2 * c * 2 * LANES
        out_ref[0, :, base:base + LANES] = jnp.where(low_half, x, fill).astype(out_ref.dtype)
        out_ref[0, :, base + LANES:base + 2 * LANES] = jnp.where(low_half, fill, xr).astype(out_ref.dtype)
        out_ref[0, :, base + 2 * LANES:base + 3 * LANES] = jnp.where(low_half, xr, fill).astype(out_ref.dtype)
        out_ref[0, :, base + 3 * LANES:base + 4 * LANES] = jnp.where(low_half, fill, x).astype(out_ref.dtype)

    def dup(x, out_ref, c):
        xr = pltpu.roll(x, HEAD_DIM, 1)
        out_ref[0, :, 2 * c * LANES:(2 * c + 1) * LANES] = jnp.where(low_half, x, xr).astype(out_ref.dtype)
        out_ref[0, :, (2 * c + 1) * LANES:(2 * c + 2) * LANES] = jnp.where(low_half, xr, x).astype(out_ref.dtype)

    for c in range(NSA_GROUPS // 2):
        sl = slice(c * LANES, (c + 1) * LANES)
        ks = _head_norm_rope(ks_ref[0, :, sl].astype(F32), g_ref[1:2, :], cos, sin, bd, True)
        kw = _head_norm_rope(kw_ref[0, :, sl].astype(F32), g_ref[2:3, :], cos, sin, bd, True)
        spread(ks, blk_onehot, ks2_ref, c)
        spread(kw, zeros, kw2_ref, c)
        dup(vs_ref[0, :, sl].astype(F32), vs2_ref, c)
        dup(vw_ref[0, :, sl].astype(F32), vw2_ref, c)


def nsa_prep(main, gains, cos, sin, *, ts=512):
    b, s, _ = main.shape
    ts = min(ts, s)
    gw = NSA_GROUPS * HEAD_DIM

    def kv_spec(i):
        return pl.BlockSpec((1, ts, gw), lambda bi, si: (bi, si, i))

    return pl.pallas_call(
        _nsa_prep_kernel,
        grid=(b, s // ts),
        in_specs=[pl.BlockSpec((1, ts, D_MODEL), lambda bi, si: (bi, si, 0)),
                  kv_spec(6), kv_spec(7), kv_spec(8), kv_spec(9),
                  pl.BlockSpec((8, LANES), lambda bi, si: (0, 0)),
                  pl.BlockSpec((ts, LANES), lambda bi, si: (si, 0)),
                  pl.BlockSpec((ts, LANES), lambda bi, si: (si, 0))],
        out_specs=[pl.BlockSpec((1, ts, D_MODEL), lambda bi, si: (bi, si, 0)),
                   pl.BlockSpec((1, ts, NSA_GROUPS * 2 * LANES), lambda bi, si: (bi, si, 0)),
                   pl.BlockSpec((1, ts, NSA_GROUPS * LANES), lambda bi, si: (bi, si, 0)),
                   pl.BlockSpec((1, ts, NSA_GROUPS * 2 * LANES), lambda bi, si: (bi, si, 0)),
                   pl.BlockSpec((1, ts, NSA_GROUPS * LANES), lambda bi, si: (bi, si, 0))],
        out_shape=[jax.ShapeDtypeStruct((b, s, D_MODEL), BF16),
                   jax.ShapeDtypeStruct((b, s, NSA_GROUPS * 2 * LANES), BF16),
                   jax.ShapeDtypeStruct((b, s, NSA_GROUPS * LANES), BF16),
                   jax.ShapeDtypeStruct((b, s, NSA_GROUPS * 2 * LANES), BF16),
                   jax.ShapeDtypeStruct((b, s, NSA_GROUPS * LANES), BF16)],
        compiler_params=_cp("parallel", "parallel"),
        name="nsa_prep",
    )(main, main, main, main, main, gains, cos, sin)


def _nsa_compress_kernel(x_ref, pos_ref, w1_ref, w2_ref, g_ref, cos_ref, sin_ref, o_ref, *, is_key):
    x = (x_ref[...].astype(F32) + pos_ref[...]).astype(BF16)
    hid = _silu(_dot(x, w1_ref[...]))
    y = _dot(hid.astype(BF16), w2_ref[...])
    if is_key:
        ms = jnp.mean(y * y, axis=-1, keepdims=True)
        y = y * lax.rsqrt(ms + EPS) * g_ref[...]
        r = lax.broadcasted_iota(I32, (HEAD_DIM, HEAD_DIM), 0)
        c = lax.broadcasted_iota(I32, (HEAD_DIM, HEAD_DIM), 1)
        swap = (((r + HEAD_DIM // 2) % HEAD_DIM) == c).astype(BF16)
        y = y * cos_ref[...] + _dot_f32_exact_rhs(y, swap) * sin_ref[...]
    o_ref[...] = y


def nsa_compress(x, pos, w1, w2, gain, cos_blk, sin_blk, *, is_key):
    rows, k = x.shape
    nb = cos_blk.shape[0]
    hid = w1.shape[1]
    return pl.pallas_call(
        functools.partial(_nsa_compress_kernel, is_key=is_key),
        grid=(rows // nb,),
        in_specs=[pl.BlockSpec((nb, k), lambda i: (i, 0)),
                  pl.BlockSpec((1, k), lambda i: (0, 0)),
                  pl.BlockSpec((k, hid), lambda i: (0, 0)),
                  pl.BlockSpec((hid, HEAD_DIM), lambda i: (0, 0)),
                  pl.BlockSpec((1, HEAD_DIM), lambda i: (0, 0)),
                  pl.BlockSpec((nb, HEAD_DIM), lambda i: (0, 0)),
                  pl.BlockSpec((nb, HEAD_DIM), lambda i: (0, 0))],
        out_specs=pl.BlockSpec((nb, HEAD_DIM), lambda i: (i, 0)),
        out_shape=jax.ShapeDtypeStruct((rows, HEAD_DIM), F32),
        compiler_params=_cp("parallel"),
        name="nsa_compress",
    )(x, pos, w1, w2, gain, cos_blk, sin_blk)


def _nsa_cmp_kernel(q_ref, kc_ref, vc_ref, o_ref, qa_ref, *, n_sel):
    tq = q_ref.shape[1]
    lane = _lane_iota(tq)
    low_half = lane < HEAD_DIM
    blk = lane % HEAD_DIM
    t = pl.program_id(2) * tq + lax.broadcasted_iota(I32, (tq, LANES), 0)
    cmask = (blk + 1) * NSA_BLOCK <= t + 1
    kc = kc_ref[0, 0]
    vc = vc_ref[0, 0]
    imp = jnp.zeros((tq, LANES), F32)
    qblk = [q_ref[0, :, 0:LANES], q_ref[0, :, LANES:2 * LANES]]
    zero = jnp.zeros_like(qblk[0])
    outs = [None, None]
    for p in range(NSA_HPG):
        in_low = (p % 2) == 0
        qb = qblk[p // 2]
        qm = jnp.where(low_half, qb, zero) if in_low else jnp.where(low_half, zero, qb)
        s = jnp.where(cmask, _dot_nt(qm, kc), NEG_INF)
        m = jnp.max(s, axis=-1, keepdims=True)
        e = jnp.where(cmask, jnp.exp2(s - m), 0.0)
        den = jnp.maximum(0.5 * jnp.sum(e, axis=-1, keepdims=True), TINY)
        pc = e / den
        imp = imp + pc
        o = _dot(pc.astype(BF16), vc)
        prev = outs[p // 2]
        outs[p // 2] = o if prev is None else jnp.where(low_half, prev, o)
    o_ref[0, :, 0:LANES] = outs[0].astype(o_ref.dtype)
    o_ref[0, :, LANES:2 * LANES] = outs[1].astype(o_ref.dtype)

    cur = t // NSA_BLOCK
    forced = (blk == 0) | (blk == cur) | (blk == cur - 1)
    x = jnp.where(blk > cur, -1.0, jnp.where(forced, NSA_HPG + 1.0, imp))
    blk_f = blk.astype(F32)
    selected = jnp.zeros((tq, LANES), jnp.bool_)
    for _ in range(n_sel):
        mx = jnp.max(x, axis=-1, keepdims=True)
        first = jnp.min(jnp.where(x == mx, blk_f, float(LANES)), axis=-1, keepdims=True)
        hit = blk_f == first
        selected = selected | hit
        x = jnp.where(hit, -2.0, x)
    sel_bias = jnp.where(selected, 0.0, NEG_INF).astype(qa_ref.dtype)
    for p in range(NSA_HPG):
        qb = qblk[p // 2]
        qa = jnp.where(low_half, qb, sel_bias) if p % 2 == 0 else jnp.where(low_half, sel_bias, qb)
        qa_ref[0, :, p * LANES:(p + 1) * LANES] = qa


def nsa_compressed_attention(q, kc2, vc2, *, n_sel, tq=256):
    b, s, _ = q.shape
    tq = min(tq, s)
    gq = NSA_HPG * HEAD_DIM
    return pl.pallas_call(
        functools.partial(_nsa_cmp_kernel, n_sel=n_sel),
        grid=(b, NSA_GROUPS, s // tq),
        in_specs=[pl.BlockSpec((1, tq, gq), lambda bi, g, i: (bi, i, g)),
                  pl.BlockSpec((1, 1, LANES, LANES), lambda bi, g, i: (bi, g, 0, 0)),
                  pl.BlockSpec((1, 1, LANES, LANES), lambda bi, g, i: (bi, g, 0, 0))],
        out_specs=[pl.BlockSpec((1, tq, gq), lambda bi, g, i: (bi, i, g)),
                   pl.BlockSpec((1, tq, 2 * gq), lambda bi, g, i: (bi, i, g))],
        out_shape=[jax.ShapeDtypeStruct((b, s, D_MODEL), BF16),
                   jax.ShapeDtypeStruct((b, s, 2 * D_MODEL), BF16)],
        compiler_params=_cp("parallel", "parallel", "parallel"),
        name="nsa_compressed_attention",
    )(q, kc2, vc2)


def _route(logits_t, rb):
    scores = _sigmoid(logits_t)
    sel = scores + rb
    rows = [sel[i:i + 1, :] for i in range(N_EXPERTS)]
    srows = [scores[i:i + 1, :] for i in range(N_EXPERTS)]
    best = grp = None
    for g in range(N_EXPERT_GROUPS):
        a, b, c, d = rows[4 * g:4 * g + 4]
        hi1, lo1, hi2, lo2 = jnp.maximum(a, b), jnp.minimum(a, b), jnp.maximum(c, d), jnp.minimum(c, d)
        gs = jnp.maximum(hi1, hi2) + jnp.maximum(jnp.minimum(hi1, hi2), jnp.maximum(lo1, lo2))
        if g == 0:
            best, grp = gs, jnp.zeros(gs.shape, I32)
        else:
            better = gs > best
            grp = jnp.where(better, g, grp)
            best = jnp.where(better, gs, best)

    def pick(vals, i):
        out = vals[i]
        for g in range(1, N_EXPERT_GROUPS):
            out = jnp.where(grp == g, vals[4 * g + i], out)
        return out

    v = [pick(rows, i) for i in range(EXPERTS_PER_GROUP)]
    w = [pick(srows, i) for i in range(EXPERTS_PER_GROUP)]
    l1, b1, w1 = jnp.zeros(grp.shape, I32), v[0], w[0]
    for i in range(1, EXPERTS_PER_GROUP):
        better = v[i] > b1
        l1 = jnp.where(better, i, l1)
        b1 = jnp.where(better, v[i], b1)
        w1 = jnp.where(better, w[i], w1)
    have = jnp.zeros(grp.shape, jnp.bool_)
    l2, b2, w2 = jnp.zeros(grp.shape, I32), jnp.zeros_like(b1), jnp.zeros_like(w1)
    for i in range(EXPERTS_PER_GROUP):
        valid = l1 != i
        better = valid & (jnp.logical_not(have) | (v[i] > b2))
        l2 = jnp.where(better, i, l2)
        b2 = jnp.where(better, v[i], b2)
        w2 = jnp.where(better, w[i], w2)
        have = have | valid
    wsum = w1 + w2
    return (grp * EXPERTS_PER_GROUP + l1, grp * EXPERTS_PER_GROUP + l2), (w1 / wsum, w2 / wsum)


def _outproj_router_kernel(*refs, mix):
    it = iter(refs)
    if mix == "plain":
        x = next(it)[0]
    elif mix == "nsa":
        o_refs = [next(it), next(it), next(it)]
        gl = _sigmoid(next(it)[0])
        g_hi = gl.astype(BF16)
        g_lo = (gl - g_hi.astype(F32)).astype(BF16)
        r = lax.broadcasted_iota(I32, (LANES, D_MODEL), 0)
        c = lax.broadcasted_iota(I32, (LANES, D_MODEL), 1) // HEAD_DIM
        x = None
        for i in range(3):
            expand = (r == c + i * N_HEADS).astype(BF16)
            term = (_dot(g_hi, expand) + _dot(g_lo, expand)) * o_refs[i][0].astype(F32)
            x = term if x is None else x + term
        x = x.astype(BF16)
    else:
        o_refs = [next(it), next(it), next(it)]
        lse = [next(it)[0], next(it)[0], next(it)[0]]
        mx = jnp.maximum(jnp.maximum(lse[0], lse[1]), lse[2])
        ex = [jnp.exp(l - mx) for l in lse]
        den = ex[0] + ex[1] + ex[2]
        x = ((ex[0] / den) * o_refs[0][0].astype(F32) + (ex[1] / den) * o_refs[1][0].astype(F32)
             + (ex[2] / den) * o_refs[2][0].astype(F32)).astype(BF16)
    h_ref, mod_ref, w_ref, g2_ref, rw_ref, rb_ref = (next(it) for _ in range(6))
    ho_ref, u_ref, e_ref, wt_ref = (next(it) for _ in range(4))
    mod = mod_ref[0]
    h_new = h_ref[0] + mod[2:3, :] * _dot(x, w_ref[...])
    ho_ref[0] = h_new
    u = _modulated_norm(h_new, g2_ref[...], mod, 3, 4)
    u_ref[0] = u
    logits_t = _dot_f32(rw_ref[...], u, dot=_dot_nt)
    eidx, wts = _route(logits_t, rb_ref[:, 0:1])
    for k in range(TOP_K):
        e_ref[0, k:k + 1, :] = eidx[k]
        wt_ref[0, k:k + 1, :] = wts[k]


def outproj_router(attn_inputs, h, mod, w_out, gain2, router_wt, router_b, *, mix, tm=256):
    b, s, d = h.shape
    tm = min(tm, s)
    row_spec = pl.BlockSpec((1, tm, d), lambda bi, si: (bi, si, 0))
    in_specs, args = [], []
    for a in attn_inputs:
        in_specs.append(pl.BlockSpec((1, tm, a.shape[2]), lambda bi, si: (bi, si, 0)))
        args.append(a)
    in_specs += [row_spec,
                 pl.BlockSpec((1, 6, d), lambda bi, si: (bi, 0, 0)),
                 pl.BlockSpec((d, d), lambda bi, si: (0, 0)),
                 pl.BlockSpec((1, d), lambda bi, si: (0, 0)),
                 pl.BlockSpec((N_EXPERTS, d), lambda bi, si: (0, 0)),
                 pl.BlockSpec((N_EXPERTS, LANES), lambda bi, si: (0, 0))]
    args += [h, mod, w_out, gain2.reshape(1, d), router_wt, router_b]
    return pl.pallas_call(
        functools.partial(_outproj_router_kernel, mix=mix),
        grid=(b, s // tm),
        in_specs=in_specs,
        out_specs=[row_spec, row_spec,
                   pl.BlockSpec((1, TOP_K, tm), lambda bi, si: (bi, 0, si)),
                   pl.BlockSpec((1, TOP_K, tm), lambda bi, si: (bi, 0, si))],
        out_shape=[jax.ShapeDtypeStruct((b, s, d), F32), jax.ShapeDtypeStruct((b, s, d), F32),
                   jax.ShapeDtypeStruct((b, TOP_K, s), I32), jax.ShapeDtypeStruct((b, TOP_K, s), F32)],
        compiler_params=_cp("parallel", "parallel"),
        name="outproj_router_" + mix,
    )(*args)


def _row_copy(src_hbm, row, dst, slot, r, sem):
    return pltpu.make_async_copy(src_hbm.at[pl.ds(row, 1), :], dst.at[slot, pl.ds(r, 1), :], sem.at[slot])


def _moe_ffn_kernel(be_ref, tok_cur, tok_next, u_hbm, wg_ref, wu_ref, wd_ref, y_ref, xbuf, sem):
    i = pl.program_id(0)
    n = pl.num_programs(0)
    rows = xbuf.shape[1]
    slot = i % 2

    def issue(tok_ref, s):
        def one(r, carry):
            _row_copy(u_hbm, tok_ref[r], xbuf, s, r, sem).start()
            return carry
        lax.fori_loop(0, rows, one, 0)

    @pl.when(i == 0)
    def _():
        issue(tok_cur, 0)

    @pl.when(i + 1 < n)
    def _():
        issue(tok_next, 1 - slot)

    def wait_one(r, carry):
        _row_copy(u_hbm, 0, xbuf, slot, r, sem).wait()
        return carry
    lax.fori_loop(0, rows, wait_one, 0)

    x = xbuf[slot].astype(BF16)
    hid = _silu(_dot(x, wg_ref[0])) * _dot(x, wu_ref[0])
    y_ref[...] = _dot(hid.astype(BF16), wd_ref[0])


def moe_expert_ffn(u_rows, row_token, blk_expert, w_gate, w_up, w_down):
    t, d = u_rows.shape
    r_total = row_token.shape[0]
    nblk = r_total // MOE_ROWS
    ff = w_gate.shape[2]
    smem_blk = functools.partial(pl.BlockSpec, (MOE_ROWS,), memory_space=pltpu.SMEM)
    return pl.pallas_call(
        _moe_ffn_kernel,
        grid_spec=pltpu.PrefetchScalarGridSpec(
            num_scalar_prefetch=1,
            grid=(nblk,),
            in_specs=[smem_blk(lambda i, be: (i,)),
                      smem_blk(lambda i, be: (jnp.minimum(i + 1, nblk - 1),)),
                      pl.BlockSpec(memory_space=pl.ANY),
                      pl.BlockSpec((1, d, ff), lambda i, be: (be[i], 0, 0)),
                      pl.BlockSpec((1, d, ff), lambda i, be: (be[i], 0, 0)),
                      pl.BlockSpec((1, ff, d), lambda i, be: (be[i], 0, 0))],
            out_specs=pl.BlockSpec((MOE_ROWS, d), lambda i, be: (i, 0)),
            scratch_shapes=[pltpu.VMEM((2, MOE_ROWS, d), F32), pltpu.SemaphoreType.DMA((2,))]),
        out_shape=jax.ShapeDtypeStruct((r_total, d), F32),
        compiler_params=_cp("arbitrary"),
        name="moe_expert_ffn",
    )(blk_expert, row_token, row_token, u_rows, w_gate, w_up, w_down)


def _pair_copy(src_hbm, row, dst, slot, k, r, sem):
    return pltpu.make_async_copy(src_hbm.at[pl.ds(row, 1), :], dst.at[slot, k, pl.ds(r, 1), :], sem.at[slot])


def _moe_combine_kernel(d_cur, d_next, ys_hbm, h_ref, mod_ref, w_ref, o_ref, buf, sem):
    i = pl.program_id(0)
    n = pl.num_programs(0)
    tm = buf.shape[2]
    slot = i % 2

    def issue(d_ref, s):
        def one(r, carry):
            _pair_copy(ys_hbm, d_ref[0, r], buf, s, 0, r, sem).start()
            _pair_copy(ys_hbm, d_ref[1, r], buf, s, 1, r, sem).start()
            return carry
        lax.fori_loop(0, tm, one, 0)

    @pl.when(i == 0)
    def _():
        issue(d_cur, 0)

    @pl.when(i + 1 < n)
    def _():
        issue(d_next, 1 - slot)

    def wait_one(r, carry):
        _pair_copy(ys_hbm, 0, buf, slot, 0, r, sem).wait()
        _pair_copy(ys_hbm, 0, buf, slot, 1, r, sem).wait()
        return carry
    lax.fori_loop(0, tm, wait_one, 0)

    w = w_ref[...]
    y = w[:, 0:1] * buf[slot, 0] + w[:, 1:2] * buf[slot, 1]
    o_ref[...] = h_ref[...] + mod_ref[0, 5:6, :] * y


def moe_combine(ys, dest, h, mod, wts, *, tm=256):
    b, s, d = h.shape
    t = b * s
    tm = min(tm, s)
    per_b = s // tm
    n = t // tm
    smem_blk = functools.partial(pl.BlockSpec, (TOP_K, tm), memory_space=pltpu.SMEM)
    out = pl.pallas_call(
        _moe_combine_kernel,
        grid=(n,),
        in_specs=[smem_blk(lambda i: (0, i)),
                  smem_blk(lambda i: (0, jnp.minimum(i + 1, n - 1))),
                  pl.BlockSpec(memory_space=pl.ANY),
                  pl.BlockSpec((tm, d), lambda i: (i, 0)),
                  pl.BlockSpec((1, 6, d), lambda i: (i // per_b, 0, 0)),
                  pl.BlockSpec((tm, TOP_K), lambda i: (i, 0))],
        out_specs=pl.BlockSpec((tm, d), lambda i: (i, 0)),
        out_shape=jax.ShapeDtypeStruct((t, d), F32),
        scratch_shapes=[pltpu.VMEM((2, TOP_K, tm, d), F32), pltpu.SemaphoreType.DMA((2,))],
        compiler_params=_cp("arbitrary"),
        name="moe_combine",
    )(dest, dest, ys, h.reshape(t, d), mod, wts)
    return out.reshape(b, s, d)


def moe_layer(h, u, eidx, wts, mod, w_gate, w_up, w_down):
    b, s, d = h.shape
    t = b * s
    e_flat = eidx.transpose(0, 2, 1).reshape(-1)
    n_pairs = t * TOP_K
    r_total = n_pairs + N_EXPERTS * MOE_ROWS
    nblk = r_total // MOE_ROWS
    onehot = (e_flat[:, None] == jnp.arange(N_EXPERTS, dtype=I32)[None, :]).astype(I32)
    csum = jnp.cumsum(onehot, axis=0)
    counts = csum[-1]
    rank = jnp.take_along_axis(csum, e_flat[:, None], axis=1)[:, 0] - 1
    padded = (counts + MOE_ROWS - 1) // MOE_ROWS * MOE_ROWS
    pad_end = jnp.cumsum(padded)
    pad_start = pad_end - padded
    dest = (pad_start[e_flat] + rank).astype(I32)
    row_token = jnp.zeros((r_total,), I32).at[dest].set(jnp.arange(n_pairs, dtype=I32) // TOP_K)
    blk_expert = jnp.minimum(jnp.searchsorted(pad_end, jnp.arange(nblk, dtype=I32) * MOE_ROWS, side="right"),
                             N_EXPERTS - 1).astype(I32)
    ys = moe_expert_ffn(u.reshape(t, d), row_token, blk_expert, w_gate, w_up, w_down)
    dest2 = dest.reshape(t, TOP_K).T
    w_tok = wts.transpose(0, 2, 1).reshape(t, TOP_K)
    return moe_combine(ys, dest2, h, mod, w_tok)


def _pad_cols(w, n):
    return jnp.pad(w, ((0, 0), (0, n - w.shape[1])))


def fox_attention(h, mod, gain, w_in, b_f, q_gain, k_gain, cos, sin):
    b, s, _ = h.shape
    n_main = 3 * D_MODEL
    main, tail = norm_proj(h, mod, gain, w_in[:, :n_main].astype(BF16),
                           _pad_cols(w_in[:, n_main:], LANES).astype(BF16))
    gains = jnp.stack([_tile_gain(q_gain, Q_SCALE), _tile_gain(k_gain)])
    qk = head_prep(main, gains, (0, 1), cos, sin, rope=False)
    tk = min(FLASH_TILE, s)
    cum = fox_cumulative_gate(tail, b_f)
    key_bias = cum.reshape(b, N_PAIRS, 2, s // tk, tk).transpose(0, 1, 3, 2, 4)
    return flash_attention(qk, qk, main, q_blk=lambda j: j, k_blk=lambda j: N_PAIRS + j,
                           v_blk=lambda j: 2 * N_PAIRS + j, n_batch=b, n_inner=N_PAIRS,
                           out_cols=D_MODEL, out_blk=lambda j: j, key_bias=key_bias, tk=tk)


def diff_attention(h, mod, gain, w_in, q_gain, k_gain, lambdas, sub_gain, layer_idx, cos, sin):
    b, s, _ = h.shape
    main = norm_proj(h, mod, gain, w_in.astype(BF16))
    gains = jnp.stack([_tile_gain(q_gain, Q_SCALE), _tile_gain(k_gain)])
    qk = head_prep(main, gains, (0, 1), cos, sin, rope=True)
    lam_init = 0.8 - 0.6 * math.exp(-0.3 * layer_idx)
    lam = jnp.pad(lambdas.astype(F32), ((0, 0), (0, LANES - HEAD_DIM)))
    return flash_attention(qk, qk, main, q_blk=lambda j: j, k_blk=lambda j: N_PAIRS + j,
                           v_blk=lambda j: 2 * N_PAIRS + j, n_batch=b, n_inner=N_PAIRS,
                           out_cols=D_MODEL, out_blk=lambda j: j, fin="diff", lam=lam,
                           sub_gain=sub_gain.astype(F32).reshape(1, LANES), lam_init=lam_init)


def dilated_attention(h, mod, gain, w_in, q_gain, k_gain, cos, sin):
    b, s, _ = h.shape
    ng = len(DIL_PAIRS)
    main = norm_proj(h, mod, gain, w_in.astype(BF16), tn=3 * D_MODEL)
    gq, gk = _tile_gain(q_gain, Q_SCALE), _tile_gain(k_gain)
    qk = head_prep(main, jnp.stack([gq, gk] * ng), tuple(3 * g + j for g in range(ng) for j in range(2)),
                   cos, sin, rope=True)
    outs, lses = [], []
    for g, (window, dil) in enumerate(DIL_PAIRS):
        sd = s // dil
        if dil == 1:
            qk_g, v_g, q0, v0 = qk, main, 2 * g * N_PAIRS, (3 * g + 2) * N_PAIRS
        else:
            qk_g = qk[:, :, 2 * g * D_MODEL:(2 * g + 2) * D_MODEL].reshape(b, sd, dil * 2 * D_MODEL)
            v_g = main[:, :, (3 * g + 2) * D_MODEL:(3 * g + 3) * D_MODEL].reshape(b, sd, dil * D_MODEL)
            q0 = v0 = 0
        o, lse = band_attention(
            qk_g, qk_g, v_g,
            q_blk=lambda j, q0=q0: q0 + (j // N_PAIRS) * 2 * N_PAIRS + j % N_PAIRS,
            k_blk=lambda j, q0=q0: q0 + (j // N_PAIRS) * 2 * N_PAIRS + N_PAIRS + j % N_PAIRS,
            v_blk=lambda j, v0=v0: v0 + j,
            n_batch=b, n_inner=dil * N_PAIRS, out_cols=dil * D_MODEL, out_blk=lambda j: j,
            window=window // dil + 1)
        outs.append(o.reshape(b, s, D_MODEL))
        lses.append(lse.reshape(b, s, D_MODEL))
    return outs + lses


def nsa_attention(h, mod, gain, w_in, q_gain, k_gain, cmp_pos, cmp_w1, cmp_w2, cos, sin, cos_h, sin_h):
    b, s, _ = h.shape
    nblk = s // NSA_BLOCK
    n_main = D_MODEL + 6 * NSA_GROUPS * HEAD_DIM
    main, tail = norm_proj(h, mod, gain, w_in[:, :n_main].astype(BF16),
                           _pad_cols(w_in[:, n_main:], LANES).astype(BF16))
    gains = jnp.zeros((8, LANES), F32)
    gains = gains.at[0].set(jnp.tile(q_gain.astype(F32) * Q_SCALE, 2))
    gains = gains.at[1].set(jnp.tile(k_gain[1].astype(F32), 2)).at[2].set(jnp.tile(k_gain[2].astype(F32), 2))
    q, ks2, vs2, kw2, vw2 = nsa_prep(main, gains, cos, sin)

    def to_block_rows(col0):
        x = main[:, :, col0:col0 + NSA_GROUPS * HEAD_DIM].reshape(b, nblk, NSA_BLOCK, NSA_GROUPS, HEAD_DIM)
        return x.transpose(0, 3, 1, 2, 4).reshape(b * NSA_GROUPS * nblk, NSA_BLOCK * HEAD_DIM)

    cos_b = jnp.tile(cos_h[NSA_BLOCK - 1::NSA_BLOCK], (1, 2))
    sin_b = jnp.concatenate([-sin_h[NSA_BLOCK - 1::NSA_BLOCK], sin_h[NSA_BLOCK - 1::NSA_BLOCK]], axis=-1)
    kc = nsa_compress(to_block_rows(D_MODEL), cmp_pos[0].reshape(1, -1), cmp_w1[0].astype(BF16),
                      cmp_w2[0].astype(BF16), k_gain[0].astype(F32).reshape(1, HEAD_DIM), cos_b, sin_b, is_key=True)
    vc = nsa_compress(to_block_rows(D_MODEL + NSA_GROUPS * HEAD_DIM), cmp_pos[1].reshape(1, -1),
                      cmp_w1[1].astype(BF16), cmp_w2[1].astype(BF16),
                      k_gain[0].astype(F32).reshape(1, HEAD_DIM), cos_b, sin_b, is_key=False)
    kc = kc.reshape(b, NSA_GROUPS, nblk, HEAD_DIM)
    vc = vc.reshape(b, NSA_GROUPS, nblk, HEAD_DIM)
    pad_rows = ((0, 0), (0, 0), (0, HEAD_DIM - nblk), (0, 0))
    kc = jnp.pad(kc, pad_rows)
    vc = jnp.pad(vc, pad_rows)
    kc2 = jnp.tile(kc, (1, 1, 2, 2)).astype(BF16)
    vc2 = jnp.concatenate([jnp.tile(vc, (1, 1, 1, 2)), jnp.zeros_like(jnp.tile(vc, (1, 1, 1, 2)))],
                          axis=2).astype(BF16)
    o_cmp, q_aug = nsa_compressed_attention(q, kc2, vc2, n_sel=min(NSA_TOPN, nblk))
    common = dict(q_blk=lambda j: j, k_blk=lambda j: j // 2, v_blk=lambda j: j // 2, n_batch=b,
                  n_inner=N_PAIRS, out_cols=D_MODEL, out_blk=lambda j: j, mode="aug")
    o_sel = flash_attention(q_aug, ks2, vs2, **common)
    o_win = flash_attention(q_aug, kw2, vw2, window=NSA_WINDOW, **common)
    return [o_cmp, o_sel, o_win, tail]


def kernel(x, c, fox_w_in, fox_b_f, fox_q_gain, fox_k_gain, fox_w_out, nsa_w_in, nsa_q_gain, nsa_k_gain, nsa_cmp_pos, nsa_cmp_w1, nsa_cmp_w2, nsa_w_out, dil_w_in, dil_q_gain, dil_k_gain, dil_w_out, diff_w_in, diff_q_gain, diff_k_gain, diff_lambda, diff_sub_gain, diff_w_out, norm_gain, ada_w, ada_b, router_w, router_b, moe_w_gate, moe_w_up, moe_w_down):
    b, s, d = x.shape
    depth = norm_gain.shape[0]
    cos, sin, cos_h, sin_h = rope_lane_tables(s)
    mods = ada_modulation(c, ada_w, ada_b).reshape(depth, b, 6, d)
    router_wt = router_w.T.astype(F32)
    router_bb = jnp.broadcast_to(router_b.astype(F32)[:, None], (N_EXPERTS, LANES))
    h = x
    for i in range(depth):
        mod = mods[i]
        kind, j = i % 4, i // 4
        g1 = norm_gain[i, 0]
        if kind == 0:
            attn = [fox_attention(h, mod, g1, fox_w_in[j], fox_b_f[j], fox_q_gain[j], fox_k_gain[j], cos, sin)]
            w_out, mix = fox_w_out[j], "plain"
        elif kind == 1:
            attn = nsa_attention(h, mod, g1, nsa_w_in[j], nsa_q_gain[j], nsa_k_gain[j], nsa_cmp_pos[j],
                                 nsa_cmp_w1[j], nsa_cmp_w2[j], cos, sin, cos_h, sin_h)
            w_out, mix = nsa_w_out[j], "nsa"
        elif kind == 2:
            attn = dilated_attention(h, mod, g1, dil_w_in[j], dil_q_gain[j], dil_k_gain[j], cos, sin)
            w_out, mix = dil_w_out[j], "dil"
        else:
            attn = [diff_attention(h, mod, g1, diff_w_in[j], diff_q_gain[j], diff_k_gain[j], diff_lambda[j],
                                   diff_sub_gain[j], i, cos, sin)]
            w_out, mix = diff_w_out[j], "plain"
        h, u, eidx, wts = outproj_router(attn, h, mod, w_out.astype(BF16), norm_gain[i, 1], router_wt,
                                         router_bb, mix=mix)
        h = moe_layer(h, u, eidx, wts, mod, moe_w_gate[i].astype(BF16), moe_w_up[i].astype(BF16),
                      moe_w_down[i].astype(BF16))
    return h
```

```python
import functools
import math

import jax
import jax.numpy as jnp
from jax import lax
from jax.experimental import pallas as pl
from jax.experimental.pallas import tpu as pltpu

F32 = jnp.float32
BF16 = jnp.bfloat16
I32 = jnp.int32

D_MODEL = 1024
HEAD_DIM = 64
LANES = 128
N_HEADS = D_MODEL // HEAD_DIM
N_PAIRS = D_MODEL // LANES
ROPE_THETA = 10000.0
EPS = 1e-6
NEG_INF = -1e30
TINY = 1e-30
M_INIT = -1e29
LOG2E = 1.4426950408889634
LN2 = 0.6931471805599453
Q_SCALE = HEAD_DIM ** -0.5 * LOG2E

NSA_GROUPS = 4
NSA_HPG = N_HEADS // NSA_GROUPS
NSA_BLOCK = 64
NSA_TOPN = 16
NSA_WINDOW = 512
DIL_PAIRS = ((128, 1), (512, 4), (2048, 16))

N_EXPERTS = 16
N_EXPERT_GROUPS = 4
EXPERTS_PER_GROUP = 4
TOP_K = 2
EXPERT_FF = 512
MOE_ROWS = 256
FLASH_TILE = 512

VMEM_LIMIT = 52 * 1024 * 1024


def _cp(*sem, vmem=VMEM_LIMIT):
    return pltpu.CompilerParams(dimension_semantics=sem, vmem_limit_bytes=vmem)


def _split3(a):
    hi = a.astype(BF16)
    r1 = a - hi.astype(F32)
    mid = r1.astype(BF16)
    lo = (r1 - mid.astype(F32)).astype(BF16)
    return hi, mid, lo


def _dot(a, b):
    return jnp.dot(a, b, preferred_element_type=F32)


def _dot_nt(a, b):
    return lax.dot_general(a, b, (((1,), (1,)), ((), ())), preferred_element_type=F32)


def _dot_f32(a, b, dot=_dot):
    ah, am, al = _split3(a)
    bh, bm, bl = _split3(b)
    return (dot(ah, bh) + (dot(ah, bm) + dot(am, bh))
            + (dot(ah, bl) + dot(al, bh) + dot(am, bm)))


def _dot_f32_exact_rhs(a, b_bf16):
    ah, am, al = _split3(a)
    return _dot(ah, b_bf16) + _dot(am, b_bf16) + _dot(al, b_bf16)


def _sigmoid(x):
    return 1.0 / (1.0 + jnp.exp(-x))


def _silu(x):
    return x * _sigmoid(x)


def _ada_kernel(c_ref, w_ref, b_ref, o_ref):
    c = c_ref[...]
    o_ref[0] = _dot_f32(_silu(c), w_ref[0]) + b_ref[0]


def ada_modulation(c, ada_w, ada_b):
    depth, d, n = ada_w.shape
    b = c.shape[0]
    tn = 1024
    return pl.pallas_call(
        _ada_kernel,
        grid=(depth, n // tn),
        in_specs=[pl.BlockSpec((b, d), lambda i, j: (0, 0)),
                  pl.BlockSpec((1, d, tn), lambda i, j: (i, 0, j)),
                  pl.BlockSpec((1, 1, tn), lambda i, j: (i, 0, j))],
        out_specs=pl.BlockSpec((1, b, tn), lambda i, j: (i, 0, j)),
        out_shape=jax.ShapeDtypeStruct((depth, b, n), F32),
        compiler_params=_cp("parallel", "parallel"),
        name="ada_modulation",
    )(c, ada_w, ada_b.reshape(depth, 1, n))


def _modulated_norm(x, gain, mod, shift_row, scale_row):
    ms = jnp.mean(x * x, axis=-1, keepdims=True)
    y = x * lax.rsqrt(ms + EPS) * gain
    return y * (1.0 + mod[scale_row:scale_row + 1, :]) + mod[shift_row:shift_row + 1, :]


def _norm_proj_kernel(h_ref, mod_ref, g_ref, w_ref, *rest, has_tail):
    u = _modulated_norm(h_ref[0], g_ref[...], mod_ref[0], 0, 1).astype(BF16)
    if has_tail:
        wt_ref, main_ref, tail_ref = rest
        tail_ref[0] = _dot(u, wt_ref[...])
    else:
        (main_ref,) = rest
    main_ref[0] = _dot(u, w_ref[...]).astype(main_ref.dtype)


def norm_proj(h, mod, gain, w_main, w_tail=None, *, tn=None, tm=512):
    b, s, d = h.shape
    n = w_main.shape[1]
    tn = tn or n
    tm = min(tm, s)
    in_specs = [pl.BlockSpec((1, tm, d), lambda j, bi, si: (bi, si, 0)),
                pl.BlockSpec((1, 6, d), lambda j, bi, si: (bi, 0, 0)),
                pl.BlockSpec((1, d), lambda j, bi, si: (0, 0)),
                pl.BlockSpec((d, tn), lambda j, bi, si: (0, j))]
    out_specs = [pl.BlockSpec((1, tm, tn), lambda j, bi, si: (bi, si, j))]
    out_shape = [jax.ShapeDtypeStruct((b, s, n), BF16)]
    args = [h, mod, gain.reshape(1, d), w_main]
    if w_tail is not None:
        in_specs.append(pl.BlockSpec((d, LANES), lambda j, bi, si: (0, 0)))
        out_specs.append(pl.BlockSpec((1, tm, LANES), lambda j, bi, si: (bi, si, 0)))
        out_shape.append(jax.ShapeDtypeStruct((b, s, LANES), F32))
        args.append(w_tail)
    outs = pl.pallas_call(
        functools.partial(_norm_proj_kernel, has_tail=w_tail is not None),
        grid=(n // tn, b, s // tm),
        in_specs=in_specs, out_specs=out_specs, out_shape=out_shape,
        compiler_params=_cp("parallel", "parallel", "parallel"),
        name="norm_proj",
    )(*args)
    return outs if w_tail is not None else outs[0]


def _lane_iota(rows):
    return lax.broadcasted_iota(I32, (rows, LANES), 1)


def _head_block_diag():
    r = lax.broadcasted_iota(I32, (LANES, LANES), 0) // HEAD_DIM
    c = lax.broadcasted_iota(I32, (LANES, LANES), 1) // HEAD_DIM
    return (r == c).astype(BF16)


def _head_norm_rope(x, gain, cos, sin, bd, rope):
    y = x * x
    hi = y.astype(BF16)
    lo = (y - hi.astype(F32)).astype(BF16)
    seg = _dot(hi, bd) + _dot(lo, bd)
    xn = x * lax.rsqrt(seg * (1.0 / HEAD_DIM) + EPS) * gain
    if rope:
        first_half = (_lane_iota(x.shape[0]) % HEAD_DIM) < HEAD_DIM // 2
        partner = jnp.where(first_half, pltpu.roll(xn, LANES - HEAD_DIM // 2, 1),
                            pltpu.roll(xn, HEAD_DIM // 2, 1))
        xn = xn * cos + partner * sin
    return xn


def _prep_kernel(cb_ref, x_ref, g_ref, cos_ref, sin_ref, o_ref, *, rope):
    del cb_ref
    bd = _head_block_diag()
    cos = cos_ref[...]
    sin = sin_ref[...]
    for c in range(x_ref.shape[2] // LANES):
        sl = slice(c * LANES, (c + 1) * LANES)
        x = x_ref[0, :, sl].astype(F32)
        o_ref[0, :, sl] = _head_norm_rope(x, g_ref[0, :, sl], cos, sin, bd, rope).astype(o_ref.dtype)


def head_prep(src, gains, col_blocks, cos, sin, *, rope, ts=512):
    b, s, _ = src.shape
    n = len(col_blocks)
    ts = min(ts, s)
    cb = jnp.asarray(col_blocks, I32)

    def x_map(bi, si, ci, cb_ref):
        return (bi, si, cb_ref[ci])

    return pl.pallas_call(
        functools.partial(_prep_kernel, rope=rope),
        grid_spec=pltpu.PrefetchScalarGridSpec(
            num_scalar_prefetch=1,
            grid=(b, s // ts, n),
            in_specs=[pl.BlockSpec((1, ts, D_MODEL), x_map),
                      pl.BlockSpec((1, 1, D_MODEL), lambda bi, si, ci, cb_ref: (ci, 0, 0)),
                      pl.BlockSpec((ts, LANES), lambda bi, si, ci, cb_ref: (si, 0)),
                      pl.BlockSpec((ts, LANES), lambda bi, si, ci, cb_ref: (si, 0))],
            out_specs=pl.BlockSpec((1, ts, D_MODEL), lambda bi, si, ci, cb_ref: (bi, si, ci))),
        out_shape=jax.ShapeDtypeStruct((b, s, n * D_MODEL), BF16),
        compiler_params=_cp("parallel", "parallel", "arbitrary"),
        name="head_prep",
    )(cb, src, gains, cos, sin)


def rope_lane_tables(s):
    inv = ROPE_THETA ** (-jnp.arange(0, HEAD_DIM, 2, dtype=F32) / HEAD_DIM)
    ang = jnp.arange(s).astype(F32)[:, None] * inv[None, :]
    cos, sin = jnp.cos(ang), jnp.sin(ang)
    return jnp.tile(cos, (1, 4)), jnp.tile(jnp.concatenate([-sin, sin], axis=-1), (1, 2)), cos, sin


def _tile_gain(g, scale=1.0):
    return jnp.tile(g.astype(F32) * scale, N_HEADS).reshape(1, D_MODEL)


def _fox_cum_kernel(f_ref, b_ref, o_ref, carry_ref):
    si = pl.program_id(1)
    ts = f_ref.shape[1]

    @pl.when(si == 0)
    def _():
        carry_ref[...] = jnp.zeros_like(carry_ref)

    z = f_ref[0] + b_ref[...]
    log_f = -(jnp.maximum(-z, 0.0) + jnp.log1p(jnp.exp(-jnp.abs(z))))
    r = lax.broadcasted_iota(I32, (ts, ts), 0)
    c = lax.broadcasted_iota(I32, (ts, ts), 1)
    upper = (r <= c).astype(BF16)
    cum = _dot_f32_exact_rhs(log_f.T, upper) + carry_ref[:, 0:1]
    o_ref[0] = cum[0:N_HEADS, :] * LOG2E
    carry_ref[...] = jnp.broadcast_to(cum[:, ts - 1:ts], carry_ref.shape)


def fox_cumulative_gate(tail, b_f, *, ts=256):
    b, s, _ = tail.shape
    ts = min(ts, s)
    bias = jnp.zeros((1, LANES), F32).at[0, :N_HEADS].set(b_f.astype(F32))
    return pl.pallas_call(
        _fox_cum_kernel,
        grid=(b, s // ts),
        in_specs=[pl.BlockSpec((1, ts, LANES), lambda bi, si: (bi, si, 0)),
                  pl.BlockSpec((1, LANES), lambda bi, si: (0, 0))],
        out_specs=pl.BlockSpec((1, N_HEADS, ts), lambda bi, si: (bi, 0, si)),
        out_shape=jax.ShapeDtypeStruct((b, N_HEADS, s), F32),
        scratch_shapes=[pltpu.VMEM((LANES, LANES), F32)],
        compiler_params=_cp("parallel", "arbitrary"),
        name="fox_cumulative_gate",
    )(tail, bias)


def _flash_kernel(*refs, tq, tk, window, mode, fin, has_bias, has_lse, lam_init):
    it = iter(refs)
    q_ref, k_ref, v_ref = next(it), next(it), next(it)
    kb_ref = next(it) if has_bias else None
    if fin == "diff":
        lam_ref, sg_ref = next(it), next(it)
    o_ref = next(it)
    lse_ref = next(it) if has_lse else None
    m_sc, l_sc, acc_sc = next(it), next(it), next(it)

    q_start = pl.program_id(2) * tq
    lane = _lane_iota(tq)
    low_half = lane < HEAD_DIM
    if mode == "pair":
        q = q_ref[0]
        zero = jnp.zeros_like(q)
        qh = (jnp.where(low_half, q, zero), jnp.where(low_half, zero, q))
    else:
        qh = (q_ref[0, :, 0:LANES], q_ref[0, :, LANES:2 * LANES])

    m_sc[...] = jnp.full(m_sc.shape, M_INIT, F32)
    l_sc[...] = jnp.zeros(l_sc.shape, F32)
    acc_sc[...] = jnp.zeros(acc_sc.shape, F32)
    row = q_start + lax.broadcasted_iota(I32, (tq, LANES), 0)
    n_chunk = tk // LANES

    def step(kv, masked):
        ks = pl.multiple_of(kv * tk, tk)
        kblk = k_ref[0, pl.ds(ks, tk), :]
        vblk = v_ref[0, pl.ds(ks, tk), :]
        if masked:
            masks = []
            for c in range(n_chunk):
                col = ks + c * LANES + lane
                mk = col <= row
                if window:
                    mk = mk & ((row - col) < window)
                masks.append(mk)
        if has_bias:
            kb = kb_ref[0, 0, kv]
        for h in range(2):
            kk = kblk if mode == "pair" else kblk[:, h * LANES:(h + 1) * LANES]
            s = _dot_nt(qh[h], kk)
            if has_bias:
                s = s - kb[h:h + 1, :]
            chunks = [s[:, c * LANES:(c + 1) * LANES] for c in range(n_chunk)]
            if masked:
                chunks = [jnp.where(mk, ch, -jnp.inf) for mk, ch in zip(masks, chunks)]
            mb = functools.reduce(jnp.maximum, chunks)
            m_old = m_sc[h]
            m_new = jnp.maximum(m_old, jnp.broadcast_to(jnp.max(mb, axis=-1, keepdims=True), (tq, LANES)))
            alpha = jnp.exp2(m_old - m_new)
            ps = [jnp.exp2(ch - m_new) for ch in chunks]
            l_sc[h] = alpha * l_sc[h] + functools.reduce(jnp.add, ps)
            p = ps[0] if n_chunk == 1 else jnp.concatenate(ps, axis=1)
            acc_sc[h] = alpha * acc_sc[h] + _dot(p.astype(BF16), vblk)
            m_sc[h] = m_new

    def loop(lo, hi, masked):
        def body(kv, carry):
            step(kv, masked)
            return carry
        lax.fori_loop(lo, hi, body, 0)

    last_blk = (q_start + (tq - 1)) // tk
    full_hi = (q_start + 1) // tk
    if window:
        first_blk = jnp.maximum(q_start - (window - 1), 0) // tk
        full_lo = jnp.maximum(q_start + (tq - 1) - window + tk, 0) // tk
        full_lo = jnp.maximum(jnp.minimum(full_lo, full_hi), first_blk)
        loop(first_blk, full_lo, True)
    else:
        full_lo = 0
    loop(full_lo, full_hi, False)
    loop(jnp.maximum(full_hi, full_lo), last_blk + 1, True)

    l0 = jnp.maximum(jnp.sum(l_sc[0], axis=-1, keepdims=True), TINY)
    l1 = jnp.maximum(jnp.sum(l_sc[1], axis=-1, keepdims=True), TINY)
    o0 = acc_sc[0] * (1.0 / l0)
    o1 = acc_sc[1] * (1.0 / l1)
    if fin == "select":
        o = jnp.where(low_half, o0, o1)
    else:
        lam_rows = lam_ref[...]
        lam = (jnp.exp(jnp.sum(lam_rows[0:1] * lam_rows[1:2], axis=-1, keepdims=True))
               - jnp.exp(jnp.sum(lam_rows[2:3] * lam_rows[3:4], axis=-1, keepdims=True)) + lam_init)
        o = o0 - lam * o1
        ms = jnp.mean(o * o, axis=-1, keepdims=True)
        o = o * lax.rsqrt(ms + EPS) * sg_ref[...] * (1.0 - lam_init)
    o_ref[0] = o.astype(o_ref.dtype)
    if has_lse:
        lse0 = m_sc[0] * LN2 + jnp.log(l0)
        lse1 = m_sc[1] * LN2 + jnp.log(l1)
        lse_ref[0] = jnp.where(low_half, lse0, lse1)


def flash_attention(q, k, v, *, q_blk, k_blk, v_blk, n_batch, n_inner, out_cols, out_blk,
                    mode="pair", fin="select", window=0, key_bias=None, lam=None, sub_gain=None,
                    lam_init=0.0, with_lse=False, tq=FLASH_TILE, tk=FLASH_TILE):
    nb, s, _ = q.shape
    tq, tk = min(tq, s), min(tk, s)
    wq = LANES if mode == "pair" else 2 * LANES
    in_specs = [pl.BlockSpec((1, tq, wq), lambda b, j, i: (b, i, q_blk(j))),
                pl.BlockSpec((1, s, wq), lambda b, j, i: (b, 0, k_blk(j))),
                pl.BlockSpec((1, s, LANES), lambda b, j, i: (b, 0, v_blk(j)))]
    args = [q, k, v]
    if key_bias is not None:
        in_specs.append(pl.BlockSpec((1, 1, s // tk, 2, tk), lambda b, j, i: (b, j, 0, 0, 0)))
        args.append(key_bias)
    if fin == "diff":
        in_specs += [pl.BlockSpec((4, LANES), lambda b, j, i: (0, 0)),
                     pl.BlockSpec((1, LANES), lambda b, j, i: (0, 0))]
        args += [lam, sub_gain]
    out_specs = [pl.BlockSpec((1, tq, LANES), lambda b, j, i: (b, i, out_blk(j)))]
    out_shape = [jax.ShapeDtypeStruct((nb, s, out_cols), BF16)]
    if with_lse:
        out_specs.append(pl.BlockSpec((1, tq, LANES), lambda b, j, i: (b, i, out_blk(j))))
        out_shape.append(jax.ShapeDtypeStruct((nb, s, out_cols), F32))
    outs = pl.pallas_call(
        functools.partial(_flash_kernel, tq=tq, tk=tk, window=window, mode=mode, fin=fin,
                          has_bias=key_bias is not None, has_lse=with_lse, lam_init=lam_init),
        grid=(n_batch, n_inner, s // tq),
        in_specs=in_specs, out_specs=out_specs, out_shape=out_shape,
        scratch_shapes=[pltpu.VMEM((2, tq, LANES), F32)] * 3,
        compiler_params=_cp("parallel", "parallel", "arbitrary"),
        name="flash_" + mode + "_" + fin,
    )(*args)
    return outs if with_lse else outs[0]


BAND_SUB = LANES


def _band_kernel(q_ref, k_ref, v_ref, o_ref, lse_ref, *, tq, span, window):
    seq = k_ref.shape[1]
    sub = min(BAND_SUB, tq)
    q_start = pl.program_id(2) * tq
    lane = _lane_iota(sub)
    low_half = lane < HEAD_DIM
    low_half_kv = _lane_iota(span) < HEAD_DIM
    for r in range(tq // sub):
        q0 = q_start + r * sub
        ks = pl.multiple_of(jnp.minimum(jnp.maximum(q0 - sub, 0), seq - span), sub)
        kblk = k_ref[0, pl.ds(ks, span), :]
        vblk = v_ref[0, pl.ds(ks, span), :]
        ones = jnp.ones_like(vblk)
        q = q_ref[0, r * sub:(r + 1) * sub, :]
        zero = jnp.zeros_like(q)
        row = q0 + lax.broadcasted_iota(I32, (sub, LANES), 0)
        masks = []
        for c in range(span // LANES):
            col = ks + c * LANES + lane
            masks.append((col <= row) & ((row - col) < window))
        outs, lses = [], []
        for h in range(2):
            own = low_half if h == 0 else jnp.logical_not(low_half)
            own_kv = low_half_kv if h == 0 else jnp.logical_not(low_half_kv)
            s = _dot_nt(jnp.where(own, q, zero), kblk)
            chunks = [jnp.where(mk, s[:, c * LANES:(c + 1) * LANES], -jnp.inf) for c, mk in enumerate(masks)]
            m = jnp.broadcast_to(jnp.max(functools.reduce(jnp.maximum, chunks), axis=-1, keepdims=True),
                                 (sub, LANES))
            ps = [jnp.exp2(ch - m) for ch in chunks]
            p = ps[0] if len(ps) == 1 else jnp.concatenate(ps, axis=1)
            acc = _dot(p.astype(BF16), jnp.where(own_kv, vblk, ones))
            l_own = jnp.maximum(pltpu.roll(acc, HEAD_DIM, 1), TINY)
            outs.append(acc * (1.0 / l_own))
            lses.append(m * LN2 + jnp.log(l_own))
        o_ref[0, r * sub:(r + 1) * sub, :] = jnp.where(low_half, outs[0], outs[1]).astype(o_ref.dtype)
        lse_ref[0, r * sub:(r + 1) * sub, :] = jnp.where(low_half, lses[0], lses[1])


def band_attention(q, k, v, *, q_blk, k_blk, v_blk, n_batch, n_inner, out_cols, out_blk, window, tq=512):
    nb, s, _ = q.shape
    tq = min(tq, s)
    span = min(2 * BAND_SUB, s)
    assert window <= span - min(BAND_SUB, tq) + 1 or span == s
    blk = lambda f, rows: pl.BlockSpec((1, rows, LANES), f)
    return pl.pallas_call(
        functools.partial(_band_kernel, tq=tq, span=span, window=window),
        grid=(n_batch, n_inner, s // tq),
        in_specs=[blk(lambda b, j, i: (b, i, q_blk(j)), tq),
                  blk(lambda b, j, i: (b, 0, k_blk(j)), s),
                  blk(lambda b, j, i: (b, 0, v_blk(j)), s)],
        out_specs=[blk(lambda b, j, i: (b, i, out_blk(j)), tq), blk(lambda b, j, i: (b, i, out_blk(j)), tq)],
        out_shape=[jax.ShapeDtypeStruct((nb, s, out_cols), BF16), jax.ShapeDtypeStruct((nb, s, out_cols), F32)],
        compiler_params=_cp("parallel", "parallel", "parallel"),
        name="band_attention",
    )(q, k, v)


def _nsa_prep_kernel(q_ref, ks_ref, vs_ref, kw_ref, vw_ref, g_ref, cos_ref, sin_ref,
                     qo_ref, ks2_ref, vs2_ref, kw2_ref, vw2_ref):
    ts = q_ref.shape[1]
    bd = _head_block_diag()
    cos, sin = cos_ref[...], sin_ref[...]
    lane = _lane_iota(ts)
    low_half = lane < HEAD_DIM
    for c in range(N_PAIRS):
        sl = slice(c * LANES, (c + 1) * LANES)
        qo_ref[0, :, sl] = _head_norm_rope(q_ref[0, :, sl].astype(F32), g_ref[0:1, :], cos, sin, bd,
                                           True).astype(qo_ref.dtype)
    t = pl.program_id(1) * ts + lax.broadcasted_iota(I32, (ts, LANES), 0)
    blk_onehot = ((t // NSA_BLOCK) == (lane % HEAD_DIM)).astype(F32)
    zeros = jnp.zeros((ts, LANES), F32)

    def spread(x, fill, out_ref, c):
        xr = pltpu.roll(x, HEAD_DIM, 1)
        base = 2 * c * 2 * LANES
        out_ref[0, :, base:base + LANES] = jnp.where(low_half, x, fill).astype(out_ref.dtype)
        out_ref[0, :, base + LANES:base + 2 * LANES] = jnp.where(low_half, fill, xr).astype(out_ref.dtype)
        out_ref[0, :, base + 2 * LANES:base + 3 * LANES] = jnp.where(low_half, xr, fill).astype(out_ref.dtype)
        out_ref[0, :, base + 3 * LANES:base + 4 * LANES] = jnp.where(low_half, fill, x).astype(out_ref.dtype)

    def dup(x, out_ref, c):
        xr = pltpu.roll(x, HEAD_DIM, 1)
        out_ref[0, :, 2 * c * LANES:(2 * c + 1) * LANES] = jnp.where(low_half, x, xr).astype(out_ref.dtype)
        out_ref[0, :, (2 * c + 1) * LANES:(2 * c + 2) * LANES] = jnp.where(low_half, xr, x).astype(out_ref.dtype)

    for c in range(NSA_GROUPS // 2):
        sl = slice(c * LANES, (c + 1) * LANES)
        ks = _head_norm_rope(ks_ref[0, :, sl].astype(F32), g_ref[1:2, :], cos, sin, bd, True)
        kw = _head_norm_rope(kw_ref[0, :, sl].astype(F32), g_ref[2:3, :], cos, sin, bd, True)
        spread(ks, blk_onehot, ks2_ref, c)
        spread(kw, zeros, kw2_ref, c)
        dup(vs_ref[0, :, sl].astype(F32), vs2_ref, c)
        dup(vw_ref[0, :, sl].astype(F32), vw2_ref, c)


def nsa_prep(main, gains, cos, sin, *, ts=512):
    b, s, _ = main.shape
    ts = min(ts, s)
    gw = NSA_GROUPS * HEAD_DIM

    def kv_spec(i):
        return pl.BlockSpec((1, ts, gw), lambda bi, si: (bi, si, i))

    return pl.pallas_call(
        _nsa_prep_kernel,
        grid=(b, s // ts),
        in_specs=[pl.BlockSpec((1, ts, D_MODEL), lambda bi, si: (bi, si, 0)),
                  kv_spec(6), kv_spec(7), kv_spec(8), kv_spec(9),
                  pl.BlockSpec((8, LANES), lambda bi, si: (0, 0)),
                  pl.BlockSpec((ts, LANES), lambda bi, si: (si, 0)),
                  pl.BlockSpec((ts, LANES), lambda bi, si: (si, 0))],
        out_specs=[pl.BlockSpec((1, ts, D_MODEL), lambda bi, si: (bi, si, 0)),
                   pl.BlockSpec((1, ts, NSA_GROUPS * 2 * LANES), lambda bi, si: (bi, si, 0)),
                   pl.BlockSpec((1, ts, NSA_GROUPS * LANES), lambda bi, si: (bi, si, 0)),
                   pl.BlockSpec((1, ts, NSA_GROUPS * 2 * LANES), lambda bi, si: (bi, si, 0)),
                   pl.BlockSpec((1, ts, NSA_GROUPS * LANES), lambda bi, si: (bi, si, 0))],
        out_shape=[jax.ShapeDtypeStruct((b, s, D_MODEL), BF16),
                   jax.ShapeDtypeStruct((b, s, NSA_GROUPS * 2 * LANES), BF16),
                   jax.ShapeDtypeStruct((b, s, NSA_GROUPS * LANES), BF16),
                   jax.ShapeDtypeStruct((b, s, NSA_GROUPS * 2 * LANES), BF16),
                   jax.ShapeDtypeStruct((b, s, NSA_GROUPS * LANES), BF16)],
        compiler_params=_cp("parallel", "parallel"),
        name="nsa_prep",
    )(main, main, main, main, main, gains, cos, sin)


def _nsa_compress_kernel(x_ref, pos_ref, w1_ref, w2_ref, g_ref, cos_ref, sin_ref, o_ref, *, is_key):
    x = (x_ref[...].astype(F32) + pos_ref[...]).astype(BF16)
    hid = _silu(_dot(x, w1_ref[...]))
    y = _dot(hid.astype(BF16), w2_ref[...])
    if is_key:
        ms = jnp.mean(y * y, axis=-1, keepdims=True)
        y = y * lax.rsqrt(ms + EPS) * g_ref[...]
        r = lax.broadcasted_iota(I32, (HEAD_DIM, HEAD_DIM), 0)
        c = lax.broadcasted_iota(I32, (HEAD_DIM, HEAD_DIM), 1)
        swap = (((r + HEAD_DIM // 2) % HEAD_DIM) == c).astype(BF16)
        y = y * cos_ref[...] + _dot_f32_exact_rhs(y, swap) * sin_ref[...]
    o_ref[...] = y


def nsa_compress(x, pos, w1, w2, gain, cos_blk, sin_blk, *, is_key):
    rows, k = x.shape
    nb = cos_blk.shape[0]
    hid = w1.shape[1]
    return pl.pallas_call(
        functools.partial(_nsa_compress_kernel, is_key=is_key),
        grid=(rows // nb,),
        in_specs=[pl.BlockSpec((nb, k), lambda i: (i, 0)),
                  pl.BlockSpec((1, k), lambda i: (0, 0)),
                  pl.BlockSpec((k, hid), lambda i: (0, 0)),
                  pl.BlockSpec((hid, HEAD_DIM), lambda i: (0, 0)),
                  pl.BlockSpec((1, HEAD_DIM), lambda i: (0, 0)),
                  pl.BlockSpec((nb, HEAD_DIM), lambda i: (0, 0)),
                  pl.BlockSpec((nb, HEAD_DIM), lambda i: (0, 0))],
        out_specs=pl.BlockSpec((nb, HEAD_DIM), lambda i: (i, 0)),
        out_shape=jax.ShapeDtypeStruct((rows, HEAD_DIM), F32),
        compiler_params=_cp("parallel"),
        name="nsa_compress",
    )(x, pos, w1, w2, gain, cos_blk, sin_blk)


def _nsa_cmp_kernel(q_ref, kc_ref, vc_ref, o_ref, qa_ref, *, n_sel):
    tq = q_ref.shape[1]
    lane = _lane_iota(tq)
    low_half = lane < HEAD_DIM
    blk = lane % HEAD_DIM
    t = pl.program_id(2) * tq + lax.broadcasted_iota(I32, (tq, LANES), 0)
    cmask = (blk + 1) * NSA_BLOCK <= t + 1
    kc = kc_ref[0, 0]
    vc = vc_ref[0, 0]
    imp = jnp.zeros((tq, LANES), F32)
    qblk = [q_ref[0, :, 0:LANES], q_ref[0, :, LANES:2 * LANES]]
    zero = jnp.zeros_like(qblk[0])
    outs = [None, None]
    for p in range(NSA_HPG):
        in_low = (p % 2) == 0
        qb = qblk[p // 2]
        qm = jnp.where(low_half, qb, zero) if in_low else jnp.where(low_half, zero, qb)
        s = jnp.where(cmask, _dot_nt(qm, kc), NEG_INF)
        m = jnp.max(s, axis=-1, keepdims=True)
        e = jnp.where(cmask, jnp.exp2(s - m), 0.0)
        den = jnp.maximum(0.5 * jnp.sum(e, axis=-1, keepdims=True), TINY)
        pc = e / den
        imp = imp + pc
        o = _dot(pc.astype(BF16), vc)
        prev = outs[p // 2]
        outs[p // 2] = o if prev is None else jnp.where(low_half, prev, o)
    o_ref[0, :, 0:LANES] = outs[0].astype(o_ref.dtype)
    o_ref[0, :, LANES:2 * LANES] = outs[1].astype(o_ref.dtype)

    cur = t // NSA_BLOCK
    forced = (blk == 0) | (blk == cur) | (blk == cur - 1)
    x = jnp.where(blk > cur, -1.0, jnp.where(forced, NSA_HPG + 1.0, imp))
    blk_f = blk.astype(F32)
    selected = jnp.zeros((tq, LANES), jnp.bool_)
    for _ in range(n_sel):
        mx = jnp.max(x, axis=-1, keepdims=True)
        first = jnp.min(jnp.where(x == mx, blk_f, float(LANES)), axis=-1, keepdims=True)
        hit = blk_f == first
        selected = selected | hit
        x = jnp.where(hit, -2.0, x)
    sel_bias = jnp.where(selected, 0.0, NEG_INF).astype(qa_ref.dtype)
    for p in range(NSA_HPG):
        qb = qblk[p // 2]
        qa = jnp.where(low_half, qb, sel_bias) if p % 2 == 0 else jnp.where(low_half, sel_bias, qb)
        qa_ref[0, :, p * LANES:(p + 1) * LANES] = qa


def nsa_compressed_attention(q, kc2, vc2, *, n_sel, tq=1024):
    b, s, _ = q.shape
    tq = min(tq, s)
    gq = NSA_HPG * HEAD_DIM
    return pl.pallas_call(
        functools.partial(_nsa_cmp_kernel, n_sel=n_sel),
        grid=(b, NSA_GROUPS, s // tq),
        in_specs=[pl.BlockSpec((1, tq, gq), lambda bi, g, i: (bi, i, g)),
                  pl.BlockSpec((1, 1, LANES, LANES), lambda bi, g, i: (bi, g, 0, 0)),
                  pl.BlockSpec((1, 1, LANES, LANES), lambda bi, g, i: (bi, g, 0, 0))],
        out_specs=[pl.BlockSpec((1, tq, gq), lambda bi, g, i: (bi, i, g)),
                   pl.BlockSpec((1, tq, 2 * gq), lambda bi, g, i: (bi, i, g))],
        out_shape=[jax.ShapeDtypeStruct((b, s, D_MODEL), BF16),
                   jax.ShapeDtypeStruct((b, s, 2 * D_MODEL), BF16)],
        compiler_params=_cp("parallel", "parallel", "parallel"),
        name="nsa_compressed_attention",
    )(q, kc2, vc2)


def _route(logits_t, rb):
    scores = _sigmoid(logits_t)
    sel = scores + rb
    rows = [sel[i:i + 1, :] for i in range(N_EXPERTS)]
    srows = [scores[i:i + 1, :] for i in range(N_EXPERTS)]
    best = grp = None
    for g in range(N_EXPERT_GROUPS):
        a, b, c, d = rows[4 * g:4 * g + 4]
        hi1, lo1, hi2, lo2 = jnp.maximum(a, b), jnp.minimum(a, b), jnp.maximum(c, d), jnp.minimum(c, d)
        gs = jnp.maximum(hi1, hi2) + jnp.maximum(jnp.minimum(hi1, hi2), jnp.maximum(lo1, lo2))
        if g == 0:
            best, grp = gs, jnp.zeros(gs.shape, I32)
        else:
            better = gs > best
            grp = jnp.where(better, g, grp)
            best = jnp.where(better, gs, best)

    def pick(vals, i):
        out = vals[i]
        for g in range(1, N_EXPERT_GROUPS):
            out = jnp.where(grp == g, vals[4 * g + i], out)
        return out

    v = [pick(rows, i) for i in range(EXPERTS_PER_GROUP)]
    w = [pick(srows, i) for i in range(EXPERTS_PER_GROUP)]
    l1, b1, w1 = jnp.zeros(grp.shape, I32), v[0], w[0]
    for i in range(1, EXPERTS_PER_GROUP):
        better = v[i] > b1
        l1 = jnp.where(better, i, l1)
        b1 = jnp.where(better, v[i], b1)
        w1 = jnp.where(better, w[i], w1)
    have = jnp.zeros(grp.shape, jnp.bool_)
    l2, b2, w2 = jnp.zeros(grp.shape, I32), jnp.zeros_like(b1), jnp.zeros_like(w1)
    for i in range(EXPERTS_PER_GROUP):
        valid = l1 != i
        better = valid & (jnp.logical_not(have) | (v[i] > b2))
        l2 = jnp.where(better, i, l2)
        b2 = jnp.where(better, v[i], b2)
        w2 = jnp.where(better, w[i], w2)
        have = have | valid
    wsum = w1 + w2
    return (grp * EXPERTS_PER_GROUP + l1, grp * EXPERTS_PER_GROUP + l2), (w1 / wsum, w2 / wsum)


def _outproj_router_kernel(*refs, mix):
    it = iter(refs)
    if mix == "plain":
        x = next(it)[0]
    elif mix == "nsa":
        o_refs = [next(it), next(it), next(it)]
        gl = _sigmoid(next(it)[0])
        g_hi = gl.astype(BF16)
        g_lo = (gl - g_hi.astype(F32)).astype(BF16)
        r = lax.broadcasted_iota(I32, (LANES, D_MODEL), 0)
        c = lax.broadcasted_iota(I32, (LANES, D_MODEL), 1) // HEAD_DIM
        x = None
        for i in range(3):
            expand = (r == c + i * N_HEADS).astype(BF16)
            term = (_dot(g_hi, expand) + _dot(g_lo, expand)) * o_refs[i][0].astype(F32)
            x = term if x is None else x + term
        x = x.astype(BF16)
    else:
        o_refs = [next(it), next(it), next(it)]
        lse = [next(it)[0], next(it)[0], next(it)[0]]
        mx = jnp.maximum(jnp.maximum(lse[0], lse[1]), lse[2])
        ex = [jnp.exp(l - mx) for l in lse]
        den = ex[0] + ex[1] + ex[2]
        x = ((ex[0] / den) * o_refs[0][0].astype(F32) + (ex[1] / den) * o_refs[1][0].astype(F32)
             + (ex[2] / den) * o_refs[2][0].astype(F32)).astype(BF16)
    h_ref, mod_ref, w_ref, g2_ref, rw_ref, rb_ref = (next(it) for _ in range(6))
    ho_ref, u_ref, e_ref, wt_ref = (next(it) for _ in range(4))
    mod = mod_ref[0]
    h_new = h_ref[0] + mod[2:3, :] * _dot(x, w_ref[...])
    ho_ref[0] = h_new
    u = _modulated_norm(h_new, g2_ref[...], mod, 3, 4)
    u_ref[0] = u
    logits_t = _dot_f32(rw_ref[...], u, dot=_dot_nt)
    eidx, wts = _route(logits_t, rb_ref[:, 0:1])
    for k in range(TOP_K):
        e_ref[0, k:k + 1, :] = eidx[k]
        wt_ref[0, k:k + 1, :] = wts[k]


def outproj_router(attn_inputs, h, mod, w_out, gain2, router_wt, router_b, *, mix, tm=256):
    b, s, d = h.shape
    tm = min(tm, s)
    row_spec = pl.BlockSpec((1, tm, d), lambda bi, si: (bi, si, 0))
    in_specs, args = [], []
    for a in attn_inputs:
        in_specs.append(pl.BlockSpec((1, tm, a.shape[2]), lambda bi, si: (bi, si, 0)))
        args.append(a)
    in_specs += [row_spec,
                 pl.BlockSpec((1, 6, d), lambda bi, si: (bi, 0, 0)),
                 pl.BlockSpec((d, d), lambda bi, si: (0, 0)),
                 pl.BlockSpec((1, d), lambda bi, si: (0, 0)),
                 pl.BlockSpec((N_EXPERTS, d), lambda bi, si: (0, 0)),
                 pl.BlockSpec((N_EXPERTS, LANES), lambda bi, si: (0, 0))]
    args += [h, mod, w_out, gain2.reshape(1, d), router_wt, router_b]
    return pl.pallas_call(
        functools.partial(_outproj_router_kernel, mix=mix),
        grid=(b, s // tm),
        in_specs=in_specs,
        out_specs=[row_spec, row_spec,
                   pl.BlockSpec((1, TOP_K, tm), lambda bi, si: (bi, 0, si)),
                   pl.BlockSpec((1, TOP_K, tm), lambda bi, si: (bi, 0, si))],
        out_shape=[jax.ShapeDtypeStruct((b, s, d), F32), jax.ShapeDtypeStruct((b, s, d), F32),
                   jax.ShapeDtypeStruct((b, TOP_K, s), I32), jax.ShapeDtypeStruct((b, TOP_K, s), F32)],
        compiler_params=_cp("parallel", "parallel"),
        name="outproj_router_" + mix,
    )(*args)


GATHER_UNROLL = 8


def _gather_rows(src_hbm, index_of, dst, n_rows, sem):
    def group(g, carry):
        for u in range(GATHER_UNROLL):
            r = g * GATHER_UNROLL + u
            pltpu.make_async_copy(src_hbm.at[pl.ds(index_of(r), 1), :], dst.at[pl.ds(r, 1), :],
                                  sem).start(priority=u % 2)
        return carry
    lax.fori_loop(0, n_rows // GATHER_UNROLL, group, 0)


def _wait_rows(src_hbm, dst, sem):
    pltpu.make_async_copy(src_hbm.at[pl.ds(0, dst.shape[0]), :], dst, sem).wait()


def _moe_ffn_kernel(be_ref, tok_cur, tok_next, u_hbm, wg_ref, wu_ref, wd_ref, y_ref, xbuf, wg_sc, wu_sc, wd_sc, sem):
    i = pl.program_id(0)
    n = pl.num_programs(0)
    rows = xbuf.shape[1]
    slot = i % 2

    @pl.when((i == 0) | (be_ref[i] != be_ref[jnp.maximum(i - 1, 0)]))
    def _():
        wg_sc[...] = wg_ref[0, 0].astype(BF16)
        wu_sc[...] = wu_ref[0, 0].astype(BF16)
        wd_sc[...] = wd_ref[0, 0].astype(BF16)

    @pl.when(i == 0)
    def _():
        _gather_rows(u_hbm, lambda r: tok_cur[r], xbuf.at[0], rows, sem.at[0])

    @pl.when(i + 1 < n)
    def _():
        _gather_rows(u_hbm, lambda r: tok_next[r], xbuf.at[1 - slot], rows, sem.at[1 - slot])

    _wait_rows(u_hbm, xbuf.at[slot], sem.at[slot])
    x = xbuf[slot].astype(BF16)
    hid = _silu(_dot(x, wg_sc[...])) * _dot(x, wu_sc[...])
    y_ref[...] = _dot(hid.astype(BF16), wd_sc[...])


def moe_expert_ffn(u_rows, row_token, blk_expert, w_gate, w_up, w_down, layer):
    t, d = u_rows.shape
    r_total = row_token.shape[0]
    nblk = r_total // MOE_ROWS
    ff = w_gate.shape[3]
    smem_blk = functools.partial(pl.BlockSpec, (MOE_ROWS,), memory_space=pltpu.SMEM)
    return pl.pallas_call(
        _moe_ffn_kernel,
        grid_spec=pltpu.PrefetchScalarGridSpec(
            num_scalar_prefetch=1,
            grid=(nblk,),
            in_specs=[smem_blk(lambda i, be: (i,)),
                      smem_blk(lambda i, be: (jnp.minimum(i + 1, nblk - 1),)),
                      pl.BlockSpec(memory_space=pl.ANY),
                      pl.BlockSpec((1, 1, d, ff), lambda i, be: (layer, be[i], 0, 0)),
                      pl.BlockSpec((1, 1, d, ff), lambda i, be: (layer, be[i], 0, 0)),
                      pl.BlockSpec((1, 1, ff, d), lambda i, be: (layer, be[i], 0, 0))],
            out_specs=pl.BlockSpec((MOE_ROWS, d), lambda i, be: (i, 0)),
            scratch_shapes=[pltpu.VMEM((2, MOE_ROWS, d), F32), pltpu.VMEM((d, ff), BF16), pltpu.VMEM((d, ff), BF16),
                            pltpu.VMEM((ff, d), BF16), pltpu.SemaphoreType.DMA((2,))]),
        out_shape=jax.ShapeDtypeStruct((r_total, d), F32),
        compiler_params=_cp("arbitrary"),
        name="moe_expert_ffn",
    )(blk_expert, row_token, row_token, u_rows, w_gate, w_up, w_down)


def _moe_combine_kernel(d_cur, d_next, ys_hbm, h_ref, mod_ref, w_ref, o_ref, buf, sem):
    i = pl.program_id(0)
    n = pl.num_programs(0)
    tm = buf.shape[2]
    slot = i % 2

    def issue(d_ref, s):
        for k in range(TOP_K):
            _gather_rows(ys_hbm, lambda r, k=k: d_ref[k, r], buf.at[s, k], tm, sem.at[s])

    @pl.when(i == 0)
    def _():
        issue(d_cur, 0)

    @pl.when(i + 1 < n)
    def _():
        issue(d_next, 1 - slot)

    for k in range(TOP_K):
        _wait_rows(ys_hbm, buf.at[slot, k], sem.at[slot])
    w = w_ref[...]
    y = w[:, 0:1] * buf[slot, 0] + w[:, 1:2] * buf[slot, 1]
    o_ref[...] = h_ref[...] + mod_ref[0, 5:6, :] * y


def moe_combine(ys, dest, h, mod, wts, *, tm=256):
    b, s, d = h.shape
    t = b * s
    tm = min(tm, s)
    per_b = s // tm
    n = t // tm
    smem_blk = functools.partial(pl.BlockSpec, (TOP_K, tm), memory_space=pltpu.SMEM)
    out = pl.pallas_call(
        _moe_combine_kernel,
        grid=(n,),
        in_specs=[smem_blk(lambda i: (0, i)),
                  smem_blk(lambda i: (0, jnp.minimum(i + 1, n - 1))),
                  pl.BlockSpec(memory_space=pl.ANY),
                  pl.BlockSpec((tm, d), lambda i: (i, 0)),
                  pl.BlockSpec((1, 6, d), lambda i: (i // per_b, 0, 0)),
                  pl.BlockSpec((tm, TOP_K), lambda i: (i, 0))],
        out_specs=pl.BlockSpec((tm, d), lambda i: (i, 0)),
        out_shape=jax.ShapeDtypeStruct((t, d), F32),
        scratch_shapes=[pltpu.VMEM((2, TOP_K, tm, d), F32), pltpu.SemaphoreType.DMA((2,))],
        compiler_params=_cp("arbitrary"),
        name="moe_combine",
    )(dest, dest, ys, h.reshape(t, d), mod, wts)
    return out.reshape(b, s, d)


def moe_layer(h, u, eidx, wts, mod, w_gate, w_up, w_down, layer):
    b, s, d = h.shape
    t = b * s
    e_flat = eidx.transpose(0, 2, 1).reshape(-1)
    n_pairs = t * TOP_K
    r_total = n_pairs + N_EXPERTS * MOE_ROWS
    nblk = r_total // MOE_ROWS
    onehot = (e_flat[:, None] == jnp.arange(N_EXPERTS, dtype=I32)[None, :]).astype(I32)
    csum = jnp.cumsum(onehot, axis=0)
    counts = csum[-1]
    rank = jnp.take_along_axis(csum, e_flat[:, None], axis=1)[:, 0] - 1
    padded = (counts + MOE_ROWS - 1) // MOE_ROWS * MOE_ROWS
    pad_end = jnp.cumsum(padded)
    pad_start = pad_end - padded
    dest = (pad_start[e_flat] + rank).astype(I32)
    row_token = jnp.zeros((r_total,), I32).at[dest].set(jnp.arange(n_pairs, dtype=I32) // TOP_K)
    blk_expert = jnp.minimum(jnp.searchsorted(pad_end, jnp.arange(nblk, dtype=I32) * MOE_ROWS, side="right"),
                             N_EXPERTS - 1).astype(I32)
    ys = moe_expert_ffn(u.reshape(t, d), row_token, blk_expert, w_gate, w_up, w_down, layer)
    dest2 = dest.reshape(t, TOP_K).T
    w_tok = wts.transpose(0, 2, 1).reshape(t, TOP_K)
    return moe_combine(ys, dest2, h, mod, w_tok)


def _pad_cols(w, n):
    return jnp.pad(w, ((0, 0), (0, n - w.shape[1])))


def fox_attention(h, mod, gain, w_in, b_f, q_gain, k_gain, cos, sin):
    b, s, _ = h.shape
    n_main = 3 * D_MODEL
    main, tail = norm_proj(h, mod, gain, w_in[:, :n_main].astype(BF16),
                           _pad_cols(w_in[:, n_main:], LANES).astype(BF16))
    gains = jnp.stack([_tile_gain(q_gain, Q_SCALE), _tile_gain(k_gain)])
    qk = head_prep(main, gains, (0, 1), cos, sin, rope=False)
    tk = min(FLASH_TILE, s)
    cum = fox_cumulative_gate(tail, b_f)
    key_bias = cum.reshape(b, N_PAIRS, 2, s // tk, tk).transpose(0, 1, 3, 2, 4)
    return flash_attention(qk, qk, main, q_blk=lambda j: j, k_blk=lambda j: N_PAIRS + j,
                           v_blk=lambda j: 2 * N_PAIRS + j, n_batch=b, n_inner=N_PAIRS,
                           out_cols=D_MODEL, out_blk=lambda j: j, key_bias=key_bias, tk=tk)


def diff_attention(h, mod, gain, w_in, q_gain, k_gain, lambdas, sub_gain, layer_idx, cos, sin):
    b, s, _ = h.shape
    main = norm_proj(h, mod, gain, w_in.astype(BF16))
    gains = jnp.stack([_tile_gain(q_gain, Q_SCALE), _tile_gain(k_gain)])
    qk = head_prep(main, gains, (0, 1), cos, sin, rope=True)
    lam_init = 0.8 - 0.6 * math.exp(-0.3 * layer_idx)
    lam = jnp.pad(lambdas.astype(F32), ((0, 0), (0, LANES - HEAD_DIM)))
    return flash_attention(qk, qk, main, q_blk=lambda j: j, k_blk=lambda j: N_PAIRS + j,
                           v_blk=lambda j: 2 * N_PAIRS + j, n_batch=b, n_inner=N_PAIRS,
                           out_cols=D_MODEL, out_blk=lambda j: j, fin="diff", lam=lam,
                           sub_gain=sub_gain.astype(F32).reshape(1, LANES), lam_init=lam_init)


def dilated_attention(h, mod, gain, w_in, q_gain, k_gain, cos, sin):
    b, s, _ = h.shape
    ng = len(DIL_PAIRS)
    main = norm_proj(h, mod, gain, w_in.astype(BF16), tn=3 * D_MODEL)
    gq, gk = _tile_gain(q_gain, Q_SCALE), _tile_gain(k_gain)
    qk = head_prep(main, jnp.stack([gq, gk] * ng), tuple(3 * g + j for g in range(ng) for j in range(2)),
                   cos, sin, rope=True)
    outs, lses = [], []
    for g, (window, dil) in enumerate(DIL_PAIRS):
        sd = s // dil
        if dil == 1:
            qk_g, v_g, q0, v0 = qk, main, 2 * g * N_PAIRS, (3 * g + 2) * N_PAIRS
        else:
            qk_g = qk[:, :, 2 * g * D_MODEL:(2 * g + 2) * D_MODEL].reshape(b, sd, dil * 2 * D_MODEL)
            v_g = main[:, :, (3 * g + 2) * D_MODEL:(3 * g + 3) * D_MODEL].reshape(b, sd, dil * D_MODEL)
            q0 = v0 = 0
        o, lse = band_attention(
            qk_g, qk_g, v_g,
            q_blk=lambda j, q0=q0: q0 + (j // N_PAIRS) * 2 * N_PAIRS + j % N_PAIRS,
            k_blk=lambda j, q0=q0: q0 + (j // N_PAIRS) * 2 * N_PAIRS + N_PAIRS + j % N_PAIRS,
            v_blk=lambda j, v0=v0: v0 + j,
            n_batch=b, n_inner=dil * N_PAIRS, out_cols=dil * D_MODEL, out_blk=lambda j: j,
            window=window // dil + 1)
        outs.append(o.reshape(b, s, D_MODEL))
        lses.append(lse.reshape(b, s, D_MODEL))
    return outs + lses


def nsa_attention(h, mod, gain, w_in, q_gain, k_gain, cmp_pos, cmp_w1, cmp_w2, cos, sin, cos_h, sin_h):
    b, s, _ = h.shape
    nblk = s // NSA_BLOCK
    n_main = D_MODEL + 6 * NSA_GROUPS * HEAD_DIM
    main, tail = norm_proj(h, mod, gain, w_in[:, :n_main].astype(BF16),
                           _pad_cols(w_in[:, n_main:], LANES).astype(BF16))
    gains = jnp.zeros((8, LANES), F32)
    gains = gains.at[0].set(jnp.tile(q_gain.astype(F32) * Q_SCALE, 2))
    gains = gains.at[1].set(jnp.tile(k_gain[1].astype(F32), 2)).at[2].set(jnp.tile(k_gain[2].astype(F32), 2))
    q, ks2, vs2, kw2, vw2 = nsa_prep(main, gains, cos, sin)

    def to_block_rows(col0):
        x = main[:, :, col0:col0 + NSA_GROUPS * HEAD_DIM].reshape(b, nblk, NSA_BLOCK, NSA_GROUPS, HEAD_DIM)
        return x.transpose(0, 3, 1, 2, 4).reshape(b * NSA_GROUPS * nblk, NSA_BLOCK * HEAD_DIM)

    cos_b = jnp.tile(cos_h[NSA_BLOCK - 1::NSA_BLOCK], (1, 2))
    sin_b = jnp.concatenate([-sin_h[NSA_BLOCK - 1::NSA_BLOCK], sin_h[NSA_BLOCK - 1::NSA_BLOCK]], axis=-1)
    kc = nsa_compress(to_block_rows(D_MODEL), cmp_pos[0].reshape(1, -1), cmp_w1[0].astype(BF16),
                      cmp_w2[0].astype(BF16), k_gain[0].astype(F32).reshape(1, HEAD_DIM), cos_b, sin_b, is_key=True)
    vc = nsa_compress(to_block_rows(D_MODEL + NSA_GROUPS * HEAD_DIM), cmp_pos[1].reshape(1, -1),
                      cmp_w1[1].astype(BF16), cmp_w2[1].astype(BF16),
                      k_gain[0].astype(F32).reshape(1, HEAD_DIM), cos_b, sin_b, is_key=False)
    kc = kc.reshape(b, NSA_GROUPS, nblk, HEAD_DIM)
    vc = vc.reshape(b, NSA_GROUPS, nblk, HEAD_DIM)
    pad_rows = ((0, 0), (0, 0), (0, HEAD_DIM - nblk), (0, 0))
    kc = jnp.pad(kc, pad_rows)
    vc = jnp.pad(vc, pad_rows)
    kc2 = jnp.tile(kc, (1, 1, 2, 2)).astype(BF16)
    vc2 = jnp.concatenate([jnp.tile(vc, (1, 1, 1, 2)), jnp.zeros_like(jnp.tile(vc, (1, 1, 1, 2)))],
                          axis=2).astype(BF16)
    o_cmp, q_aug = nsa_compressed_attention(q, kc2, vc2, n_sel=min(NSA_TOPN, nblk))
    common = dict(q_blk=lambda j: j, k_blk=lambda j: j // 2, v_blk=lambda j: j // 2, n_batch=b,
                  n_inner=N_PAIRS, out_cols=D_MODEL, out_blk=lambda j: j, mode="aug")
    o_sel = flash_attention(q_aug, ks2, vs2, **common)
    o_win = flash_attention(q_aug, kw2, vw2, window=NSA_WINDOW, **common)
    return [o_cmp, o_sel, o_win, tail]


def kernel(x, c, fox_w_in, fox_b_f, fox_q_gain, fox_k_gain, fox_w_out, nsa_w_in, nsa_q_gain, nsa_k_gain, nsa_cmp_pos, nsa_cmp_w1, nsa_cmp_w2, nsa_w_out, dil_w_in, dil_q_gain, dil_k_gain, dil_w_out, diff_w_in, diff_q_gain, diff_k_gain, diff_lambda, diff_sub_gain, diff_w_out, norm_gain, ada_w, ada_b, router_w, router_b, moe_w_gate, moe_w_up, moe_w_down):
    b, s, d = x.shape
    depth = norm_gain.shape[0]
    cos, sin, cos_h, sin_h = rope_lane_tables(s)
    mods = ada_modulation(c, ada_w, ada_b).reshape(depth, b, 6, d)
    router_wt = router_w.T.astype(F32)
    router_bb = jnp.broadcast_to(router_b.astype(F32)[:, None], (N_EXPERTS, LANES))
    h = x
    for i in range(depth):
        mod = mods[i]
        kind, j = i % 4, i // 4
        g1 = norm_gain[i, 0]
        if kind == 0:
            attn = [fox_attention(h, mod, g1, fox_w_in[j], fox_b_f[j], fox_q_gain[j], fox_k_gain[j], cos, sin)]
            w_out, mix = fox_w_out[j], "plain"
        elif kind == 1:
            attn = nsa_attention(h, mod, g1, nsa_w_in[j], nsa_q_gain[j], nsa_k_gain[j], nsa_cmp_pos[j],
                                 nsa_cmp_w1[j], nsa_cmp_w2[j], cos, sin, cos_h, sin_h)
            w_out, mix = nsa_w_out[j], "nsa"
        elif kind == 2:
            attn = dilated_attention(h, mod, g1, dil_w_in[j], dil_q_gain[j], dil_k_gain[j], cos, sin)
            w_out, mix = dil_w_out[j], "dil"
        else:
            attn = [diff_attention(h, mod, g1, diff_w_in[j], diff_q_gain[j], diff_k_gain[j], diff_lambda[j],
                                   diff_sub_gain[j], i, cos, sin)]
            w_out, mix = diff_w_out[j], "plain"
        h, u, eidx, wts = outproj_router(attn, h, mod, w_out.astype(BF16), norm_gain[i, 1], router_wt,
                                         router_bb, mix=mix)
        h = moe_layer(h, u, eidx, wts, mod, moe_w_gate, moe_w_up, moe_w_down, i)
    return h
```

```python
import functools
import math

import jax
import jax.numpy as jnp
from jax import lax
from jax.experimental import pallas as pl
from jax.experimental.pallas import tpu as pltpu

F32 = jnp.float32
BF16 = jnp.bfloat16
I32 = jnp.int32

D_MODEL = 1024
HEAD_DIM = 64
LANES = 128
N_HEADS = D_MODEL // HEAD_DIM
N_PAIRS = D_MODEL // LANES
ROPE_THETA = 10000.0
EPS = 1e-6
NEG_INF = -1e30
TINY = 1e-30
M_INIT = -1e29
LOG2E = 1.4426950408889634
LN2 = 0.6931471805599453
Q_SCALE = HEAD_DIM ** -0.5 * LOG2E

NSA_GROUPS = 4
NSA_HPG = N_HEADS // NSA_GROUPS
NSA_BLOCK = 64
NSA_TOPN = 16
NSA_WINDOW = 512
DIL_PAIRS = ((128, 1), (512, 4), (2048, 16))

N_EXPERTS = 16
N_EXPERT_GROUPS = 4
EXPERTS_PER_GROUP = 4
TOP_K = 2
EXPERT_FF = 512
MOE_ROWS = 256
FLASH_TILE = 512
FLASH_PAIRS = 2

VMEM_LIMIT = 52 * 1024 * 1024


def _cp(*sem, vmem=VMEM_LIMIT):
    return pltpu.CompilerParams(dimension_semantics=sem, vmem_limit_bytes=vmem)


def _split3(a):
    hi = a.astype(BF16)
    r1 = a - hi.astype(F32)
    mid = r1.astype(BF16)
    lo = (r1 - mid.astype(F32)).astype(BF16)
    return hi, mid, lo


def _dot(a, b):
    return jnp.dot(a, b, preferred_element_type=F32)


def _dot_nt(a, b):
    return lax.dot_general(a, b, (((1,), (1,)), ((), ())), preferred_element_type=F32)


def _dot_f32(a, b, dot=_dot):
    ah, am, al = _split3(a)
    bh, bm, bl = _split3(b)
    return (dot(ah, bh) + (dot(ah, bm) + dot(am, bh))
            + (dot(ah, bl) + dot(al, bh) + dot(am, bm)))


def _dot_f32_exact_rhs(a, b_bf16):
    ah, am, al = _split3(a)
    return _dot(ah, b_bf16) + _dot(am, b_bf16) + _dot(al, b_bf16)


def _sigmoid(x):
    return 1.0 / (1.0 + jnp.exp(-x))


def _silu(x):
    return x * _sigmoid(x)


def _ada_kernel(c_ref, w_ref, b_ref, o_ref):
    c = c_ref[...]
    o_ref[0] = _dot_f32(_silu(c), w_ref[0]) + b_ref[0]


def ada_modulation(c, ada_w, ada_b):
    depth, d, n = ada_w.shape
    b = c.shape[0]
    tn = 1024
    return pl.pallas_call(
        _ada_kernel,
        grid=(depth, n // tn),
        in_specs=[pl.BlockSpec((b, d), lambda i, j: (0, 0)),
                  pl.BlockSpec((1, d, tn), lambda i, j: (i, 0, j)),
                  pl.BlockSpec((1, 1, tn), lambda i, j: (i, 0, j))],
        out_specs=pl.BlockSpec((1, b, tn), lambda i, j: (i, 0, j)),
        out_shape=jax.ShapeDtypeStruct((depth, b, n), F32),
        compiler_params=_cp("parallel", "parallel"),
        name="ada_modulation",
    )(c, ada_w, ada_b.reshape(depth, 1, n))


def _modulated_norm(x, gain, mod, shift_row, scale_row):
    ms = jnp.mean(x * x, axis=-1, keepdims=True)
    y = x * lax.rsqrt(ms + EPS) * gain
    return y * (1.0 + mod[scale_row:scale_row + 1, :]) + mod[shift_row:shift_row + 1, :]


def _norm_proj_kernel(h_ref, mod_ref, g_ref, w_ref, *rest, has_tail):
    u = _modulated_norm(h_ref[0], g_ref[...], mod_ref[0], 0, 1).astype(BF16)
    if has_tail:
        wt_ref, main_ref, tail_ref = rest
        tail_ref[0] = _dot(u, wt_ref[...])
    else:
        (main_ref,) = rest
    main_ref[0] = _dot(u, w_ref[...]).astype(main_ref.dtype)


def norm_proj(h, mod, gain, w_main, w_tail=None, *, tn=None, tm=512):
    b, s, d = h.shape
    n = w_main.shape[1]
    tn = tn or n
    tm = min(tm, s)
    in_specs = [pl.BlockSpec((1, tm, d), lambda j, bi, si: (bi, si, 0)),
                pl.BlockSpec((1, 6, d), lambda j, bi, si: (bi, 0, 0)),
                pl.BlockSpec((1, d), lambda j, bi, si: (0, 0)),
                pl.BlockSpec((d, tn), lambda j, bi, si: (0, j))]
    out_specs = [pl.BlockSpec((1, tm, tn), lambda j, bi, si: (bi, si, j))]
    out_shape = [jax.ShapeDtypeStruct((b, s, n), BF16)]
    args = [h, mod, gain.reshape(1, d), w_main]
    if w_tail is not None:
        in_specs.append(pl.BlockSpec((d, LANES), lambda j, bi, si: (0, 0)))
        out_specs.append(pl.BlockSpec((1, tm, LANES), lambda j, bi, si: (bi, si, 0)))
        out_shape.append(jax.ShapeDtypeStruct((b, s, LANES), F32))
        args.append(w_tail)
    outs = pl.pallas_call(
        functools.partial(_norm_proj_kernel, has_tail=w_tail is not None),
        grid=(n // tn, b, s // tm),
        in_specs=in_specs, out_specs=out_specs, out_shape=out_shape,
        compiler_params=_cp("parallel", "parallel", "parallel"),
        name="norm_proj",
    )(*args)
    return outs if w_tail is not None else outs[0]


def _lane_iota(rows):
    return lax.broadcasted_iota(I32, (rows, LANES), 1)


def _head_block_diag():
    r = lax.broadcasted_iota(I32, (LANES, LANES), 0) // HEAD_DIM
    c = lax.broadcasted_iota(I32, (LANES, LANES), 1) // HEAD_DIM
    return (r == c).astype(BF16)


def _head_norm_rope(x, gain, cos, sin, bd, rope):
    y = x * x
    hi = y.astype(BF16)
    lo = (y - hi.astype(F32)).astype(BF16)
    seg = _dot(hi, bd) + _dot(lo, bd)
    xn = x * lax.rsqrt(seg * (1.0 / HEAD_DIM) + EPS) * gain
    if rope:
        first_half = (_lane_iota(x.shape[0]) % HEAD_DIM) < HEAD_DIM // 2
        partner = jnp.where(first_half, pltpu.roll(xn, LANES - HEAD_DIM // 2, 1),
                            pltpu.roll(xn, HEAD_DIM // 2, 1))
        xn = xn * cos + partner * sin
    return xn


def _prep_kernel(cb_ref, x_ref, g_ref, cos_ref, sin_ref, o_ref, *, rope):
    del cb_ref
    bd = _head_block_diag()
    cos = cos_ref[...]
    sin = sin_ref[...]
    for c in range(x_ref.shape[2] // LANES):
        sl = slice(c * LANES, (c + 1) * LANES)
        x = x_ref[0, :, sl].astype(F32)
        o_ref[0, :, sl] = _head_norm_rope(x, g_ref[0, :, sl], cos, sin, bd, rope).astype(o_ref.dtype)


def head_prep(src, gains, col_blocks, cos, sin, *, rope, ts=512):
    b, s, _ = src.shape
    n = len(col_blocks)
    ts = min(ts, s)
    cb = jnp.asarray(col_blocks, I32)

    def x_map(bi, si, ci, cb_ref):
        return (bi, si, cb_ref[ci])

    return pl.pallas_call(
        functools.partial(_prep_kernel, rope=rope),
        grid_spec=pltpu.PrefetchScalarGridSpec(
            num_scalar_prefetch=1,
            grid=(b, s // ts, n),
            in_specs=[pl.BlockSpec((1, ts, D_MODEL), x_map),
                      pl.BlockSpec((1, 1, D_MODEL), lambda bi, si, ci, cb_ref: (ci, 0, 0)),
                      pl.BlockSpec((ts, LANES), lambda bi, si, ci, cb_ref: (si, 0)),
                      pl.BlockSpec((ts, LANES), lambda bi, si, ci, cb_ref: (si, 0))],
            out_specs=pl.BlockSpec((1, ts, D_MODEL), lambda bi, si, ci, cb_ref: (bi, si, ci))),
        out_shape=jax.ShapeDtypeStruct((b, s, n * D_MODEL), BF16),
        compiler_params=_cp("parallel", "parallel", "arbitrary"),
        name="head_prep",
    )(cb, src, gains, cos, sin)


def rope_lane_tables(s):
    inv = ROPE_THETA ** (-jnp.arange(0, HEAD_DIM, 2, dtype=F32) / HEAD_DIM)
    ang = jnp.arange(s).astype(F32)[:, None] * inv[None, :]
    cos, sin = jnp.cos(ang), jnp.sin(ang)
    return jnp.tile(cos, (1, 4)), jnp.tile(jnp.concatenate([-sin, sin], axis=-1), (1, 2)), cos, sin


def _tile_gain(g, scale=1.0):
    return jnp.tile(g.astype(F32) * scale, N_HEADS).reshape(1, D_MODEL)


def _fox_cum_kernel(f_ref, b_ref, o_ref, carry_ref):
    si = pl.program_id(1)
    ts = f_ref.shape[1]

    @pl.when(si == 0)
    def _():
        carry_ref[...] = jnp.zeros_like(carry_ref)

    z = f_ref[0] + b_ref[...]
    log_f = -(jnp.maximum(-z, 0.0) + jnp.log1p(jnp.exp(-jnp.abs(z))))
    r = lax.broadcasted_iota(I32, (ts, ts), 0)
    c = lax.broadcasted_iota(I32, (ts, ts), 1)
    upper = (r <= c).astype(BF16)
    cum = _dot_f32_exact_rhs(log_f.T, upper) + carry_ref[:, 0:1]
    o_ref[0] = cum[0:N_HEADS, :] * LOG2E
    carry_ref[...] = jnp.broadcast_to(cum[:, ts - 1:ts], carry_ref.shape)


def fox_cumulative_gate(tail, b_f, *, ts=256):
    b, s, _ = tail.shape
    ts = min(ts, s)
    bias = jnp.zeros((1, LANES), F32).at[0, :N_HEADS].set(b_f.astype(F32))
    return pl.pallas_call(
        _fox_cum_kernel,
        grid=(b, s // ts),
        in_specs=[pl.BlockSpec((1, ts, LANES), lambda bi, si: (bi, si, 0)),
                  pl.BlockSpec((1, LANES), lambda bi, si: (0, 0))],
        out_specs=pl.BlockSpec((1, N_HEADS, ts), lambda bi, si: (bi, 0, si)),
        out_shape=jax.ShapeDtypeStruct((b, N_HEADS, s), F32),
        scratch_shapes=[pltpu.VMEM((LANES, LANES), F32)],
        compiler_params=_cp("parallel", "arbitrary"),
        name="fox_cumulative_gate",
    )(tail, bias)


def _flash_kernel(*refs, tq, tk, window, mode, fin, has_bias, pairs, lam_init):
    it = iter(refs)
    q_ref, k_ref, v_ref = next(it), next(it), next(it)
    kb_ref = next(it) if has_bias else None
    if fin == "diff":
        lam_ref, sg_ref = next(it), next(it)
    o_ref = next(it)
    m_sc, l_sc, acc_sc = next(it), next(it), next(it)

    n_heads = 2 * pairs
    q_start = pl.program_id(2) * tq
    lane = _lane_iota(tq)
    low_half = lane < HEAD_DIM
    qh = []
    for h in range(n_heads):
        if mode == "pair":
            q = q_ref[0, :, (h // 2) * LANES:(h // 2 + 1) * LANES]
            zero = jnp.zeros_like(q)
            qh.append(jnp.where(low_half, q, zero) if h % 2 == 0 else jnp.where(low_half, zero, q))
        else:
            qh.append(q_ref[0, :, h * LANES:(h + 1) * LANES])

    m_sc[...] = jnp.full(m_sc.shape, M_INIT, F32)
    l_sc[...] = jnp.zeros(l_sc.shape, F32)
    acc_sc[...] = jnp.zeros(acc_sc.shape, F32)
    row = q_start + lax.broadcasted_iota(I32, (tq, LANES), 0)
    n_chunk = tk // LANES

    def step(kv, masked):
        ks = pl.multiple_of(kv * tk, tk)
        kblk = k_ref[0, pl.ds(ks, tk), :]
        vblk = v_ref[0, pl.ds(ks, tk), :]
        if masked:
            masks = []
            for c in range(n_chunk):
                col = ks + c * LANES + lane
                mk = col <= row
                if window:
                    mk = mk & ((row - col) < window)
                masks.append(mk)
        for h in range(n_heads):
            if mode == "pair":
                kk = kblk[:, (h // 2) * LANES:(h // 2 + 1) * LANES]
                vv = vblk[:, (h // 2) * LANES:(h // 2 + 1) * LANES]
            else:
                kk = kblk[:, (h % 2) * LANES:(h % 2 + 1) * LANES]
                vv = vblk
            s = _dot_nt(qh[h], kk)
            if has_bias:
                s = s - kb_ref[0, h // 2, kv][h % 2:h % 2 + 1, :]
            chunks = [s[:, c * LANES:(c + 1) * LANES] for c in range(n_chunk)]
            if masked:
                chunks = [jnp.where(mk, ch, -jnp.inf) for mk, ch in zip(masks, chunks)]
            mb = functools.reduce(jnp.maximum, chunks)
            m_old = m_sc[h]
            m_new = jnp.maximum(m_old, jnp.broadcast_to(jnp.max(mb, axis=-1, keepdims=True), (tq, LANES)))
            alpha = jnp.exp2(m_old - m_new)
            ps = [jnp.exp2(ch - m_new) for ch in chunks]
            l_sc[h] = alpha * l_sc[h] + functools.reduce(jnp.add, ps)
            p = ps[0] if n_chunk == 1 else jnp.concatenate(ps, axis=1)
            acc_sc[h] = alpha * acc_sc[h] + _dot(p.astype(BF16), vv)
            m_sc[h] = m_new

    def loop(lo, hi, masked):
        def body(kv, carry):
            step(kv, masked)
            return carry
        lax.fori_loop(lo, hi, body, 0)

    last_blk = (q_start + (tq - 1)) // tk
    full_hi = (q_start + 1) // tk
    if window:
        first_blk = jnp.maximum(q_start - (window - 1), 0) // tk
        full_lo = jnp.maximum(q_start + (tq - 1) - window + tk, 0) // tk
        full_lo = jnp.maximum(jnp.minimum(full_lo, full_hi), first_blk)
        loop(first_blk, full_lo, True)
    else:
        full_lo = 0
    loop(full_lo, full_hi, False)
    loop(jnp.maximum(full_hi, full_lo), last_blk + 1, True)

    if fin == "diff":
        lam_rows = lam_ref[...]
        lam = (jnp.exp(jnp.sum(lam_rows[0:1] * lam_rows[1:2], axis=-1, keepdims=True))
               - jnp.exp(jnp.sum(lam_rows[2:3] * lam_rows[3:4], axis=-1, keepdims=True)) + lam_init)
    for pi in range(pairs):
        l0 = jnp.maximum(jnp.sum(l_sc[2 * pi], axis=-1, keepdims=True), TINY)
        l1 = jnp.maximum(jnp.sum(l_sc[2 * pi + 1], axis=-1, keepdims=True), TINY)
        o0 = acc_sc[2 * pi] * (1.0 / l0)
        o1 = acc_sc[2 * pi + 1] * (1.0 / l1)
        if fin == "select":
            o = jnp.where(low_half, o0, o1)
        else:
            o = o0 - lam * o1
            ms = jnp.mean(o * o, axis=-1, keepdims=True)
            o = o * lax.rsqrt(ms + EPS) * sg_ref[...] * (1.0 - lam_init)
        o_ref[0, :, pi * LANES:(pi + 1) * LANES] = o.astype(o_ref.dtype)


def flash_attention(q, k, v, *, q_off, k_off, v_off, k_div=1, mode="pair", fin="select", window=0,
                    key_bias=None, lam=None, sub_gain=None, lam_init=0.0, pairs=FLASH_PAIRS,
                    tq=FLASH_TILE, tk=FLASH_TILE):
    nb, s, _ = q.shape
    tq, tk = min(tq, s), min(tk, s)
    n_inner = N_PAIRS // pairs
    wo = pairs * LANES
    if mode == "pair":
        wq = wk = wv = wo
    else:
        wq, wk, wv = 2 * wo, 2 * LANES, LANES
    in_specs = [pl.BlockSpec((1, tq, wq), lambda b, j, i: (b, i, q_off // (wq // LANES) + j)),
                pl.BlockSpec((1, s, wk), lambda b, j, i: (b, 0, k_off // (wk // LANES) + j // k_div)),
                pl.BlockSpec((1, s, wv), lambda b, j, i: (b, 0, v_off // (wv // LANES) + j // k_div))]
    args = [q, k, v]
    if key_bias is not None:
        in_specs.append(pl.BlockSpec((1, pairs, s // tk, 2, tk), lambda b, j, i: (b, j, 0, 0, 0)))
        args.append(key_bias)
    if fin == "diff":
        in_specs += [pl.BlockSpec((4, LANES), lambda b, j, i: (0, 0)),
                     pl.BlockSpec((1, LANES), lambda b, j, i: (0, 0))]
        args += [lam, sub_gain]
    return pl.pallas_call(
        functools.partial(_flash_kernel, tq=tq, tk=tk, window=window, mode=mode, fin=fin,
                          has_bias=key_bias is not None, pairs=pairs, lam_init=lam_init),
        grid=(nb, n_inner, s // tq),
        in_specs=in_specs,
        out_specs=pl.BlockSpec((1, tq, wo), lambda b, j, i: (b, i, j)),
        out_shape=jax.ShapeDtypeStruct((nb, s, D_MODEL), BF16),
        scratch_shapes=[pltpu.VMEM((2 * pairs, tq, LANES), F32)] * 3,
        compiler_params=_cp("parallel", "parallel", "arbitrary"),
        name="flash_" + mode + "_" + fin,
    )(*args)


BAND_SUB = LANES


def _band_kernel(q_ref, k_ref, v_ref, o_ref, lse_ref, *, tq, span, window):
    seq = k_ref.shape[1]
    sub = min(BAND_SUB, tq)
    q_start = pl.program_id(2) * tq
    lane = _lane_iota(sub)
    low_half = lane < HEAD_DIM
    low_half_kv = _lane_iota(span) < HEAD_DIM
    for r in range(tq // sub):
        q0 = q_start + r * sub
        ks = pl.multiple_of(jnp.minimum(jnp.maximum(q0 - sub, 0), seq - span), sub)
        kblk = k_ref[0, pl.ds(ks, span), :]
        vblk = v_ref[0, pl.ds(ks, span), :]
        ones = jnp.ones_like(vblk)
        q = q_ref[0, r * sub:(r + 1) * sub, :]
        zero = jnp.zeros_like(q)
        row = q0 + lax.broadcasted_iota(I32, (sub, LANES), 0)
        masks = []
        for c in range(span // LANES):
            col = ks + c * LANES + lane
            masks.append((col <= row) & ((row - col) < window))
        outs, lses = [], []
        for h in range(2):
            own = low_half if h == 0 else jnp.logical_not(low_half)
            own_kv = low_half_kv if h == 0 else jnp.logical_not(low_half_kv)
            s = _dot_nt(jnp.where(own, q, zero), kblk)
            chunks = [jnp.where(mk, s[:, c * LANES:(c + 1) * LANES], -jnp.inf) for c, mk in enumerate(masks)]
            m = jnp.broadcast_to(jnp.max(functools.reduce(jnp.maximum, chunks), axis=-1, keepdims=True),
                                 (sub, LANES))
            ps = [jnp.exp2(ch - m) for ch in chunks]
            p = ps[0] if len(ps) == 1 else jnp.concatenate(ps, axis=1)
            acc = _dot(p.astype(BF16), jnp.where(own_kv, vblk, ones))
            l_own = jnp.maximum(pltpu.roll(acc, HEAD_DIM, 1), TINY)
            outs.append(acc * (1.0 / l_own))
            lses.append(m * LN2 + jnp.log(l_own))
        o_ref[0, r * sub:(r + 1) * sub, :] = jnp.where(low_half, outs[0], outs[1]).astype(o_ref.dtype)
        lse_ref[0, r * sub:(r + 1) * sub, :] = jnp.where(low_half, lses[0], lses[1])


def band_attention(q, k, v, *, q_blk, k_blk, v_blk, n_batch, n_inner, out_cols, out_blk, window, tq=512):
    nb, s, _ = q.shape
    tq = min(tq, s)
    span = min(2 * BAND_SUB, s)
    assert window <= span - min(BAND_SUB, tq) + 1 or span == s
    blk = lambda f, rows: pl.BlockSpec((1, rows, LANES), f)
    return pl.pallas_call(
        functools.partial(_band_kernel, tq=tq, span=span, window=window),
        grid=(n_batch, n_inner, s // tq),
        in_specs=[blk(lambda b, j, i: (b, i, q_blk(j)), tq),
                  blk(lambda b, j, i: (b, 0, k_blk(j)), s),
                  blk(lambda b, j, i: (b, 0, v_blk(j)), s)],
        out_specs=[blk(lambda b, j, i: (b, i, out_blk(j)), tq), blk(lambda b, j, i: (b, i, out_blk(j)), tq)],
        out_shape=[jax.ShapeDtypeStruct((nb, s, out_cols), BF16), jax.ShapeDtypeStruct((nb, s, out_cols), F32)],
        compiler_params=_cp("parallel", "parallel", "parallel"),
        name="band_attention",
    )(q, k, v)


def _nsa_prep_kernel(q_ref, ks_ref, vs_ref, kw_ref, vw_ref, g_ref, cos_ref, sin_ref,
                     qo_ref, ks2_ref, vs2_ref, kw2_ref, vw2_ref):
    ts = q_ref.shape[1]
    bd = _head_block_diag()
    cos, sin = cos_ref[...], sin_ref[...]
    lane = _lane_iota(ts)
    low_half = lane < HEAD_DIM
    for c in range(N_PAIRS):
        sl = slice(c * LANES, (c + 1) * LANES)
        qo_ref[0, :, sl] = _head_norm_rope(q_ref[0, :, sl].astype(F32), g_ref[0:1, :], cos, sin, bd,
                                           True).astype(qo_ref.dtype)
    t = pl.program_id(1) * ts + lax.broadcasted_iota(I32, (ts, LANES), 0)
    blk_onehot = ((t // NSA_BLOCK) == (lane % HEAD_DIM)).astype(F32)
    zeros = jnp.zeros((ts, LANES), F32)

    def spread(x, fill, out_ref, c):
        xr = pltpu.roll(x, HEAD_DIM, 1)
        base = 2 * c * 2 * LANES
        out_ref[0, :, base:base + LANES] = jnp.where(low_half, x, fill).astype(out_ref.dtype)
        out_ref[0, :, base + LANES:base + 2 * LANES] = jnp.where(low_half, fill, xr).astype(out_ref.dtype)
        out_ref[0, :, base + 2 * LANES:base + 3 * LANES] = jnp.where(low_half, xr, fill).astype(out_ref.dtype)
        out_ref[0, :, base + 3 * LANES:base + 4 * LANES] = jnp.where(low_half, fill, x).astype(out_ref.dtype)

    def dup(x, out_ref, c):
        xr = pltpu.roll(x, HEAD_DIM, 1)
        out_ref[0, :, 2 * c * LANES:(2 * c + 1) * LANES] = jnp.where(low_half, x, xr).astype(out_ref.dtype)
        out_ref[0, :, (2 * c + 1) * LANES:(2 * c + 2) * LANES] = jnp.where(low_half, xr, x).astype(out_ref.dtype)

    for c in range(NSA_GROUPS // 2):
        sl = slice(c * LANES, (c + 1) * LANES)
        ks = _head_norm_rope(ks_ref[0, :, sl].astype(F32), g_ref[1:2, :], cos, sin, bd, True)
        kw = _head_norm_rope(kw_ref[0, :, sl].astype(F32), g_ref[2:3, :], cos, sin, bd, True)
        spread(ks, blk_onehot, ks2_ref, c)
        spread(kw, zeros, kw2_ref, c)
        dup(vs_ref[0, :, sl].astype(F32), vs2_ref, c)
        dup(vw_ref[0, :, sl].astype(F32), vw2_ref, c)


def nsa_prep(main, gains, cos, sin, *, ts=512):
    b, s, _ = main.shape
    ts = min(ts, s)
    gw = NSA_GROUPS * HEAD_DIM

    def kv_spec(i):
        return pl.BlockSpec((1, ts, gw), lambda bi, si: (bi, si, i))

    return pl.pallas_call(
        _nsa_prep_kernel,
        grid=(b, s // ts),
        in_specs=[pl.BlockSpec((1, ts, D_MODEL), lambda bi, si: (bi, si, 0)),
                  kv_spec(6), kv_spec(7), kv_spec(8), kv_spec(9),
                  pl.BlockSpec((8, LANES), lambda bi, si: (0, 0)),
                  pl.BlockSpec((ts, LANES), lambda bi, si: (si, 0)),
                  pl.BlockSpec((ts, LANES), lambda bi, si: (si, 0))],
        out_specs=[pl.BlockSpec((1, ts, D_MODEL), lambda bi, si: (bi, si, 0)),
                   pl.BlockSpec((1, ts, NSA_GROUPS * 2 * LANES), lambda bi, si: (bi, si, 0)),
                   pl.BlockSpec((1, ts, NSA_GROUPS * LANES), lambda bi, si: (bi, si, 0)),
                   pl.BlockSpec((1, ts, NSA_GROUPS * 2 * LANES), lambda bi, si: (bi, si, 0)),
                   pl.BlockSpec((1, ts, NSA_GROUPS * LANES), lambda bi, si: (bi, si, 0))],
        out_shape=[jax.ShapeDtypeStruct((b, s, D_MODEL), BF16),
                   jax.ShapeDtypeStruct((b, s, NSA_GROUPS * 2 * LANES), BF16),
                   jax.ShapeDtypeStruct((b, s, NSA_GROUPS * LANES), BF16),
                   jax.ShapeDtypeStruct((b, s, NSA_GROUPS * 2 * LANES), BF16),
                   jax.ShapeDtypeStruct((b, s, NSA_GROUPS * LANES), BF16)],
        compiler_params=_cp("parallel", "parallel"),
        name="nsa_prep",
    )(main, main, main, main, main, gains, cos, sin)


def _nsa_compress_kernel(x_ref, pos_ref, w1_ref, w2_ref, g_ref, cos_ref, sin_ref, o_ref, *, is_key):
    x = (x_ref[...].astype(F32) + pos_ref[...]).astype(BF16)
    hid = _silu(_dot(x, w1_ref[...]))
    y = _dot(hid.astype(BF16), w2_ref[...])
    if is_key:
        ms = jnp.mean(y * y, axis=-1, keepdims=True)
        y = y * lax.rsqrt(ms + EPS) * g_ref[...]
        r = lax.broadcasted_iota(I32, (HEAD_DIM, HEAD_DIM), 0)
        c = lax.broadcasted_iota(I32, (HEAD_DIM, HEAD_DIM), 1)
        swap = (((r + HEAD_DIM // 2) % HEAD_DIM) == c).astype(BF16)
        y = y * cos_ref[...] + _dot_f32_exact_rhs(y, swap) * sin_ref[...]
    o_ref[...] = y


def nsa_compress(x, pos, w1, w2, gain, cos_blk, sin_blk, *, is_key):
    rows, k = x.shape
    nb = cos_blk.shape[0]
    hid = w1.shape[1]
    return pl.pallas_call(
        functools.partial(_nsa_compress_kernel, is_key=is_key),
        grid=(rows // nb,),
        in_specs=[pl.BlockSpec((nb, k), lambda i: (i, 0)),
                  pl.BlockSpec((1, k), lambda i: (0, 0)),
                  pl.BlockSpec((k, hid), lambda i: (0, 0)),
                  pl.BlockSpec((hid, HEAD_DIM), lambda i: (0, 0)),
                  pl.BlockSpec((1, HEAD_DIM), lambda i: (0, 0)),
                  pl.BlockSpec((nb, HEAD_DIM), lambda i: (0, 0)),
                  pl.BlockSpec((nb, HEAD_DIM), lambda i: (0, 0))],
        out_specs=pl.BlockSpec((nb, HEAD_DIM), lambda i: (i, 0)),
        out_shape=jax.ShapeDtypeStruct((rows, HEAD_DIM), F32),
        compiler_params=_cp("parallel"),
        name="nsa_compress",
    )(x, pos, w1, w2, gain, cos_blk, sin_blk)


def _nsa_cmp_kernel(q_ref, kc_ref, vc_ref, o_ref, qa_ref, *, n_sel):
    tq = q_ref.shape[1]
    lane = _lane_iota(tq)
    low_half = lane < HEAD_DIM
    blk = lane % HEAD_DIM
    t = pl.program_id(2) * tq + lax.broadcasted_iota(I32, (tq, LANES), 0)
    cmask = (blk + 1) * NSA_BLOCK <= t + 1
    kc = kc_ref[0, 0]
    vc = vc_ref[0, 0]
    imp = jnp.zeros((tq, LANES), F32)
    qblk = [q_ref[0, :, 0:LANES], q_ref[0, :, LANES:2 * LANES]]
    zero = jnp.zeros_like(qblk[0])
    outs = [None, None]
    for p in range(NSA_HPG):
        in_low = (p % 2) == 0
        qb = qblk[p // 2]
        qm = jnp.where(low_half, qb, zero) if in_low else jnp.where(low_half, zero, qb)
        s = jnp.where(cmask, _dot_nt(qm, kc), NEG_INF)
        m = jnp.max(s, axis=-1, keepdims=True)
        e = jnp.where(cmask, jnp.exp2(s - m), 0.0)
        den = jnp.maximum(0.5 * jnp.sum(e, axis=-1, keepdims=True), TINY)
        pc = e / den
        imp = imp + pc
        o = _dot(pc.astype(BF16), vc)
        prev = outs[p // 2]
        outs[p // 2] = o if prev is None else jnp.where(low_half, prev, o)
    o_ref[0, :, 0:LANES] = outs[0].astype(o_ref.dtype)
    o_ref[0, :, LANES:2 * LANES] = outs[1].astype(o_ref.dtype)

    cur = t // NSA_BLOCK
    forced = (blk == 0) | (blk == cur) | (blk == cur - 1)
    x = jnp.where(blk > cur, -1.0, jnp.where(forced, NSA_HPG + 1.0, imp))
    blk_f = blk.astype(F32)
    selected = jnp.zeros((tq, LANES), jnp.bool_)
    for _ in range(n_sel):
        mx = jnp.max(x, axis=-1, keepdims=True)
        first = jnp.min(jnp.where(x == mx, blk_f, float(LANES)), axis=-1, keepdims=True)
        hit = blk_f == first
        selected = selected | hit
        x = jnp.where(hit, -2.0, x)
    sel_bias = jnp.where(selected, 0.0, NEG_INF).astype(qa_ref.dtype)
    for p in range(NSA_HPG):
        qb = qblk[p // 2]
        qa = jnp.where(low_half, qb, sel_bias) if p % 2 == 0 else jnp.where(low_half, sel_bias, qb)
        qa_ref[0, :, p * LANES:(p + 1) * LANES] = qa


def nsa_compressed_attention(q, kc2, vc2, *, n_sel, tq=1024):
    b, s, _ = q.shape
    tq = min(tq, s)
    gq = NSA_HPG * HEAD_DIM
    return pl.pallas_call(
        functools.partial(_nsa_cmp_kernel, n_sel=n_sel),
        grid=(b, NSA_GROUPS, s // tq),
        in_specs=[pl.BlockSpec((1, tq, gq), lambda bi, g, i: (bi, i, g)),
                  pl.BlockSpec((1, 1, LANES, LANES), lambda bi, g, i: (bi, g, 0, 0)),
                  pl.BlockSpec((1, 1, LANES, LANES), lambda bi, g, i: (bi, g, 0, 0))],
        out_specs=[pl.BlockSpec((1, tq, gq), lambda bi, g, i: (bi, i, g)),
                   pl.BlockSpec((1, tq, 2 * gq), lambda bi, g, i: (bi, i, g))],
        out_shape=[jax.ShapeDtypeStruct((b, s, D_MODEL), BF16),
                   jax.ShapeDtypeStruct((b, s, 2 * D_MODEL), BF16)],
        compiler_params=_cp("parallel", "parallel", "parallel"),
        name="nsa_compressed_attention",
    )(q, kc2, vc2)


def _route(logits_t, rb):
    scores = _sigmoid(logits_t)
    sel = scores + rb
    rows = [sel[i:i + 1, :] for i in range(N_EXPERTS)]
    srows = [scores[i:i + 1, :] for i in range(N_EXPERTS)]
    best = grp = None
    for g in range(N_EXPERT_GROUPS):
        a, b, c, d = rows[4 * g:4 * g + 4]
        hi1, lo1, hi2, lo2 = jnp.maximum(a, b), jnp.minimum(a, b), jnp.maximum(c, d), jnp.minimum(c, d)
        gs = jnp.maximum(hi1, hi2) + jnp.maximum(jnp.minimum(hi1, hi2), jnp.maximum(lo1, lo2))
        if g == 0:
            best, grp = gs, jnp.zeros(gs.shape, I32)
        else:
            better = gs > best
            grp = jnp.where(better, g, grp)
            best = jnp.where(better, gs, best)

    def pick(vals, i):
        out = vals[i]
        for g in range(1, N_EXPERT_GROUPS):
            out = jnp.where(grp == g, vals[4 * g + i], out)
        return out

    v = [pick(rows, i) for i in range(EXPERTS_PER_GROUP)]
    w = [pick(srows, i) for i in range(EXPERTS_PER_GROUP)]
    l1, b1, w1 = jnp.zeros(grp.shape, I32), v[0], w[0]
    for i in range(1, EXPERTS_PER_GROUP):
        better = v[i] > b1
        l1 = jnp.where(better, i, l1)
        b1 = jnp.where(better, v[i], b1)
        w1 = jnp.where(better, w[i], w1)
    have = jnp.zeros(grp.shape, jnp.bool_)
    l2, b2, w2 = jnp.zeros(grp.shape, I32), jnp.zeros_like(b1), jnp.zeros_like(w1)
    for i in range(EXPERTS_PER_GROUP):
        valid = l1 != i
        better = valid & (jnp.logical_not(have) | (v[i] > b2))
        l2 = jnp.where(better, i, l2)
        b2 = jnp.where(better, v[i], b2)
        w2 = jnp.where(better, w[i], w2)
        have = have | valid
    wsum = w1 + w2
    return (grp * EXPERTS_PER_GROUP + l1, grp * EXPERTS_PER_GROUP + l2), (w1 / wsum, w2 / wsum)


def _outproj_router_kernel(*refs, mix):
    it = iter(refs)
    if mix == "plain":
        x = next(it)[0]
    elif mix == "nsa":
        o_refs = [next(it), next(it), next(it)]
        gl = _sigmoid(next(it)[0])
        g_hi = gl.astype(BF16)
        g_lo = (gl - g_hi.astype(F32)).astype(BF16)
        r = lax.broadcasted_iota(I32, (LANES, D_MODEL), 0)
        c = lax.broadcasted_iota(I32, (LANES, D_MODEL), 1) // HEAD_DIM
        x = None
        for i in range(3):
            expand = (r == c + i * N_HEADS).astype(BF16)
            term = (_dot(g_hi, expand) + _dot(g_lo, expand)) * o_refs[i][0].astype(F32)
            x = term if x is None else x + term
        x = x.astype(BF16)
    else:
        o_refs = [next(it), next(it), next(it)]
        lse = [next(it)[0], next(it)[0], next(it)[0]]
        mx = jnp.maximum(jnp.maximum(lse[0], lse[1]), lse[2])
        ex = [jnp.exp(l - mx) for l in lse]
        den = ex[0] + ex[1] + ex[2]
        x = ((ex[0] / den) * o_refs[0][0].astype(F32) + (ex[1] / den) * o_refs[1][0].astype(F32)
             + (ex[2] / den) * o_refs[2][0].astype(F32)).astype(BF16)
    h_ref, mod_ref, w_ref, g2_ref, rw_ref, rb_ref = (next(it) for _ in range(6))
    ho_ref, u_ref, e_ref, wt_ref = (next(it) for _ in range(4))
    mod = mod_ref[0]
    h_new = h_ref[0] + mod[2:3, :] * _dot(x, w_ref[...])
    ho_ref[0] = h_new
    u = _modulated_norm(h_new, g2_ref[...], mod, 3, 4)
    _store_token_tiles(u_ref, u)
    logits_t = _dot_f32(rw_ref[...], u, dot=_dot_nt)
    eidx, wts = _route(logits_t, rb_ref[:, 0:1])
    for k in range(TOP_K):
        e_ref[0, k:k + 1, :] = eidx[k]
        wt_ref[0, k:k + 1, :] = wts[k]


def outproj_router(attn_inputs, h, mod, w_out, gain2, router_wt, router_b, *, mix, tm=256):
    b, s, d = h.shape
    tm = min(tm, s)
    row_spec = pl.BlockSpec((1, tm, d), lambda bi, si: (bi, si, 0))
    in_specs, args = [], []
    for a in attn_inputs:
        in_specs.append(pl.BlockSpec((1, tm, a.shape[2]), lambda bi, si: (bi, si, 0)))
        args.append(a)
    in_specs += [row_spec,
                 pl.BlockSpec((1, 6, d), lambda bi, si: (bi, 0, 0)),
                 pl.BlockSpec((d, d), lambda bi, si: (0, 0)),
                 pl.BlockSpec((1, d), lambda bi, si: (0, 0)),
                 pl.BlockSpec((N_EXPERTS, d), lambda bi, si: (0, 0)),
                 pl.BlockSpec((N_EXPERTS, LANES), lambda bi, si: (0, 0))]
    args += [h, mod, w_out, gain2.reshape(1, d), router_wt, router_b]
    return pl.pallas_call(
        functools.partial(_outproj_router_kernel, mix=mix),
        grid=(b, s // tm),
        in_specs=in_specs,
        out_specs=[row_spec, pl.BlockSpec((tm * ROW_TILE, LANES), lambda bi, si: (bi * (s // tm) + si, 0)),
                   pl.BlockSpec((1, TOP_K, tm), lambda bi, si: (bi, 0, si)),
                   pl.BlockSpec((1, TOP_K, tm), lambda bi, si: (bi, 0, si))],
        out_shape=[jax.ShapeDtypeStruct((b, s, d), F32), jax.ShapeDtypeStruct((b * s * ROW_TILE, LANES), F32),
                   jax.ShapeDtypeStruct((b, TOP_K, s), I32), jax.ShapeDtypeStruct((b, TOP_K, s), F32)],
        compiler_params=_cp("parallel", "parallel"),
        name="outproj_router_" + mix,
    )(*args)


ROW_TILE = D_MODEL // LANES
FFN_PIECES = 6


def _store_token_tiles(ref, x):
    rows = x.shape[0]
    for c in range(ROW_TILE):
        ref[pl.ds(c, rows, stride=ROW_TILE), :] = x[:, c * LANES:(c + 1) * LANES]


def _load_token_tiles(ref, rows):
    return jnp.concatenate([ref[pl.ds(c, rows, stride=ROW_TILE), :] for c in range(ROW_TILE)], axis=1)


def _tile_copy(src_hbm, index, dst, r, sem):
    start = pl.multiple_of(index * ROW_TILE, ROW_TILE)
    return pltpu.make_async_copy(src_hbm.at[pl.ds(start, ROW_TILE), :], dst.at[pl.ds(r * ROW_TILE, ROW_TILE), :], sem)


def _gather_tiles_loop(src_hbm, index_of, dst, n_rows, sem):
    def one(r, carry):
        _tile_copy(src_hbm, index_of(r), dst, r, sem).start()
        return carry
    lax.fori_loop(0, n_rows, one, 0)


def _wait_tiles(src_hbm, dst, sem):
    pltpu.make_async_copy(src_hbm.at[pl.ds(0, dst.shape[0]), :], dst, sem).wait()


def _moe_ffn_kernel(be_ref, tok_cur, tok_next, u_hbm, wg_ref, wu_ref, wd_ref, y_ref, xbuf, wg_sc, wu_sc, wd_sc, sem):
    i = pl.program_id(0)
    n = pl.num_programs(0)
    rows = MOE_ROWS
    slot = i % 2
    nxt = 1 - slot

    @pl.when(i == 0)
    def _():
        _gather_tiles_loop(u_hbm, lambda r: tok_cur[r], xbuf.at[0], rows, sem.at[0])

    @pl.when((i == 0) | (be_ref[i] != be_ref[jnp.maximum(i - 1, 0)]))
    def _():
        wg_sc[...] = wg_ref[0, 0].astype(BF16)
        wu_sc[...] = wu_ref[0, 0].astype(BF16)
        wd_sc[...] = wd_ref[0, 0].astype(BF16)

    _wait_tiles(u_hbm, xbuf.at[slot], sem.at[slot])

    per = rows // FFN_PIECES + 1
    issued = [0]

    def issue_some():
        lo, hi = issued[0], min(issued[0] + per, rows)
        for r in range(lo, hi):
            _tile_copy(u_hbm, tok_next[r], xbuf.at[nxt], r, sem.at[nxt]).start(priority=r % 2)
        issued[0] = hi

    x = _load_token_tiles(xbuf.at[slot], rows).astype(BF16)
    g = _dot(x, wg_sc[...])
    issue_some()
    u = _dot(x, wu_sc[...])
    issue_some()
    hid = (_silu(g) * u).astype(BF16)
    d = wd_sc.shape[1]
    n_out = FFN_PIECES - 2
    w = d // n_out
    for c in range(n_out):
        y = _dot(hid, wd_sc[:, c * w:(c + 1) * w])
        for j in range(w // LANES):
            y_ref[pl.ds(c * (w // LANES) + j, rows, stride=ROW_TILE), :] = y[:, j * LANES:(j + 1) * LANES]
        issue_some()
    assert issued[0] == rows

    @pl.when(i == n - 1)
    def _():
        _wait_tiles(u_hbm, xbuf.at[nxt], sem.at[nxt])


def moe_expert_ffn(u_tiles, row_token, blk_expert, w_gate, w_up, w_down, layer):
    r_total = row_token.shape[0]
    nblk = r_total // MOE_ROWS
    d, ff = w_gate.shape[2], w_gate.shape[3]
    smem_blk = functools.partial(pl.BlockSpec, (MOE_ROWS,), memory_space=pltpu.SMEM)
    return pl.pallas_call(
        _moe_ffn_kernel,
        grid_spec=pltpu.PrefetchScalarGridSpec(
            num_scalar_prefetch=1,
            grid=(nblk,),
            in_specs=[smem_blk(lambda i, be: (i,)),
                      smem_blk(lambda i, be: (jnp.minimum(i + 1, nblk - 1),)),
                      pl.BlockSpec(memory_space=pl.ANY),
                      pl.BlockSpec((1, 1, d, ff), lambda i, be: (layer, be[i], 0, 0)),
                      pl.BlockSpec((1, 1, d, ff), lambda i, be: (layer, be[i], 0, 0)),
                      pl.BlockSpec((1, 1, ff, d), lambda i, be: (layer, be[i], 0, 0))],
            out_specs=pl.BlockSpec((MOE_ROWS * ROW_TILE, LANES), lambda i, be: (i, 0)),
            scratch_shapes=[pltpu.VMEM((2, MOE_ROWS * ROW_TILE, LANES), F32), pltpu.VMEM((d, ff), BF16),
                            pltpu.VMEM((d, ff), BF16), pltpu.VMEM((ff, d), BF16), pltpu.SemaphoreType.DMA((2,))]),
        out_shape=jax.ShapeDtypeStruct((r_total * ROW_TILE, LANES), F32),
        compiler_params=_cp("arbitrary"),
        name="moe_expert_ffn",
    )(blk_expert, row_token, row_token, u_tiles, w_gate, w_up, w_down)


def _moe_combine_kernel(d_cur, d_next, ys_hbm, h_ref, mod_ref, w_ref, o_ref, buf, sem):
    i = pl.program_id(0)
    n = pl.num_programs(0)
    tm = h_ref.shape[0]
    slot = i % 2
    nxt = 1 - slot

    @pl.when(i == 0)
    def _():
        for k in range(TOP_K):
            _gather_tiles_loop(ys_hbm, lambda r, k=k: d_cur[k, r], buf.at[0, k], tm, sem.at[0])

    for r in range(tm):
        for k in range(TOP_K):
            _tile_copy(ys_hbm, d_next[k, r], buf.at[nxt, k], r, sem.at[nxt]).start(priority=k)

    for k in range(TOP_K):
        _wait_tiles(ys_hbm, buf.at[slot, k], sem.at[slot])
    w = w_ref[...]
    y = w[:, 0:1] * _load_token_tiles(buf.at[slot, 0], tm) + w[:, 1:2] * _load_token_tiles(buf.at[slot, 1], tm)
    o_ref[...] = h_ref[...] + mod_ref[0, 5:6, :] * y

    @pl.when(i == n - 1)
    def _():
        for k in range(TOP_K):
            _wait_tiles(ys_hbm, buf.at[nxt, k], sem.at[nxt])


def moe_combine(ys, dest, h, mod, wts, *, tm=256):
    b, s, d = h.shape
    t = b * s
    tm = min(tm, s)
    per_b = s // tm
    n = t // tm
    smem_blk = functools.partial(pl.BlockSpec, (TOP_K, tm), memory_space=pltpu.SMEM)
    out = pl.pallas_call(
        _moe_combine_kernel,
        grid=(n,),
        in_specs=[smem_blk(lambda i: (0, i)),
                  smem_blk(lambda i: (0, jnp.minimum(i + 1, n - 1))),
                  pl.BlockSpec(memory_space=pl.ANY),
                  pl.BlockSpec((tm, d), lambda i: (i, 0)),
                  pl.BlockSpec((1, 6, d), lambda i: (i // per_b, 0, 0)),
                  pl.BlockSpec((tm, TOP_K), lambda i: (i, 0))],
        out_specs=pl.BlockSpec((tm, d), lambda i: (i, 0)),
        out_shape=jax.ShapeDtypeStruct((t, d), F32),
        scratch_shapes=[pltpu.VMEM((2, TOP_K, tm * ROW_TILE, LANES), F32), pltpu.SemaphoreType.DMA((2,))],
        compiler_params=_cp("arbitrary"),
        name="moe_combine",
    )(dest, dest, ys, h.reshape(t, d), mod, wts)
    return out.reshape(b, s, d)


def moe_layer(h, u, eidx, wts, mod, w_gate, w_up, w_down, layer):
    b, s, d = h.shape
    t = b * s
    e_flat = eidx.transpose(0, 2, 1).reshape(-1)
    n_pairs = t * TOP_K
    r_total = n_pairs + N_EXPERTS * MOE_ROWS
    nblk = r_total // MOE_ROWS
    onehot = (e_flat[:, None] == jnp.arange(N_EXPERTS, dtype=I32)[None, :]).astype(I32)
    csum = jnp.cumsum(onehot, axis=0)
    counts = csum[-1]
    rank = jnp.take_along_axis(csum, e_flat[:, None], axis=1)[:, 0] - 1
    padded = (counts + MOE_ROWS - 1) // MOE_ROWS * MOE_ROWS
    pad_end = jnp.cumsum(padded)
    pad_start = pad_end - padded
    dest = (pad_start[e_flat] + rank).astype(I32)
    row_token = jnp.zeros((r_total,), I32).at[dest].set(jnp.arange(n_pairs, dtype=I32) // TOP_K)
    blk_expert = jnp.minimum(jnp.searchsorted(pad_end, jnp.arange(nblk, dtype=I32) * MOE_ROWS, side="right"),
                             N_EXPERTS - 1).astype(I32)
    ys = moe_expert_ffn(u, row_token, blk_expert, w_gate, w_up, w_down, layer)
    dest2 = dest.reshape(t, TOP_K).T
    w_tok = wts.transpose(0, 2, 1).reshape(t, TOP_K)
    return moe_combine(ys, dest2, h, mod, w_tok)


def _pad_cols(w, n):
    return jnp.pad(w, ((0, 0), (0, n - w.shape[1])))


def fox_attention(h, mod, gain, w_in, b_f, q_gain, k_gain, cos, sin):
    b, s, _ = h.shape
    n_main = 3 * D_MODEL
    main, tail = norm_proj(h, mod, gain, w_in[:, :n_main].astype(BF16),
                           _pad_cols(w_in[:, n_main:], LANES).astype(BF16))
    gains = jnp.stack([_tile_gain(q_gain, Q_SCALE), _tile_gain(k_gain)])
    qk = head_prep(main, gains, (0, 1), cos, sin, rope=False)
    tk = min(FLASH_TILE, s)
    cum = fox_cumulative_gate(tail, b_f)
    key_bias = cum.reshape(b, N_PAIRS, 2, s // tk, tk).transpose(0, 1, 3, 2, 4)
    return flash_attention(qk, qk, main, q_off=0, k_off=N_PAIRS, v_off=2 * N_PAIRS, key_bias=key_bias, tk=tk)


def diff_attention(h, mod, gain, w_in, q_gain, k_gain, lambdas, sub_gain, layer_idx, cos, sin):
    b, s, _ = h.shape
    main = norm_proj(h, mod, gain, w_in.astype(BF16))
    gains = jnp.stack([_tile_gain(q_gain, Q_SCALE), _tile_gain(k_gain)])
    qk = head_prep(main, gains, (0, 1), cos, sin, rope=True)
    lam_init = 0.8 - 0.6 * math.exp(-0.3 * layer_idx)
    lam = jnp.pad(lambdas.astype(F32), ((0, 0), (0, LANES - HEAD_DIM)))
    return flash_attention(qk, qk, main, q_off=0, k_off=N_PAIRS, v_off=2 * N_PAIRS, fin="diff", lam=lam,
                           sub_gain=sub_gain.astype(F32).reshape(1, LANES), lam_init=lam_init)


def dilated_attention(h, mod, gain, w_in, q_gain, k_gain, cos, sin):
    b, s, _ = h.shape
    ng = len(DIL_PAIRS)
    main = norm_proj(h, mod, gain, w_in.astype(BF16), tn=3 * D_MODEL)
    gq, gk = _tile_gain(q_gain, Q_SCALE), _tile_gain(k_gain)
    qk = head_prep(main, jnp.stack([gq, gk] * ng), tuple(3 * g + j for g in range(ng) for j in range(2)),
                   cos, sin, rope=True)
    outs, lses = [], []
    for g, (window, dil) in enumerate(DIL_PAIRS):
        sd = s // dil
        if dil == 1:
            qk_g, v_g, q0, v0 = qk, main, 2 * g * N_PAIRS, (3 * g + 2) * N_PAIRS
        else:
            qk_g = qk[:, :, 2 * g * D_MODEL:(2 * g + 2) * D_MODEL].reshape(b, sd, dil * 2 * D_MODEL)
            v_g = main[:, :, (3 * g + 2) * D_MODEL:(3 * g + 3) * D_MODEL].reshape(b, sd, dil * D_MODEL)
            q0 = v0 = 0
        o, lse = band_attention(
            qk_g, qk_g, v_g,
            q_blk=lambda j, q0=q0: q0 + (j // N_PAIRS) * 2 * N_PAIRS + j % N_PAIRS,
            k_blk=lambda j, q0=q0: q0 + (j // N_PAIRS) * 2 * N_PAIRS + N_PAIRS + j % N_PAIRS,
            v_blk=lambda j, v0=v0: v0 + j,
            n_batch=b, n_inner=dil * N_PAIRS, out_cols=dil * D_MODEL, out_blk=lambda j: j,
            window=window // dil + 1)
        outs.append(o.reshape(b, s, D_MODEL))
        lses.append(lse.reshape(b, s, D_MODEL))
    return outs + lses


def nsa_attention(h, mod, gain, w_in, q_gain, k_gain, cmp_pos, cmp_w1, cmp_w2, cos, sin, cos_h, sin_h):
    b, s, _ = h.shape
    nblk = s // NSA_BLOCK
    n_main = D_MODEL + 6 * NSA_GROUPS * HEAD_DIM
    main, tail = norm_proj(h, mod, gain, w_in[:, :n_main].astype(BF16),
                           _pad_cols(w_in[:, n_main:], LANES).astype(BF16))
    gains = jnp.zeros((8, LANES), F32)
    gains = gains.at[0].set(jnp.tile(q_gain.astype(F32) * Q_SCALE, 2))
    gains = gains.at[1].set(jnp.tile(k_gain[1].astype(F32), 2)).at[2].set(jnp.tile(k_gain[2].astype(F32), 2))
    q, ks2, vs2, kw2, vw2 = nsa_prep(main, gains, cos, sin)

    def to_block_rows(col0):
        x = main[:, :, col0:col0 + NSA_GROUPS * HEAD_DIM].reshape(b, nblk, NSA_BLOCK, NSA_GROUPS, HEAD_DIM)
        return x.transpose(0, 3, 1, 2, 4).reshape(b * NSA_GROUPS * nblk, NSA_BLOCK * HEAD_DIM)

    cos_b = jnp.tile(cos_h[NSA_BLOCK - 1::NSA_BLOCK], (1, 2))
    sin_b = jnp.concatenate([-sin_h[NSA_BLOCK - 1::NSA_BLOCK], sin_h[NSA_BLOCK - 1::NSA_BLOCK]], axis=-1)
    kc = nsa_compress(to_block_rows(D_MODEL), cmp_pos[0].reshape(1, -1), cmp_w1[0].astype(BF16),
                      cmp_w2[0].astype(BF16), k_gain[0].astype(F32).reshape(1, HEAD_DIM), cos_b, sin_b, is_key=True)
    vc = nsa_compress(to_block_rows(D_MODEL + NSA_GROUPS * HEAD_DIM), cmp_pos[1].reshape(1, -1),
                      cmp_w1[1].astype(BF16), cmp_w2[1].astype(BF16),
                      k_gain[0].astype(F32).reshape(1, HEAD_DIM), cos_b, sin_b, is_key=False)
    kc = kc.reshape(b, NSA_GROUPS, nblk, HEAD_DIM)
    vc = vc.reshape(b, NSA_GROUPS, nblk, HEAD_DIM)
    pad_rows = ((0, 0), (0, 0), (0, HEAD_DIM - nblk), (0, 0))
    kc = jnp.pad(kc, pad_rows)
    vc = jnp.pad(vc, pad_rows)
    kc2 = jnp.tile(kc, (1, 1, 2, 2)).astype(BF16)
    vc2 = jnp.concatenate([jnp.tile(vc, (1, 1, 1, 2)), jnp.zeros_like(jnp.tile(vc, (1, 1, 1, 2)))],
                          axis=2).astype(BF16)
    o_cmp, q_aug = nsa_compressed_attention(q, kc2, vc2, n_sel=min(NSA_TOPN, nblk))
    common = dict(q_off=0, k_off=0, v_off=0, k_div=NSA_HPG // (2 * FLASH_PAIRS), mode="aug")
    o_sel = flash_attention(q_aug, ks2, vs2, **common)
    o_win = flash_attention(q_aug, kw2, vw2, window=NSA_WINDOW, **common)
    return [o_cmp, o_sel, o_win, tail]


def kernel(x, c, fox_w_in, fox_b_f, fox_q_gain, fox_k_gain, fox_w_out, nsa_w_in, nsa_q_gain, nsa_k_gain, nsa_cmp_pos, nsa_cmp_w1, nsa_cmp_w2, nsa_w_out, dil_w_in, dil_q_gain, dil_k_gain, dil_w_out, diff_w_in, diff_q_gain, diff_k_gain, diff_lambda, diff_sub_gain, diff_w_out, norm_gain, ada_w, ada_b, router_w, router_b, moe_w_gate, moe_w_up, moe_w_down):
    b, s, d = x.shape
    depth = norm_gain.shape[0]
    cos, sin, cos_h, sin_h = rope_lane_tables(s)
    mods = ada_modulation(c, ada_w, ada_b).reshape(depth, b, 6, d)
    router_wt = router_w.T.astype(F32)
    router_bb = jnp.broadcast_to(router_b.astype(F32)[:, None], (N_EXPERTS, LANES))
    h = x
    for i in range(depth):
        mod = mods[i]
        kind, j = i % 4, i // 4
        g1 = norm_gain[i, 0]
        if kind == 0:
            attn = [fox_attention(h, mod, g1, fox_w_in[j], fox_b_f[j], fox_q_gain[j], fox_k_gain[j], cos, sin)]
            w_out, mix = fox_w_out[j], "plain"
        elif kind == 1:
            attn = nsa_attention(h, mod, g1, nsa_w_in[j], nsa_q_gain[j], nsa_k_gain[j], nsa_cmp_pos[j],
                                 nsa_cmp_w1[j], nsa_cmp_w2[j], cos, sin, cos_h, sin_h)
            w_out, mix = nsa_w_out[j], "nsa"
        elif kind == 2:
            attn = dilated_attention(h, mod, g1, dil_w_in[j], dil_q_gain[j], dil_k_gain[j], cos, sin)
            w_out, mix = dil_w_out[j], "dil"
        else:
            attn = [diff_attention(h, mod, g1, diff_w_in[j], diff_q_gain[j], diff_k_gain[j], diff_lambda[j],
                                   diff_sub_gain[j], i, cos, sin)]
            w_out, mix = diff_w_out[j], "plain"
        h, u, eidx, wts = outproj_router(attn, h, mod, w_out.astype(BF16), norm_gain[i, 1], router_wt,
                                         router_bb, mix=mix)
        h = moe_layer(h, u, eidx, wts, mod, moe_w_gate, moe_w_up, moe_w_down, i)
    return h
```

```python
import functools
import math

import jax
import jax.numpy as jnp
from jax import lax
from jax.experimental import pallas as pl
from jax.experimental.pallas import tpu as pltpu

F32 = jnp.float32
BF16 = jnp.bfloat16
I32 = jnp.int32

D_MODEL = 1024
HEAD_DIM = 64
LANES = 128
N_HEADS = D_MODEL // HEAD_DIM
N_PAIRS = D_MODEL // LANES
ROPE_THETA = 10000.0
EPS = 1e-6
NEG_INF = -1e30
TINY = 1e-30
M_INIT = -1e29
LOG2E = 1.4426950408889634
LN2 = 0.6931471805599453
Q_SCALE = HEAD_DIM ** -0.5 * LOG2E

NSA_GROUPS = 4
NSA_HPG = N_HEADS // NSA_GROUPS
NSA_BLOCK = 64
NSA_TOPN = 16
NSA_WINDOW = 512
DIL_PAIRS = ((128, 1), (512, 4), (2048, 16))

N_EXPERTS = 16
N_EXPERT_GROUPS = 4
EXPERTS_PER_GROUP = 4
TOP_K = 2
EXPERT_FF = 512
MOE_ROWS = 256
FLASH_TILE = 512
FLASH_PAIRS = 2

VMEM_LIMIT = 52 * 1024 * 1024


def _cp(*sem, vmem=VMEM_LIMIT):
    return pltpu.CompilerParams(dimension_semantics=sem, vmem_limit_bytes=vmem)


def _split3(a):
    hi = a.astype(BF16)
    r1 = a - hi.astype(F32)
    mid = r1.astype(BF16)
    lo = (r1 - mid.astype(F32)).astype(BF16)
    return hi, mid, lo


def _dot(a, b):
    return jnp.dot(a, b, preferred_element_type=F32)


def _dot_nt(a, b):
    return lax.dot_general(a, b, (((1,), (1,)), ((), ())), preferred_element_type=F32)


def _dot_f32(a, b, dot=_dot):
    ah, am, al = _split3(a)
    bh, bm, bl = _split3(b)
    return (dot(ah, bh) + (dot(ah, bm) + dot(am, bh))
            + (dot(ah, bl) + dot(al, bh) + dot(am, bm)))


def _dot_f32_exact_rhs(a, b_bf16):
    ah, am, al = _split3(a)
    return _dot(ah, b_bf16) + _dot(am, b_bf16) + _dot(al, b_bf16)


def _sigmoid(x):
    return 1.0 / (1.0 + jnp.exp(-x))


def _silu(x):
    return x * _sigmoid(x)


def _ada_kernel(c_ref, w_ref, b_ref, o_ref):
    c = c_ref[...]
    o_ref[0] = _dot_f32(_silu(c), w_ref[0]) + b_ref[0]


def ada_modulation(c, ada_w, ada_b):
    depth, d, n = ada_w.shape
    b = c.shape[0]
    tn = 1024
    return pl.pallas_call(
        _ada_kernel,
        grid=(depth, n // tn),
        in_specs=[pl.BlockSpec((b, d), lambda i, j: (0, 0)),
                  pl.BlockSpec((1, d, tn), lambda i, j: (i, 0, j)),
                  pl.BlockSpec((1, 1, tn), lambda i, j: (i, 0, j))],
        out_specs=pl.BlockSpec((1, b, tn), lambda i, j: (i, 0, j)),
        out_shape=jax.ShapeDtypeStruct((depth, b, n), F32),
        compiler_params=_cp("parallel", "parallel"),
        name="ada_modulation",
    )(c, ada_w, ada_b.reshape(depth, 1, n))


def _modulated_norm(x, gain, mod, shift_row, scale_row):
    ms = jnp.mean(x * x, axis=-1, keepdims=True)
    y = x * lax.rsqrt(ms + EPS) * gain
    return y * (1.0 + mod[scale_row:scale_row + 1, :]) + mod[shift_row:shift_row + 1, :]


def _norm_proj_kernel(h_ref, mod_ref, g_ref, w_ref, *rest, has_tail):
    u = _modulated_norm(h_ref[0], g_ref[...], mod_ref[0], 0, 1).astype(BF16)
    if has_tail:
        wt_ref, main_ref, tail_ref = rest
        tail_ref[0] = _dot(u, wt_ref[...])
    else:
        (main_ref,) = rest
    main_ref[0] = _dot(u, w_ref[...]).astype(main_ref.dtype)


def norm_proj(h, mod, gain, w_main, w_tail=None, *, tn=None, tm=512):
    b, s, d = h.shape
    n = w_main.shape[1]
    tn = tn or n
    tm = min(tm, s)
    in_specs = [pl.BlockSpec((1, tm, d), lambda j, bi, si: (bi, si, 0)),
                pl.BlockSpec((1, 6, d), lambda j, bi, si: (bi, 0, 0)),
                pl.BlockSpec((1, d), lambda j, bi, si: (0, 0)),
                pl.BlockSpec((d, tn), lambda j, bi, si: (0, j))]
    out_specs = [pl.BlockSpec((1, tm, tn), lambda j, bi, si: (bi, si, j))]
    out_shape = [jax.ShapeDtypeStruct((b, s, n), BF16)]
    args = [h, mod, gain.reshape(1, d), w_main]
    if w_tail is not None:
        in_specs.append(pl.BlockSpec((d, LANES), lambda j, bi, si: (0, 0)))
        out_specs.append(pl.BlockSpec((1, tm, LANES), lambda j, bi, si: (bi, si, 0)))
        out_shape.append(jax.ShapeDtypeStruct((b, s, LANES), F32))
        args.append(w_tail)
    outs = pl.pallas_call(
        functools.partial(_norm_proj_kernel, has_tail=w_tail is not None),
        grid=(n // tn, b, s // tm),
        in_specs=in_specs, out_specs=out_specs, out_shape=out_shape,
        compiler_params=_cp("parallel", "parallel", "parallel"),
        name="norm_proj",
    )(*args)
    return outs if w_tail is not None else outs[0]


def _lane_iota(rows):
    return lax.broadcasted_iota(I32, (rows, LANES), 1)


def _head_block_diag():
    r = lax.broadcasted_iota(I32, (LANES, LANES), 0) // HEAD_DIM
    c = lax.broadcasted_iota(I32, (LANES, LANES), 1) // HEAD_DIM
    return (r == c).astype(BF16)


def _head_norm_rope(x, gain, cos, sin, bd, rope):
    y = x * x
    hi = y.astype(BF16)
    lo = (y - hi.astype(F32)).astype(BF16)
    seg = _dot(hi, bd) + _dot(lo, bd)
    xn = x * lax.rsqrt(seg * (1.0 / HEAD_DIM) + EPS) * gain
    if rope:
        first_half = (_lane_iota(x.shape[0]) % HEAD_DIM) < HEAD_DIM // 2
        partner = jnp.where(first_half, pltpu.roll(xn, LANES - HEAD_DIM // 2, 1),
                            pltpu.roll(xn, HEAD_DIM // 2, 1))
        xn = xn * cos + partner * sin
    return xn


def _prep_kernel(cb_ref, x_ref, g_ref, cos_ref, sin_ref, o_ref, *, rope):
    del cb_ref
    bd = _head_block_diag()
    cos = cos_ref[...]
    sin = sin_ref[...]
    for c in range(x_ref.shape[2] // LANES):
        sl = slice(c * LANES, (c + 1) * LANES)
        x = x_ref[0, :, sl].astype(F32)
        o_ref[0, :, sl] = _head_norm_rope(x, g_ref[0, :, sl], cos, sin, bd, rope).astype(o_ref.dtype)


def head_prep(src, gains, col_blocks, cos, sin, *, rope, ts=512):
    b, s, _ = src.shape
    n = len(col_blocks)
    ts = min(ts, s)
    cb = jnp.asarray(col_blocks, I32)

    def x_map(bi, si, ci, cb_ref):
        return (bi, si, cb_ref[ci])

    return pl.pallas_call(
        functools.partial(_prep_kernel, rope=rope),
        grid_spec=pltpu.PrefetchScalarGridSpec(
            num_scalar_prefetch=1,
            grid=(b, s // ts, n),
            in_specs=[pl.BlockSpec((1, ts, D_MODEL), x_map),
                      pl.BlockSpec((1, 1, D_MODEL), lambda bi, si, ci, cb_ref: (ci, 0, 0)),
                      pl.BlockSpec((ts, LANES), lambda bi, si, ci, cb_ref: (si, 0)),
                      pl.BlockSpec((ts, LANES), lambda bi, si, ci, cb_ref: (si, 0))],
            out_specs=pl.BlockSpec((1, ts, D_MODEL), lambda bi, si, ci, cb_ref: (bi, si, ci))),
        out_shape=jax.ShapeDtypeStruct((b, s, n * D_MODEL), BF16),
        compiler_params=_cp("parallel", "parallel", "arbitrary"),
        name="head_prep",
    )(cb, src, gains, cos, sin)


def rope_lane_tables(s):
    inv = ROPE_THETA ** (-jnp.arange(0, HEAD_DIM, 2, dtype=F32) / HEAD_DIM)
    ang = jnp.arange(s).astype(F32)[:, None] * inv[None, :]
    cos, sin = jnp.cos(ang), jnp.sin(ang)
    return jnp.tile(cos, (1, 4)), jnp.tile(jnp.concatenate([-sin, sin], axis=-1), (1, 2)), cos, sin


def _tile_gain(g, scale=1.0):
    return jnp.tile(g.astype(F32) * scale, N_HEADS).reshape(1, D_MODEL)


def _fox_cum_kernel(f_ref, b_ref, o_ref, carry_ref):
    si = pl.program_id(1)
    ts = f_ref.shape[1]

    @pl.when(si == 0)
    def _():
        carry_ref[...] = jnp.zeros_like(carry_ref)

    z = f_ref[0] + b_ref[...]
    log_f = -(jnp.maximum(-z, 0.0) + jnp.log1p(jnp.exp(-jnp.abs(z))))
    r = lax.broadcasted_iota(I32, (ts, ts), 0)
    c = lax.broadcasted_iota(I32, (ts, ts), 1)
    upper = (r <= c).astype(BF16)
    cum = _dot_f32_exact_rhs(log_f.T, upper) + carry_ref[:, 0:1]
    o_ref[0] = cum[0:N_HEADS, :] * LOG2E
    carry_ref[...] = jnp.broadcast_to(cum[:, ts - 1:ts], carry_ref.shape)


def fox_cumulative_gate(tail, b_f, *, ts=256):
    b, s, _ = tail.shape
    ts = min(ts, s)
    bias = jnp.zeros((1, LANES), F32).at[0, :N_HEADS].set(b_f.astype(F32))
    return pl.pallas_call(
        _fox_cum_kernel,
        grid=(b, s // ts),
        in_specs=[pl.BlockSpec((1, ts, LANES), lambda bi, si: (bi, si, 0)),
                  pl.BlockSpec((1, LANES), lambda bi, si: (0, 0))],
        out_specs=pl.BlockSpec((1, N_HEADS, ts), lambda bi, si: (bi, 0, si)),
        out_shape=jax.ShapeDtypeStruct((b, N_HEADS, s), F32),
        scratch_shapes=[pltpu.VMEM((LANES, LANES), F32)],
        compiler_params=_cp("parallel", "arbitrary"),
        name="fox_cumulative_gate",
    )(tail, bias)


def _flash_kernel(*refs, tq, tk, window, mode, fin, has_bias, pairs, lam_init):
    it = iter(refs)
    q_ref, k_ref, v_ref = next(it), next(it), next(it)
    kb_ref = next(it) if has_bias else None
    if fin == "diff":
        lam_ref, sg_ref = next(it), next(it)
    o_ref = next(it)
    m_sc, l_sc, acc_sc = next(it), next(it), next(it)

    n_heads = 2 * pairs
    q_start = pl.program_id(2) * tq
    lane = _lane_iota(tq)
    low_half = lane < HEAD_DIM
    qh = []
    for h in range(n_heads):
        if mode == "pair":
            q = q_ref[0, :, (h // 2) * LANES:(h // 2 + 1) * LANES]
            zero = jnp.zeros_like(q)
            qh.append(jnp.where(low_half, q, zero) if h % 2 == 0 else jnp.where(low_half, zero, q))
        else:
            qh.append(q_ref[0, :, h * LANES:(h + 1) * LANES])

    m_sc[...] = jnp.full(m_sc.shape, M_INIT, F32)
    l_sc[...] = jnp.zeros(l_sc.shape, F32)
    acc_sc[...] = jnp.zeros(acc_sc.shape, F32)
    row = q_start + lax.broadcasted_iota(I32, (tq, LANES), 0)
    n_chunk = tk // LANES

    def step(kv, masked):
        ks = pl.multiple_of(kv * tk, tk)
        kblk = k_ref[0, pl.ds(ks, tk), :]
        vblk = v_ref[0, pl.ds(ks, tk), :]
        if masked:
            masks = []
            for c in range(n_chunk):
                col = ks + c * LANES + lane
                mk = col <= row
                if window:
                    mk = mk & ((row - col) < window)
                masks.append(mk)
        for h in range(n_heads):
            if mode == "pair":
                kk = kblk[:, (h // 2) * LANES:(h // 2 + 1) * LANES]
                vv = vblk[:, (h // 2) * LANES:(h // 2 + 1) * LANES]
            else:
                kk = kblk[:, (h % 2) * LANES:(h % 2 + 1) * LANES]
                vv = vblk
            s = _dot_nt(qh[h], kk)
            if has_bias:
                s = s - kb_ref[0, h // 2, kv][h % 2:h % 2 + 1, :]
            chunks = [s[:, c * LANES:(c + 1) * LANES] for c in range(n_chunk)]
            if masked:
                chunks = [jnp.where(mk, ch, -jnp.inf) for mk, ch in zip(masks, chunks)]
            mb = functools.reduce(jnp.maximum, chunks)
            m_old = m_sc[h]
            m_new = jnp.maximum(m_old, jnp.broadcast_to(jnp.max(mb, axis=-1, keepdims=True), (tq, LANES)))
            alpha = jnp.exp2(m_old - m_new)
            ps = [jnp.exp2(ch - m_new) for ch in chunks]
            l_sc[h] = alpha * l_sc[h] + functools.reduce(jnp.add, ps)
            p = ps[0] if n_chunk == 1 else jnp.concatenate(ps, axis=1)
            acc_sc[h] = alpha * acc_sc[h] + _dot(p.astype(BF16), vv)
            m_sc[h] = m_new

    def loop(lo, hi, masked):
        def body(kv, carry):
            step(kv, masked)
            return carry
        lax.fori_loop(lo, hi, body, 0)

    last_blk = (q_start + (tq - 1)) // tk
    full_hi = (q_start + 1) // tk
    if window:
        first_blk = jnp.maximum(q_start - (window - 1), 0) // tk
        full_lo = jnp.maximum(q_start + (tq - 1) - window + tk, 0) // tk
        full_lo = jnp.maximum(jnp.minimum(full_lo, full_hi), first_blk)
        loop(first_blk, full_lo, True)
    else:
        full_lo = 0
    loop(full_lo, full_hi, False)
    loop(jnp.maximum(full_hi, full_lo), last_blk + 1, True)

    if fin == "diff":
        lam_rows = lam_ref[...]
        lam = (jnp.exp(jnp.sum(lam_rows[0:1] * lam_rows[1:2], axis=-1, keepdims=True))
               - jnp.exp(jnp.sum(lam_rows[2:3] * lam_rows[3:4], axis=-1, keepdims=True)) + lam_init)
    for pi in range(pairs):
        l0 = jnp.maximum(jnp.sum(l_sc[2 * pi], axis=-1, keepdims=True), TINY)
        l1 = jnp.maximum(jnp.sum(l_sc[2 * pi + 1], axis=-1, keepdims=True), TINY)
        o0 = acc_sc[2 * pi] * (1.0 / l0)
        o1 = acc_sc[2 * pi + 1] * (1.0 / l1)
        if fin == "select":
            o = jnp.where(low_half, o0, o1)
        else:
            o = o0 - lam * o1
            ms = jnp.mean(o * o, axis=-1, keepdims=True)
            o = o * lax.rsqrt(ms + EPS) * sg_ref[...] * (1.0 - lam_init)
        o_ref[0, :, pi * LANES:(pi + 1) * LANES] = o.astype(o_ref.dtype)


def flash_attention(q, k, v, *, q_off, k_off, v_off, k_div=1, mode="pair", fin="select", window=0,
                    key_bias=None, lam=None, sub_gain=None, lam_init=0.0, pairs=FLASH_PAIRS,
                    tq=FLASH_TILE, tk=FLASH_TILE):
    nb, s, _ = q.shape
    tq, tk = min(tq, s), min(tk, s)
    n_inner = N_PAIRS // pairs
    wo = pairs * LANES
    if mode == "pair":
        wq = wk = wv = wo
    else:
        wq, wk, wv = 2 * wo, 2 * LANES, LANES
    in_specs = [pl.BlockSpec((1, tq, wq), lambda b, j, i: (b, i, q_off // (wq // LANES) + j)),
                pl.BlockSpec((1, s, wk), lambda b, j, i: (b, 0, k_off // (wk // LANES) + j // k_div)),
                pl.BlockSpec((1, s, wv), lambda b, j, i: (b, 0, v_off // (wv // LANES) + j // k_div))]
    args = [q, k, v]
    if key_bias is not None:
        in_specs.append(pl.BlockSpec((1, pairs, s // tk, 2, tk), lambda b, j, i: (b, j, 0, 0, 0)))
        args.append(key_bias)
    if fin == "diff":
        in_specs += [pl.BlockSpec((4, LANES), lambda b, j, i: (0, 0)),
                     pl.BlockSpec((1, LANES), lambda b, j, i: (0, 0))]
        args += [lam, sub_gain]
    return pl.pallas_call(
        functools.partial(_flash_kernel, tq=tq, tk=tk, window=window, mode=mode, fin=fin,
                          has_bias=key_bias is not None, pairs=pairs, lam_init=lam_init),
        grid=(nb, n_inner, s // tq),
        in_specs=in_specs,
        out_specs=pl.BlockSpec((1, tq, wo), lambda b, j, i: (b, i, j)),
        out_shape=jax.ShapeDtypeStruct((nb, s, D_MODEL), BF16),
        scratch_shapes=[pltpu.VMEM((2 * pairs, tq, LANES), F32)] * 3,
        compiler_params=_cp("parallel", "parallel", "arbitrary"),
        name="flash_" + mode + "_" + fin,
    )(*args)


BAND_SUB = LANES


def _band_kernel(q_ref, k_ref, v_ref, o_ref, lse_ref, *, tq, span, window):
    seq = k_ref.shape[1]
    sub = min(BAND_SUB, tq)
    q_start = pl.program_id(2) * tq
    lane = _lane_iota(sub)
    low_half = lane < HEAD_DIM
    low_half_kv = _lane_iota(span) < HEAD_DIM
    for r in range(tq // sub):
        q0 = q_start + r * sub
        ks = pl.multiple_of(jnp.minimum(jnp.maximum(q0 - sub, 0), seq - span), sub)
        kblk = k_ref[0, pl.ds(ks, span), :]
        vblk = v_ref[0, pl.ds(ks, span), :]
        ones = jnp.ones_like(vblk)
        q = q_ref[0, r * sub:(r + 1) * sub, :]
        zero = jnp.zeros_like(q)
        row = q0 + lax.broadcasted_iota(I32, (sub, LANES), 0)
        masks = []
        for c in range(span // LANES):
            col = ks + c * LANES + lane
            masks.append((col <= row) & ((row - col) < window))
        outs, lses = [], []
        for h in range(2):
            own = low_half if h == 0 else jnp.logical_not(low_half)
            own_kv = low_half_kv if h == 0 else jnp.logical_not(low_half_kv)
            s = _dot_nt(jnp.where(own, q, zero), kblk)
            chunks = [jnp.where(mk, s[:, c * LANES:(c + 1) * LANES], -jnp.inf) for c, mk in enumerate(masks)]
            m = jnp.broadcast_to(jnp.max(functools.reduce(jnp.maximum, chunks), axis=-1, keepdims=True),
                                 (sub, LANES))
            ps = [jnp.exp2(ch - m) for ch in chunks]
            p = ps[0] if len(ps) == 1 else jnp.concatenate(ps, axis=1)
            acc = _dot(p.astype(BF16), jnp.where(own_kv, vblk, ones))
            l_own = jnp.maximum(pltpu.roll(acc, HEAD_DIM, 1), TINY)
            outs.append(acc * (1.0 / l_own))
            lses.append(m * LN2 + jnp.log(l_own))
        o_ref[0, r * sub:(r + 1) * sub, :] = jnp.where(low_half, outs[0], outs[1]).astype(o_ref.dtype)
        lse_ref[0, r * sub:(r + 1) * sub, :] = jnp.where(low_half, lses[0], lses[1])


def band_attention(q, k, v, *, q_blk, k_blk, v_blk, n_batch, n_inner, out_cols, out_blk, window, tq=512):
    nb, s, _ = q.shape
    tq = min(tq, s)
    span = min(2 * BAND_SUB, s)
    assert window <= span - min(BAND_SUB, tq) + 1 or span == s
    blk = lambda f, rows: pl.BlockSpec((1, rows, LANES), f)
    return pl.pallas_call(
        functools.partial(_band_kernel, tq=tq, span=span, window=window),
        grid=(n_batch, n_inner, s // tq),
        in_specs=[blk(lambda b, j, i: (b, i, q_blk(j)), tq),
                  blk(lambda b, j, i: (b, 0, k_blk(j)), s),
                  blk(lambda b, j, i: (b, 0, v_blk(j)), s)],
        out_specs=[blk(lambda b, j, i: (b, i, out_blk(j)), tq), blk(lambda b, j, i: (b, i, out_blk(j)), tq)],
        out_shape=[jax.ShapeDtypeStruct((nb, s, out_cols), BF16), jax.ShapeDtypeStruct((nb, s, out_cols), F32)],
        compiler_params=_cp("parallel", "parallel", "parallel"),
        name="band_attention",
    )(q, k, v)


def _nsa_prep_kernel(q_ref, ks_ref, vs_ref, kw_ref, vw_ref, g_ref, cos_ref, sin_ref,
                     qo_ref, ks2_ref, vs2_ref, kw2_ref, vw2_ref):
    ts = q_ref.shape[1]
    bd = _head_block_diag()
    cos, sin = cos_ref[...], sin_ref[...]
    lane = _lane_iota(ts)
    low_half = lane < HEAD_DIM
    for c in range(N_PAIRS):
        sl = slice(c * LANES, (c + 1) * LANES)
        qo_ref[0, :, sl] = _head_norm_rope(q_ref[0, :, sl].astype(F32), g_ref[0:1, :], cos, sin, bd,
                                           True).astype(qo_ref.dtype)
    t = pl.program_id(1) * ts + lax.broadcasted_iota(I32, (ts, LANES), 0)
    blk_onehot = ((t // NSA_BLOCK) == (lane % HEAD_DIM)).astype(F32)
    zeros = jnp.zeros((ts, LANES), F32)

    def spread(x, fill, out_ref, c):
        xr = pltpu.roll(x, HEAD_DIM, 1)
        base = 2 * c * 2 * LANES
        out_ref[0, :, base:base + LANES] = jnp.where(low_half, x, fill).astype(out_ref.dtype)
        out_ref[0, :, base + LANES:base + 2 * LANES] = jnp.where(low_half, fill, xr).astype(out_ref.dtype)
        out_ref[0, :, base + 2 * LANES:base + 3 * LANES] = jnp.where(low_half, xr, fill).astype(out_ref.dtype)
        out_ref[0, :, base + 3 * LANES:base + 4 * LANES] = jnp.where(low_half, fill, x).astype(out_ref.dtype)

    def dup(x, out_ref, c):
        xr = pltpu.roll(x, HEAD_DIM, 1)
        out_ref[0, :, 2 * c * LANES:(2 * c + 1) * LANES] = jnp.where(low_half, x, xr).astype(out_ref.dtype)
        out_ref[0, :, (2 * c + 1) * LANES:(2 * c + 2) * LANES] = jnp.where(low_half, xr, x).astype(out_ref.dtype)

    for c in range(NSA_GROUPS // 2):
        sl = slice(c * LANES, (c + 1) * LANES)
        ks = _head_norm_rope(ks_ref[0, :, sl].astype(F32), g_ref[1:2, :], cos, sin, bd, True)
        kw = _head_norm_rope(kw_ref[0, :, sl].astype(F32), g_ref[2:3, :], cos, sin, bd, True)
        spread(ks, blk_onehot, ks2_ref, c)
        spread(kw, zeros, kw2_ref, c)
        dup(vs_ref[0, :, sl].astype(F32), vs2_ref, c)
        dup(vw_ref[0, :, sl].astype(F32), vw2_ref, c)


def nsa_prep(main, gains, cos, sin, *, ts=512):
    b, s, _ = main.shape
    ts = min(ts, s)
    gw = NSA_GROUPS * HEAD_DIM

    def kv_spec(i):
        return pl.BlockSpec((1, ts, gw), lambda bi, si: (bi, si, i))

    return pl.pallas_call(
        _nsa_prep_kernel,
        grid=(b, s // ts),
        in_specs=[pl.BlockSpec((1, ts, D_MODEL), lambda bi, si: (bi, si, 0)),
                  kv_spec(6), kv_spec(7), kv_spec(8), kv_spec(9),
                  pl.BlockSpec((8, LANES), lambda bi, si: (0, 0)),
                  pl.BlockSpec((ts, LANES), lambda bi, si: (si, 0)),
                  pl.BlockSpec((ts, LANES), lambda bi, si: (si, 0))],
        out_specs=[pl.BlockSpec((1, ts, D_MODEL), lambda bi, si: (bi, si, 0)),
                   pl.BlockSpec((1, ts, NSA_GROUPS * 2 * LANES), lambda bi, si: (bi, si, 0)),
                   pl.BlockSpec((1, ts, NSA_GROUPS * LANES), lambda bi, si: (bi, si, 0)),
                   pl.BlockSpec((1, ts, NSA_GROUPS * 2 * LANES), lambda bi, si: (bi, si, 0)),
                   pl.BlockSpec((1, ts, NSA_GROUPS * LANES), lambda bi, si: (bi, si, 0))],
        out_shape=[jax.ShapeDtypeStruct((b, s, D_MODEL), BF16),
                   jax.ShapeDtypeStruct((b, s, NSA_GROUPS * 2 * LANES), BF16),
                   jax.ShapeDtypeStruct((b, s, NSA_GROUPS * LANES), BF16),
                   jax.ShapeDtypeStruct((b, s, NSA_GROUPS * 2 * LANES), BF16),
                   jax.ShapeDtypeStruct((b, s, NSA_GROUPS * LANES), BF16)],
        compiler_params=_cp("parallel", "parallel"),
        name="nsa_prep",
    )(main, main, main, main, main, gains, cos, sin)


def _nsa_compress_kernel(x_ref, pos_ref, w1_ref, w2_ref, g_ref, cos_ref, sin_ref, o_ref, *, is_key):
    x = (x_ref[...].astype(F32) + pos_ref[...]).astype(BF16)
    hid = _silu(_dot(x, w1_ref[...]))
    y = _dot(hid.astype(BF16), w2_ref[...])
    if is_key:
        ms = jnp.mean(y * y, axis=-1, keepdims=True)
        y = y * lax.rsqrt(ms + EPS) * g_ref[...]
        r = lax.broadcasted_iota(I32, (HEAD_DIM, HEAD_DIM), 0)
        c = lax.broadcasted_iota(I32, (HEAD_DIM, HEAD_DIM), 1)
        swap = (((r + HEAD_DIM // 2) % HEAD_DIM) == c).astype(BF16)
        y = y * cos_ref[...] + _dot_f32_exact_rhs(y, swap) * sin_ref[...]
    o_ref[...] = y


def nsa_compress(x, pos, w1, w2, gain, cos_blk, sin_blk, *, is_key):
    rows, k = x.shape
    nb = cos_blk.shape[0]
    hid = w1.shape[1]
    return pl.pallas_call(
        functools.partial(_nsa_compress_kernel, is_key=is_key),
        grid=(rows // nb,),
        in_specs=[pl.BlockSpec((nb, k), lambda i: (i, 0)),
                  pl.BlockSpec((1, k), lambda i: (0, 0)),
                  pl.BlockSpec((k, hid), lambda i: (0, 0)),
                  pl.BlockSpec((hid, HEAD_DIM), lambda i: (0, 0)),
                  pl.BlockSpec((1, HEAD_DIM), lambda i: (0, 0)),
                  pl.BlockSpec((nb, HEAD_DIM), lambda i: (0, 0)),
                  pl.BlockSpec((nb, HEAD_DIM), lambda i: (0, 0))],
        out_specs=pl.BlockSpec((nb, HEAD_DIM), lambda i: (i, 0)),
        out_shape=jax.ShapeDtypeStruct((rows, HEAD_DIM), F32),
        compiler_params=_cp("parallel"),
        name="nsa_compress",
    )(x, pos, w1, w2, gain, cos_blk, sin_blk)


def _nsa_cmp_kernel(q_ref, kc_ref, vc_ref, o_ref, qa_ref, *, n_sel):
    tq = q_ref.shape[1]
    lane = _lane_iota(tq)
    low_half = lane < HEAD_DIM
    blk = lane % HEAD_DIM
    t = pl.program_id(2) * tq + lax.broadcasted_iota(I32, (tq, LANES), 0)
    cmask = (blk + 1) * NSA_BLOCK <= t + 1
    kc = kc_ref[0, 0]
    vc = vc_ref[0, 0]
    imp = jnp.zeros((tq, LANES), F32)
    qblk = [q_ref[0, :, 0:LANES], q_ref[0, :, LANES:2 * LANES]]
    zero = jnp.zeros_like(qblk[0])
    outs = [None, None]
    for p in range(NSA_HPG):
        in_low = (p % 2) == 0
        qb = qblk[p // 2]
        qm = jnp.where(low_half, qb, zero) if in_low else jnp.where(low_half, zero, qb)
        s = jnp.where(cmask, _dot_nt(qm, kc), NEG_INF)
        m = jnp.max(s, axis=-1, keepdims=True)
        e = jnp.where(cmask, jnp.exp2(s - m), 0.0)
        den = jnp.maximum(0.5 * jnp.sum(e, axis=-1, keepdims=True), TINY)
        pc = e / den
        imp = imp + pc
        o = _dot(pc.astype(BF16), vc)
        prev = outs[p // 2]
        outs[p // 2] = o if prev is None else jnp.where(low_half, prev, o)
    o_ref[0, :, 0:LANES] = outs[0].astype(o_ref.dtype)
    o_ref[0, :, LANES:2 * LANES] = outs[1].astype(o_ref.dtype)

    cur = t // NSA_BLOCK
    forced = (blk == 0) | (blk == cur) | (blk == cur - 1)
    x = jnp.where(blk > cur, -1.0, jnp.where(forced, NSA_HPG + 1.0, imp))
    blk_f = blk.astype(F32)
    selected = jnp.zeros((tq, LANES), jnp.bool_)
    for _ in range(n_sel):
        mx = jnp.max(x, axis=-1, keepdims=True)
        first = jnp.min(jnp.where(x == mx, blk_f, float(LANES)), axis=-1, keepdims=True)
        hit = blk_f == first
        selected = selected | hit
        x = jnp.where(hit, -2.0, x)
    sel_bias = jnp.where(selected, 0.0, NEG_INF).astype(qa_ref.dtype)
    for p in range(NSA_HPG):
        qb = qblk[p // 2]
        qa = jnp.where(low_half, qb, sel_bias) if p % 2 == 0 else jnp.where(low_half, sel_bias, qb)
        qa_ref[0, :, p * LANES:(p + 1) * LANES] = qa


def nsa_compressed_attention(q, kc2, vc2, *, n_sel, tq=1024):
    b, s, _ = q.shape
    tq = min(tq, s)
    gq = NSA_HPG * HEAD_DIM
    return pl.pallas_call(
        functools.partial(_nsa_cmp_kernel, n_sel=n_sel),
        grid=(b, NSA_GROUPS, s // tq),
        in_specs=[pl.BlockSpec((1, tq, gq), lambda bi, g, i: (bi, i, g)),
                  pl.BlockSpec((1, 1, LANES, LANES), lambda bi, g, i: (bi, g, 0, 0)),
                  pl.BlockSpec((1, 1, LANES, LANES), lambda bi, g, i: (bi, g, 0, 0))],
        out_specs=[pl.BlockSpec((1, tq, gq), lambda bi, g, i: (bi, i, g)),
                   pl.BlockSpec((1, tq, 2 * gq), lambda bi, g, i: (bi, i, g))],
        out_shape=[jax.ShapeDtypeStruct((b, s, D_MODEL), BF16),
                   jax.ShapeDtypeStruct((b, s, 2 * D_MODEL), BF16)],
        compiler_params=_cp("parallel", "parallel", "parallel"),
        name="nsa_compressed_attention",
    )(q, kc2, vc2)


def _route(logits_t, rb):
    scores = _sigmoid(logits_t)
    sel = scores + rb
    rows = [sel[i:i + 1, :] for i in range(N_EXPERTS)]
    srows = [scores[i:i + 1, :] for i in range(N_EXPERTS)]
    best = grp = None
    for g in range(N_EXPERT_GROUPS):
        a, b, c, d = rows[4 * g:4 * g + 4]
        hi1, lo1, hi2, lo2 = jnp.maximum(a, b), jnp.minimum(a, b), jnp.maximum(c, d), jnp.minimum(c, d)
        gs = jnp.maximum(hi1, hi2) + jnp.maximum(jnp.minimum(hi1, hi2), jnp.maximum(lo1, lo2))
        if g == 0:
            best, grp = gs, jnp.zeros(gs.shape, I32)
        else:
            better = gs > best
            grp = jnp.where(better, g, grp)
            best = jnp.where(better, gs, best)

    def pick(vals, i):
        out = vals[i]
        for g in range(1, N_EXPERT_GROUPS):
            out = jnp.where(grp == g, vals[4 * g + i], out)
        return out

    v = [pick(rows, i) for i in range(EXPERTS_PER_GROUP)]
    w = [pick(srows, i) for i in range(EXPERTS_PER_GROUP)]
    l1, b1, w1 = jnp.zeros(grp.shape, I32), v[0], w[0]
    for i in range(1, EXPERTS_PER_GROUP):
        better = v[i] > b1
        l1 = jnp.where(better, i, l1)
        b1 = jnp.where(better, v[i], b1)
        w1 = jnp.where(better, w[i], w1)
    have = jnp.zeros(grp.shape, jnp.bool_)
    l2, b2, w2 = jnp.zeros(grp.shape, I32), jnp.zeros_like(b1), jnp.zeros_like(w1)
    for i in range(EXPERTS_PER_GROUP):
        valid = l1 != i
        better = valid & (jnp.logical_not(have) | (v[i] > b2))
        l2 = jnp.where(better, i, l2)
        b2 = jnp.where(better, v[i], b2)
        w2 = jnp.where(better, w[i], w2)
        have = have | valid
    wsum = w1 + w2
    return (grp * EXPERTS_PER_GROUP + l1, grp * EXPERTS_PER_GROUP + l2), (w1 / wsum, w2 / wsum)


def _outproj_router_kernel(*refs, mix):
    it = iter(refs)
    if mix == "plain":
        x = next(it)[0]
    elif mix == "nsa":
        o_refs = [next(it), next(it), next(it)]
        gl = _sigmoid(next(it)[0])
        g_hi = gl.astype(BF16)
        g_lo = (gl - g_hi.astype(F32)).astype(BF16)
        r = lax.broadcasted_iota(I32, (LANES, D_MODEL), 0)
        c = lax.broadcasted_iota(I32, (LANES, D_MODEL), 1) // HEAD_DIM
        x = None
        for i in range(3):
            expand = (r == c + i * N_HEADS).astype(BF16)
            term = (_dot(g_hi, expand) + _dot(g_lo, expand)) * o_refs[i][0].astype(F32)
            x = term if x is None else x + term
        x = x.astype(BF16)
    else:
        o_refs = [next(it), next(it), next(it)]
        lse = [next(it)[0], next(it)[0], next(it)[0]]
        mx = jnp.maximum(jnp.maximum(lse[0], lse[1]), lse[2])
        ex = [jnp.exp(l - mx) for l in lse]
        den = ex[0] + ex[1] + ex[2]
        x = ((ex[0] / den) * o_refs[0][0].astype(F32) + (ex[1] / den) * o_refs[1][0].astype(F32)
             + (ex[2] / den) * o_refs[2][0].astype(F32)).astype(BF16)
    h_ref, mod_ref, w_ref, g2_ref, rw_ref, rb_ref = (next(it) for _ in range(6))
    ho_ref, u_ref, e_ref, wt_ref = (next(it) for _ in range(4))
    mod = mod_ref[0]
    h_new = h_ref[0] + mod[2:3, :] * _dot(x, w_ref[...])
    ho_ref[0] = h_new
    u = _modulated_norm(h_new, g2_ref[...], mod, 3, 4)
    _store_token_tiles(u_ref, u)
    logits_t = _dot_f32(rw_ref[...], u, dot=_dot_nt)
    eidx, wts = _route(logits_t, rb_ref[:, 0:1])
    for k in range(TOP_K):
        e_ref[0, k:k + 1, :] = eidx[k]
        wt_ref[0, k:k + 1, :] = wts[k]


def outproj_router(attn_inputs, h, mod, w_out, gain2, router_wt, router_b, *, mix, tm=512):
    b, s, d = h.shape
    tm = min(tm, s)
    row_spec = pl.BlockSpec((1, tm, d), lambda bi, si: (bi, si, 0))
    in_specs, args = [], []
    for a in attn_inputs:
        in_specs.append(pl.BlockSpec((1, tm, a.shape[2]), lambda bi, si: (bi, si, 0)))
        args.append(a)
    in_specs += [row_spec,
                 pl.BlockSpec((1, 6, d), lambda bi, si: (bi, 0, 0)),
                 pl.BlockSpec((d, d), lambda bi, si: (0, 0)),
                 pl.BlockSpec((1, d), lambda bi, si: (0, 0)),
                 pl.BlockSpec((N_EXPERTS, d), lambda bi, si: (0, 0)),
                 pl.BlockSpec((N_EXPERTS, LANES), lambda bi, si: (0, 0))]
    args += [h, mod, w_out, gain2.reshape(1, d), router_wt, router_b]
    return pl.pallas_call(
        functools.partial(_outproj_router_kernel, mix=mix),
        grid=(b, s // tm),
        in_specs=in_specs,
        out_specs=[row_spec, pl.BlockSpec((tm * ROW_TILE, LANES), lambda bi, si: (bi * (s // tm) + si, 0)),
                   pl.BlockSpec((1, TOP_K, tm), lambda bi, si: (bi, 0, si)),
                   pl.BlockSpec((1, TOP_K, tm), lambda bi, si: (bi, 0, si))],
        out_shape=[jax.ShapeDtypeStruct((b, s, d), F32), jax.ShapeDtypeStruct((b * s * ROW_TILE, LANES), F32),
                   jax.ShapeDtypeStruct((b, TOP_K, s), I32), jax.ShapeDtypeStruct((b, TOP_K, s), F32)],
        compiler_params=_cp("parallel", "parallel"),
        name="outproj_router_" + mix,
    )(*args)


ROW_TILE = D_MODEL // LANES
FFN_PIECES = 6
FFN_SLOTS = 3


def _store_token_tiles(ref, x):
    rows = x.shape[0]
    for c in range(ROW_TILE):
        ref[pl.ds(c, rows, stride=ROW_TILE), :] = x[:, c * LANES:(c + 1) * LANES]


def _load_token_tiles(ref, rows):
    return jnp.concatenate([ref[pl.ds(c, rows, stride=ROW_TILE), :] for c in range(ROW_TILE)], axis=1)


def _tile_copy(src_hbm, index, dst, r, sem):
    start = pl.multiple_of(index * ROW_TILE, ROW_TILE)
    return pltpu.make_async_copy(src_hbm.at[pl.ds(start, ROW_TILE), :], dst.at[pl.ds(r * ROW_TILE, ROW_TILE), :], sem)


def _gather_tiles_loop(src_hbm, index_of, dst, n_rows, sem):
    def one(r, carry):
        _tile_copy(src_hbm, index_of(r), dst, r, sem).start()
        return carry
    lax.fori_loop(0, n_rows, one, 0)


def _wait_tiles(src_hbm, dst, sem):
    pltpu.make_async_copy(src_hbm.at[pl.ds(0, dst.shape[0]), :], dst, sem).wait()


def _moe_ffn_kernel(be_ref, tok_cur, tok_next, tok_ahead, u_hbm, wg_ref, wu_ref, wd_ref, y_ref, xbuf, wg_sc, wu_sc,
                    wd_sc, sem):
    i = pl.program_id(0)
    n = pl.num_programs(0)
    rows = MOE_ROWS
    slot = i % FFN_SLOTS
    nxt = (i + 2) % FFN_SLOTS

    @pl.when(i == 0)
    def _():
        _gather_tiles_loop(u_hbm, lambda r: tok_cur[r], xbuf.at[0], rows, sem.at[0])
        _gather_tiles_loop(u_hbm, lambda r: tok_next[r], xbuf.at[1], rows, sem.at[1])

    @pl.when((i == 0) | (be_ref[i] != be_ref[jnp.maximum(i - 1, 0)]))
    def _():
        wg_sc[...] = wg_ref[0, 0].astype(BF16)
        wu_sc[...] = wu_ref[0, 0].astype(BF16)
        wd_sc[...] = wd_ref[0, 0].astype(BF16)

    _wait_tiles(u_hbm, xbuf.at[slot], sem.at[slot])

    per = rows // FFN_PIECES + 1
    issued = [0]

    def issue_some():
        lo, hi = issued[0], min(issued[0] + per, rows)
        for r in range(lo, hi):
            _tile_copy(u_hbm, tok_ahead[r], xbuf.at[nxt], r, sem.at[nxt]).start(priority=r % 2)
        issued[0] = hi

    x = _load_token_tiles(xbuf.at[slot], rows).astype(BF16)
    g = _dot(x, wg_sc[...])
    issue_some()
    u = _dot(x, wu_sc[...])
    issue_some()
    hid = (_silu(g) * u).astype(BF16)
    d = wd_sc.shape[1]
    n_out = FFN_PIECES - 2
    w = d // n_out
    for c in range(n_out):
        y = _dot(hid, wd_sc[:, c * w:(c + 1) * w])
        for j in range(w // LANES):
            y_ref[pl.ds(c * (w // LANES) + j, rows, stride=ROW_TILE), :] = y[:, j * LANES:(j + 1) * LANES]
        issue_some()
    assert issued[0] == rows

    @pl.when(i == n - 1)
    def _():
        for ahead in (1, 2):
            s = (i + ahead) % FFN_SLOTS
            _wait_tiles(u_hbm, xbuf.at[s], sem.at[s])


def moe_expert_ffn(u_tiles, row_token, blk_expert, w_gate, w_up, w_down, layer):
    r_total = row_token.shape[0]
    nblk = r_total // MOE_ROWS
    d, ff = w_gate.shape[2], w_gate.shape[3]
    smem_blk = functools.partial(pl.BlockSpec, (MOE_ROWS,), memory_space=pltpu.SMEM)
    return pl.pallas_call(
        _moe_ffn_kernel,
        grid_spec=pltpu.PrefetchScalarGridSpec(
            num_scalar_prefetch=1,
            grid=(nblk,),
            in_specs=[smem_blk(lambda i, be: (i,)),
                      smem_blk(lambda i, be: (jnp.minimum(i + 1, nblk - 1),)),
                      smem_blk(lambda i, be: (jnp.minimum(i + 2, nblk - 1),)),
                      pl.BlockSpec(memory_space=pl.ANY),
                      pl.BlockSpec((1, 1, d, ff), lambda i, be: (layer, be[i], 0, 0)),
                      pl.BlockSpec((1, 1, d, ff), lambda i, be: (layer, be[i], 0, 0)),
                      pl.BlockSpec((1, 1, ff, d), lambda i, be: (layer, be[i], 0, 0))],
            out_specs=pl.BlockSpec((MOE_ROWS * ROW_TILE, LANES), lambda i, be: (i, 0)),
            scratch_shapes=[pltpu.VMEM((FFN_SLOTS, MOE_ROWS * ROW_TILE, LANES), F32), pltpu.VMEM((d, ff), BF16),
                            pltpu.VMEM((d, ff), BF16), pltpu.VMEM((ff, d), BF16),
                            pltpu.SemaphoreType.DMA((FFN_SLOTS,))]),
        out_shape=jax.ShapeDtypeStruct((r_total * ROW_TILE, LANES), F32),
        compiler_params=_cp("arbitrary"),
        name="moe_expert_ffn",
    )(blk_expert, row_token, row_token, row_token, u_tiles, w_gate, w_up, w_down)


def _moe_combine_kernel(d_cur, d_next, ys_hbm, h_ref, mod_ref, w_ref, o_ref, buf, sem):
    i = pl.program_id(0)
    n = pl.num_programs(0)
    tm = h_ref.shape[0]
    slot = i % 2
    nxt = 1 - slot

    @pl.when(i == 0)
    def _():
        for k in range(TOP_K):
            _gather_tiles_loop(ys_hbm, lambda r, k=k: d_cur[k, r], buf.at[0, k], tm, sem.at[0])

    for r in range(tm):
        for k in range(TOP_K):
            _tile_copy(ys_hbm, d_next[k, r], buf.at[nxt, k], r, sem.at[nxt]).start(priority=k)

    for k in range(TOP_K):
        _wait_tiles(ys_hbm, buf.at[slot, k], sem.at[slot])
    w = w_ref[...]
    y = w[:, 0:1] * _load_token_tiles(buf.at[slot, 0], tm) + w[:, 1:2] * _load_token_tiles(buf.at[slot, 1], tm)
    o_ref[...] = h_ref[...] + mod_ref[0, 5:6, :] * y

    @pl.when(i == n - 1)
    def _():
        for k in range(TOP_K):
            _wait_tiles(ys_hbm, buf.at[nxt, k], sem.at[nxt])


def moe_combine(ys, dest, h, mod, wts, *, tm=256):
    b, s, d = h.shape
    t = b * s
    tm = min(tm, s)
    per_b = s // tm
    n = t // tm
    smem_blk = functools.partial(pl.BlockSpec, (TOP_K, tm), memory_space=pltpu.SMEM)
    out = pl.pallas_call(
        _moe_combine_kernel,
        grid=(n,),
        in_specs=[smem_blk(lambda i: (0, i)),
                  smem_blk(lambda i: (0, jnp.minimum(i + 1, n - 1))),
                  pl.BlockSpec(memory_space=pl.ANY),
                  pl.BlockSpec((tm, d), lambda i: (i, 0)),
                  pl.BlockSpec((1, 6, d), lambda i: (i // per_b, 0, 0)),
                  pl.BlockSpec((tm, TOP_K), lambda i: (i, 0))],
        out_specs=pl.BlockSpec((tm, d), lambda i: (i, 0)),
        out_shape=jax.ShapeDtypeStruct((t, d), F32),
        scratch_shapes=[pltpu.VMEM((2, TOP_K, tm * ROW_TILE, LANES), F32), pltpu.SemaphoreType.DMA((2,))],
        compiler_params=_cp("arbitrary"),
        name="moe_combine",
    )(dest, dest, ys, h.reshape(t, d), mod, wts)
    return out.reshape(b, s, d)


def moe_layer(h, u, eidx, wts, mod, w_gate, w_up, w_down, layer):
    b, s, d = h.shape
    t = b * s
    e_flat = eidx.transpose(0, 2, 1).reshape(-1)
    n_pairs = t * TOP_K
    r_total = n_pairs + N_EXPERTS * MOE_ROWS
    nblk = r_total // MOE_ROWS
    onehot = (e_flat[:, None] == jnp.arange(N_EXPERTS, dtype=I32)[None, :]).astype(I32)
    csum = jnp.cumsum(onehot, axis=0)
    counts = csum[-1]
    rank = jnp.take_along_axis(csum, e_flat[:, None], axis=1)[:, 0] - 1
    padded = (counts + MOE_ROWS - 1) // MOE_ROWS * MOE_ROWS
    pad_end = jnp.cumsum(padded)
    pad_start = pad_end - padded
    dest = (pad_start[e_flat] + rank).astype(I32)
    row_token = jnp.zeros((r_total,), I32).at[dest].set(jnp.arange(n_pairs, dtype=I32) // TOP_K)
    blk_expert = jnp.minimum(jnp.searchsorted(pad_end, jnp.arange(nblk, dtype=I32) * MOE_ROWS, side="right"),
                             N_EXPERTS - 1).astype(I32)
    ys = moe_expert_ffn(u, row_token, blk_expert, w_gate, w_up, w_down, layer)
    dest2 = dest.reshape(t, TOP_K).T
    w_tok = wts.transpose(0, 2, 1).reshape(t, TOP_K)
    return moe_combine(ys, dest2, h, mod, w_tok)


def _pad_cols(w, n):
    return jnp.pad(w, ((0, 0), (0, n - w.shape[1])))


def fox_attention(h, mod, gain, w_in, b_f, q_gain, k_gain, cos, sin):
    b, s, _ = h.shape
    n_main = 3 * D_MODEL
    main, tail = norm_proj(h, mod, gain, w_in[:, :n_main].astype(BF16),
                           _pad_cols(w_in[:, n_main:], LANES).astype(BF16))
    gains = jnp.stack([_tile_gain(q_gain, Q_SCALE), _tile_gain(k_gain)])
    qk = head_prep(main, gains, (0, 1), cos, sin, rope=False)
    tk = min(FLASH_TILE, s)
    cum = fox_cumulative_gate(tail, b_f)
    key_bias = cum.reshape(b, N_PAIRS, 2, s // tk, tk).transpose(0, 1, 3, 2, 4)
    return flash_attention(qk, qk, main, q_off=0, k_off=N_PAIRS, v_off=2 * N_PAIRS, key_bias=key_bias, tk=tk)


def diff_attention(h, mod, gain, w_in, q_gain, k_gain, lambdas, sub_gain, layer_idx, cos, sin):
    b, s, _ = h.shape
    main = norm_proj(h, mod, gain, w_in.astype(BF16))
    gains = jnp.stack([_tile_gain(q_gain, Q_SCALE), _tile_gain(k_gain)])
    qk = head_prep(main, gains, (0, 1), cos, sin, rope=True)
    lam_init = 0.8 - 0.6 * math.exp(-0.3 * layer_idx)
    lam = jnp.pad(lambdas.astype(F32), ((0, 0), (0, LANES - HEAD_DIM)))
    return flash_attention(qk, qk, main, q_off=0, k_off=N_PAIRS, v_off=2 * N_PAIRS, fin="diff", lam=lam,
                           sub_gain=sub_gain.astype(F32).reshape(1, LANES), lam_init=lam_init)


def dilated_attention(h, mod, gain, w_in, q_gain, k_gain, cos, sin):
    b, s, _ = h.shape
    ng = len(DIL_PAIRS)
    main = norm_proj(h, mod, gain, w_in.astype(BF16), tn=3 * D_MODEL)
    gq, gk = _tile_gain(q_gain, Q_SCALE), _tile_gain(k_gain)
    qk = head_prep(main, jnp.stack([gq, gk] * ng), tuple(3 * g + j for g in range(ng) for j in range(2)),
                   cos, sin, rope=True)
    outs, lses = [], []
    for g, (window, dil) in enumerate(DIL_PAIRS):
        sd = s // dil
        if dil == 1:
            qk_g, v_g, q0, v0 = qk, main, 2 * g * N_PAIRS, (3 * g + 2) * N_PAIRS
        else:
            qk_g = qk[:, :, 2 * g * D_MODEL:(2 * g + 2) * D_MODEL].reshape(b, sd, dil * 2 * D_MODEL)
            v_g = main[:, :, (3 * g + 2) * D_MODEL:(3 * g + 3) * D_MODEL].reshape(b, sd, dil * D_MODEL)
            q0 = v0 = 0
        o, lse = band_attention(
            qk_g, qk_g, v_g,
            q_blk=lambda j, q0=q0: q0 + (j // N_PAIRS) * 2 * N_PAIRS + j % N_PAIRS,
            k_blk=lambda j, q0=q0: q0 + (j // N_PAIRS) * 2 * N_PAIRS + N_PAIRS + j % N_PAIRS,
            v_blk=lambda j, v0=v0: v0 + j,
            n_batch=b, n_inner=dil * N_PAIRS, out_cols=dil * D_MODEL, out_blk=lambda j: j,
            window=window // dil + 1)
        outs.append(o.reshape(b, s, D_MODEL))
        lses.append(lse.reshape(b, s, D_MODEL))
    return outs + lses


def nsa_attention(h, mod, gain, w_in, q_gain, k_gain, cmp_pos, cmp_w1, cmp_w2, cos, sin, cos_h, sin_h):
    b, s, _ = h.shape
    nblk = s // NSA_BLOCK
    n_main = D_MODEL + 6 * NSA_GROUPS * HEAD_DIM
    main, tail = norm_proj(h, mod, gain, w_in[:, :n_main].astype(BF16),
                           _pad_cols(w_in[:, n_main:], LANES).astype(BF16))
    gains = jnp.zeros((8, LANES), F32)
    gains = gains.at[0].set(jnp.tile(q_gain.astype(F32) * Q_SCALE, 2))
    gains = gains.at[1].set(jnp.tile(k_gain[1].astype(F32), 2)).at[2].set(jnp.tile(k_gain[2].astype(F32), 2))
    q, ks2, vs2, kw2, vw2 = nsa_prep(main, gains, cos, sin)

    def to_block_rows(col0):
        x = main[:, :, col0:col0 + NSA_GROUPS * HEAD_DIM].reshape(b, nblk, NSA_BLOCK, NSA_GROUPS, HEAD_DIM)
        return x.transpose(0, 3, 1, 2, 4).reshape(b * NSA_GROUPS * nblk, NSA_BLOCK * HEAD_DIM)

    cos_b = jnp.tile(cos_h[NSA_BLOCK - 1::NSA_BLOCK], (1, 2))
    sin_b = jnp.concatenate([-sin_h[NSA_BLOCK - 1::NSA_BLOCK], sin_h[NSA_BLOCK - 1::NSA_BLOCK]], axis=-1)
    kc = nsa_compress(to_block_rows(D_MODEL), cmp_pos[0].reshape(1, -1), cmp_w1[0].astype(BF16),
                      cmp_w2[0].astype(BF16), k_gain[0].astype(F32).reshape(1, HEAD_DIM), cos_b, sin_b, is_key=True)
    vc = nsa_compress(to_block_rows(D_MODEL + NSA_GROUPS * HEAD_DIM), cmp_pos[1].reshape(1, -1),
                      cmp_w1[1].astype(BF16), cmp_w2[1].astype(BF16),
                      k_gain[0].astype(F32).reshape(1, HEAD_DIM), cos_b, sin_b, is_key=False)
    kc = kc.reshape(b, NSA_GROUPS, nblk, HEAD_DIM)
    vc = vc.reshape(b, NSA_GROUPS, nblk, HEAD_DIM)
    pad_rows = ((0, 0), (0, 0), (0, HEAD_DIM - nblk), (0, 0))
    kc = jnp.pad(kc, pad_rows)
    vc = jnp.pad(vc, pad_rows)
    kc2 = jnp.tile(kc, (1, 1, 2, 2)).astype(BF16)
    vc2 = jnp.concatenate([jnp.tile(vc, (1, 1, 1, 2)), jnp.zeros_like(jnp.tile(vc, (1, 1, 1, 2)))],
                          axis=2).astype(BF16)
    o_cmp, q_aug = nsa_compressed_attention(q, kc2, vc2, n_sel=min(NSA_TOPN, nblk))
    common = dict(q_off=0, k_off=0, v_off=0, k_div=NSA_HPG // (2 * FLASH_PAIRS), mode="aug")
    o_sel = flash_attention(q_aug, ks2, vs2, **common)
    o_win = flash_attention(q_aug, kw2, vw2, window=NSA_WINDOW, **common)
    return [o_cmp, o_sel, o_win, tail]


def kernel(x, c, fox_w_in, fox_b_f, fox_q_gain, fox_k_gain, fox_w_out, nsa_w_in, nsa_q_gain, nsa_k_gain, nsa_cmp_pos, nsa_cmp_w1, nsa_cmp_w2, nsa_w_out, dil_w_in, dil_q_gain, dil_k_gain, dil_w_out, diff_w_in, diff_q_gain, diff_k_gain, diff_lambda, diff_sub_gain, diff_w_out, norm_gain, ada_w, ada_b, router_w, router_b, moe_w_gate, moe_w_up, moe_w_down):
    b, s, d = x.shape
    depth = norm_gain.shape[0]
    cos, sin, cos_h, sin_h = rope_lane_tables(s)
    mods = ada_modulation(c, ada_w, ada_b).reshape(depth, b, 6, d)
    router_wt = router_w.T.astype(F32)
    router_bb = jnp.broadcast_to(router_b.astype(F32)[:, None], (N_EXPERTS, LANES))
    h = x
    for i in range(depth):
        mod = mods[i]
        kind, j = i % 4, i // 4
        g1 = norm_gain[i, 0]
        if kind == 0:
            attn = [fox_attention(h, mod, g1, fox_w_in[j], fox_b_f[j], fox_q_gain[j], fox_k_gain[j], cos, sin)]
            w_out, mix = fox_w_out[j], "plain"
        elif kind == 1:
            attn = nsa_attention(h, mod, g1, nsa_w_in[j], nsa_q_gain[j], nsa_k_gain[j], nsa_cmp_pos[j],
                                 nsa_cmp_w1[j], nsa_cmp_w2[j], cos, sin, cos_h, sin_h)
            w_out, mix = nsa_w_out[j], "nsa"
        elif kind == 2:
            attn = dilated_attention(h, mod, g1, dil_w_in[j], dil_q_gain[j], dil_k_gain[j], cos, sin)
            w_out, mix = dil_w_out[j], "dil"
        else:
            attn = [diff_attention(h, mod, g1, diff_w_in[j], diff_q_gain[j], diff_k_gain[j], diff_lambda[j],
                                   diff_sub_gain[j], i, cos, sin)]
            w_out, mix = diff_w_out[j], "plain"
        h, u, eidx, wts = outproj_router(attn, h, mod, w_out.astype(BF16), norm_gain[i, 1], router_wt,
                                         router_bb, mix=mix)
        h = moe_layer(h, u, eidx, wts, mod, moe_w_gate, moe_w_up, moe_w_down, i)
    return h
```

```python
import functools
import math

import jax
import jax.numpy as jnp
from jax import lax
from jax.experimental import pallas as pl
from jax.experimental.pallas import tpu as pltpu

F32 = jnp.float32
BF16 = jnp.bfloat16
I32 = jnp.int32

D_MODEL = 1024
HEAD_DIM = 64
LANES = 128
N_HEADS = D_MODEL // HEAD_DIM
N_PAIRS = D_MODEL // LANES
ROPE_THETA = 10000.0
EPS = 1e-6
NEG_INF = -1e30
TINY = 1e-30
M_INIT = -1e29
LOG2E = 1.4426950408889634
Q_SCALE = HEAD_DIM ** -0.5 * LOG2E

NSA_GROUPS = 4
NSA_HPG = N_HEADS // NSA_GROUPS
NSA_BLOCK = 64
NSA_TOPN = 16
NSA_WINDOW = 512
DIL_PAIRS = ((128, 1), (512, 4), (2048, 16))

N_EXPERTS = 16
N_EXPERT_GROUPS = 4
EXPERTS_PER_GROUP = 4
TOP_K = 2
EXPERT_FF = 512
MOE_ROWS = 256
FLASH_TILE = 512
FLASH_PAIRS = 2

VMEM_LIMIT = 52 * 1024 * 1024


def _cp(*sem, vmem=VMEM_LIMIT):
    return pltpu.CompilerParams(dimension_semantics=sem, vmem_limit_bytes=vmem)


def _split3(a):
    hi = a.astype(BF16)
    r1 = a - hi.astype(F32)
    mid = r1.astype(BF16)
    lo = (r1 - mid.astype(F32)).astype(BF16)
    return hi, mid, lo


def _dot(a, b):
    return jnp.dot(a, b, preferred_element_type=F32)


def _dot_nt(a, b):
    return lax.dot_general(a, b, (((1,), (1,)), ((), ())), preferred_element_type=F32)


def _dot_f32(a, b, dot=_dot):
    ah, am, al = _split3(a)
    bh, bm, bl = _split3(b)
    return (dot(ah, bh) + (dot(ah, bm) + dot(am, bh))
            + (dot(ah, bl) + dot(al, bh) + dot(am, bm)))


def _dot_f32_exact_rhs(a, b_bf16):
    ah, am, al = _split3(a)
    return _dot(ah, b_bf16) + _dot(am, b_bf16) + _dot(al, b_bf16)


def _sigmoid(x):
    return 1.0 / (1.0 + jnp.exp(-x))


def _silu(x):
    return x * _sigmoid(x)


def _ada_kernel(c_ref, w_ref, b_ref, o_ref):
    c = c_ref[...]
    o_ref[0] = _dot_f32(_silu(c), w_ref[0]) + b_ref[0]


def ada_modulation(c, ada_w, ada_b):
    depth, d, n = ada_w.shape
    b = c.shape[0]
    tn = 1024
    return pl.pallas_call(
        _ada_kernel,
        grid=(depth, n // tn),
        in_specs=[pl.BlockSpec((b, d), lambda i, j: (0, 0)),
                  pl.BlockSpec((1, d, tn), lambda i, j: (i, 0, j)),
                  pl.BlockSpec((1, 1, tn), lambda i, j: (i, 0, j))],
        out_specs=pl.BlockSpec((1, b, tn), lambda i, j: (i, 0, j)),
        out_shape=jax.ShapeDtypeStruct((depth, b, n), F32),
        compiler_params=_cp("parallel", "parallel"),
        name="ada_modulation",
    )(c, ada_w, ada_b.reshape(depth, 1, n))


def _modulated_norm(x, gain, mod, shift_row, scale_row):
    ms = jnp.mean(x * x, axis=-1, keepdims=True)
    y = x * lax.rsqrt(ms + EPS) * gain
    return y * (1.0 + mod[scale_row:scale_row + 1, :]) + mod[shift_row:shift_row + 1, :]


def _norm_proj_kernel(h_ref, mod_ref, g_ref, w_ref, *rest, has_tail):
    u = _modulated_norm(h_ref[0], g_ref[...], mod_ref[0], 0, 1).astype(BF16)
    if has_tail:
        wt_ref, main_ref, tail_ref = rest
        tail_ref[0] = _dot(u, wt_ref[...])
    else:
        (main_ref,) = rest
    main_ref[0] = _dot(u, w_ref[...]).astype(main_ref.dtype)


def norm_proj(h, mod, gain, w_main, w_tail=None, *, tn=None, tm=512):
    b, s, d = h.shape
    n = w_main.shape[1]
    tn = tn or n
    tm = min(tm, s)
    in_specs = [pl.BlockSpec((1, tm, d), lambda j, bi, si: (bi, si, 0)),
                pl.BlockSpec((1, 6, d), lambda j, bi, si: (bi, 0, 0)),
                pl.BlockSpec((1, d), lambda j, bi, si: (0, 0)),
                pl.BlockSpec((d, tn), lambda j, bi, si: (0, j))]
    out_specs = [pl.BlockSpec((1, tm, tn), lambda j, bi, si: (bi, si, j))]
    out_shape = [jax.ShapeDtypeStruct((b, s, n), BF16)]
    args = [h, mod, gain.reshape(1, d), w_main]
    if w_tail is not None:
        in_specs.append(pl.BlockSpec((d, LANES), lambda j, bi, si: (0, 0)))
        out_specs.append(pl.BlockSpec((1, tm, LANES), lambda j, bi, si: (bi, si, 0)))
        out_shape.append(jax.ShapeDtypeStruct((b, s, LANES), F32))
        args.append(w_tail)
    outs = pl.pallas_call(
        functools.partial(_norm_proj_kernel, has_tail=w_tail is not None),
        grid=(n // tn, b, s // tm),
        in_specs=in_specs, out_specs=out_specs, out_shape=out_shape,
        compiler_params=_cp("parallel", "parallel", "parallel"),
        name="norm_proj",
    )(*args)
    return outs if w_tail is not None else outs[0]


def _lane_iota(rows):
    return lax.broadcasted_iota(I32, (rows, LANES), 1)


def _head_block_diag():
    r = lax.broadcasted_iota(I32, (LANES, LANES), 0) // HEAD_DIM
    c = lax.broadcasted_iota(I32, (LANES, LANES), 1) // HEAD_DIM
    return (r == c).astype(BF16)


def _head_norm_rope(x, gain, cos, sin, bd, rope):
    y = x * x
    hi = y.astype(BF16)
    lo = (y - hi.astype(F32)).astype(BF16)
    seg = _dot(hi, bd) + _dot(lo, bd)
    xn = x * lax.rsqrt(seg * (1.0 / HEAD_DIM) + EPS) * gain
    if rope:
        first_half = (_lane_iota(x.shape[0]) % HEAD_DIM) < HEAD_DIM // 2
        partner = jnp.where(first_half, pltpu.roll(xn, LANES - HEAD_DIM // 2, 1),
                            pltpu.roll(xn, HEAD_DIM // 2, 1))
        xn = xn * cos + partner * sin
    return xn


def _prep_kernel(cb_ref, x_ref, g_ref, cos_ref, sin_ref, o_ref, *, rope):
    del cb_ref
    bd = _head_block_diag()
    cos = cos_ref[...]
    sin = sin_ref[...]
    for c in range(x_ref.shape[2] // LANES):
        sl = slice(c * LANES, (c + 1) * LANES)
        x = x_ref[0, :, sl].astype(F32)
        o_ref[0, :, sl] = _head_norm_rope(x, g_ref[0, :, sl], cos, sin, bd, rope).astype(o_ref.dtype)


def head_prep(src, gains, col_blocks, cos, sin, *, rope, ts=512):
    b, s, _ = src.shape
    n = len(col_blocks)
    ts = min(ts, s)
    cb = jnp.asarray(col_blocks, I32)

    def x_map(bi, si, ci, cb_ref):
        return (bi, si, cb_ref[ci])

    return pl.pallas_call(
        functools.partial(_prep_kernel, rope=rope),
        grid_spec=pltpu.PrefetchScalarGridSpec(
            num_scalar_prefetch=1,
            grid=(b, s // ts, n),
            in_specs=[pl.BlockSpec((1, ts, D_MODEL), x_map),
                      pl.BlockSpec((1, 1, D_MODEL), lambda bi, si, ci, cb_ref: (ci, 0, 0)),
                      pl.BlockSpec((ts, LANES), lambda bi, si, ci, cb_ref: (si, 0)),
                      pl.BlockSpec((ts, LANES), lambda bi, si, ci, cb_ref: (si, 0))],
            out_specs=pl.BlockSpec((1, ts, D_MODEL), lambda bi, si, ci, cb_ref: (bi, si, ci))),
        out_shape=jax.ShapeDtypeStruct((b, s, n * D_MODEL), BF16),
        compiler_params=_cp("parallel", "parallel", "arbitrary"),
        name="head_prep",
    )(cb, src, gains, cos, sin)


def rope_lane_tables(s):
    inv = ROPE_THETA ** (-jnp.arange(0, HEAD_DIM, 2, dtype=F32) / HEAD_DIM)
    ang = jnp.arange(s).astype(F32)[:, None] * inv[None, :]
    cos, sin = jnp.cos(ang), jnp.sin(ang)
    return jnp.tile(cos, (1, 4)), jnp.tile(jnp.concatenate([-sin, sin], axis=-1), (1, 2)), cos, sin


def _tile_gain(g, scale=1.0):
    return jnp.tile(g.astype(F32) * scale, N_HEADS).reshape(1, D_MODEL)


def _fox_cum_kernel(f_ref, b_ref, o_ref, carry_ref):
    si = pl.program_id(1)
    ts = f_ref.shape[1]

    @pl.when(si == 0)
    def _():
        carry_ref[...] = jnp.zeros_like(carry_ref)

    z = f_ref[0] + b_ref[...]
    log_f = -(jnp.maximum(-z, 0.0) + jnp.log1p(jnp.exp(-jnp.abs(z))))
    r = lax.broadcasted_iota(I32, (ts, ts), 0)
    c = lax.broadcasted_iota(I32, (ts, ts), 1)
    upper = (r <= c).astype(BF16)
    cum = _dot_f32_exact_rhs(log_f.T, upper) + carry_ref[:, 0:1]
    o_ref[0] = cum[0:N_HEADS, :] * LOG2E
    carry_ref[...] = jnp.broadcast_to(cum[:, ts - 1:ts], carry_ref.shape)


def fox_cumulative_gate(tail, b_f, *, ts=256):
    b, s, _ = tail.shape
    ts = min(ts, s)
    bias = jnp.zeros((1, LANES), F32).at[0, :N_HEADS].set(b_f.astype(F32))
    return pl.pallas_call(
        _fox_cum_kernel,
        grid=(b, s // ts),
        in_specs=[pl.BlockSpec((1, ts, LANES), lambda bi, si: (bi, si, 0)),
                  pl.BlockSpec((1, LANES), lambda bi, si: (0, 0))],
        out_specs=pl.BlockSpec((1, N_HEADS, ts), lambda bi, si: (bi, 0, si)),
        out_shape=jax.ShapeDtypeStruct((b, N_HEADS, s), F32),
        scratch_shapes=[pltpu.VMEM((LANES, LANES), F32)],
        compiler_params=_cp("parallel", "arbitrary"),
        name="fox_cumulative_gate",
    )(tail, bias)


def _flash_kernel(*refs, tq, tk, window, mode, fin, has_bias, pairs, lam_init):
    it = iter(refs)
    q_ref, k_ref, v_ref = next(it), next(it), next(it)
    kb_ref = next(it) if has_bias else None
    if fin == "diff":
        lam_ref, sg_ref = next(it), next(it)
    o_ref = next(it)
    m_sc, l_sc, acc_sc = next(it), next(it), next(it)

    n_heads = 2 * pairs
    q_start = pl.program_id(2) * tq
    lane = _lane_iota(tq)
    low_half = lane < HEAD_DIM
    qh = []
    for h in range(n_heads):
        if mode == "pair":
            q = q_ref[0, :, (h // 2) * LANES:(h // 2 + 1) * LANES]
            zero = jnp.zeros_like(q)
            qh.append(jnp.where(low_half, q, zero) if h % 2 == 0 else jnp.where(low_half, zero, q))
        else:
            qh.append(q_ref[0, :, h * LANES:(h + 1) * LANES])

    m_sc[...] = jnp.full(m_sc.shape, M_INIT, F32)
    l_sc[...] = jnp.zeros(l_sc.shape, F32)
    acc_sc[...] = jnp.zeros(acc_sc.shape, F32)
    row = q_start + lax.broadcasted_iota(I32, (tq, LANES), 0)
    n_chunk = tk // LANES

    def step(kv, masked):
        ks = pl.multiple_of(kv * tk, tk)
        kblk = k_ref[0, pl.ds(ks, tk), :]
        vblk = v_ref[0, pl.ds(ks, tk), :]
        if masked:
            masks = []
            for c in range(n_chunk):
                col = ks + c * LANES + lane
                mk = col <= row
                if window:
                    mk = mk & ((row - col) < window)
                masks.append(mk)
        for h in range(n_heads):
            if mode == "pair":
                kk = kblk[:, (h // 2) * LANES:(h // 2 + 1) * LANES]
                vv = vblk[:, (h // 2) * LANES:(h // 2 + 1) * LANES]
            else:
                kk = kblk[:, (h % 2) * LANES:(h % 2 + 1) * LANES]
                vv = vblk
            s = _dot_nt(qh[h], kk)
            if has_bias:
                s = s - kb_ref[0, h // 2, kv][h % 2:h % 2 + 1, :]
            chunks = [s[:, c * LANES:(c + 1) * LANES] for c in range(n_chunk)]
            if masked:
                chunks = [jnp.where(mk, ch, -jnp.inf) for mk, ch in zip(masks, chunks)]
            mb = functools.reduce(jnp.maximum, chunks)
            m_old = m_sc[h]
            m_new = jnp.maximum(m_old, jnp.broadcast_to(jnp.max(mb, axis=-1, keepdims=True), (tq, LANES)))
            alpha = jnp.exp2(m_old - m_new)
            ps = [jnp.exp2(ch - m_new) for ch in chunks]
            l_sc[h] = alpha * l_sc[h] + functools.reduce(jnp.add, ps)
            p = ps[0] if n_chunk == 1 else jnp.concatenate(ps, axis=1)
            acc_sc[h] = alpha * acc_sc[h] + _dot(p.astype(BF16), vv)
            m_sc[h] = m_new

    def loop(lo, hi, masked):
        def body(kv, carry):
            step(kv, masked)
            return carry
        lax.fori_loop(lo, hi, body, 0)

    last_blk = (q_start + (tq - 1)) // tk
    full_hi = (q_start + 1) // tk
    if window:
        first_blk = jnp.maximum(q_start - (window - 1), 0) // tk
        full_lo = jnp.maximum(q_start + (tq - 1) - window + tk, 0) // tk
        full_lo = jnp.maximum(jnp.minimum(full_lo, full_hi), first_blk)
        loop(first_blk, full_lo, True)
    else:
        full_lo = 0
    loop(full_lo, full_hi, False)
    loop(jnp.maximum(full_hi, full_lo), last_blk + 1, True)

    if fin == "diff":
        lam_rows = lam_ref[...]
        lam = (jnp.exp(jnp.sum(lam_rows[0:1] * lam_rows[1:2], axis=-1, keepdims=True))
               - jnp.exp(jnp.sum(lam_rows[2:3] * lam_rows[3:4], axis=-1, keepdims=True)) + lam_init)
    for pi in range(pairs):
        l0 = jnp.maximum(jnp.sum(l_sc[2 * pi], axis=-1, keepdims=True), TINY)
        l1 = jnp.maximum(jnp.sum(l_sc[2 * pi + 1], axis=-1, keepdims=True), TINY)
        o0 = acc_sc[2 * pi] * (1.0 / l0)
        o1 = acc_sc[2 * pi + 1] * (1.0 / l1)
        if fin == "select":
            o = jnp.where(low_half, o0, o1)
        else:
            o = o0 - lam * o1
            ms = jnp.mean(o * o, axis=-1, keepdims=True)
            o = o * lax.rsqrt(ms + EPS) * sg_ref[...] * (1.0 - lam_init)
        o_ref[0, :, pi * LANES:(pi + 1) * LANES] = o.astype(o_ref.dtype)


def flash_attention(q, k, v, *, q_off, k_off, v_off, k_div=1, mode="pair", fin="select", window=0,
                    key_bias=None, lam=None, sub_gain=None, lam_init=0.0, pairs=FLASH_PAIRS,
                    tq=FLASH_TILE, tk=FLASH_TILE):
    nb, s, _ = q.shape
    tq, tk = min(tq, s), min(tk, s)
    n_inner = N_PAIRS // pairs
    wo = pairs * LANES
    if mode == "pair":
        wq = wk = wv = wo
    else:
        wq, wk, wv = 2 * wo, 2 * LANES, LANES
    in_specs = [pl.BlockSpec((1, tq, wq), lambda b, j, i: (b, i, q_off // (wq // LANES) + j)),
                pl.BlockSpec((1, s, wk), lambda b, j, i: (b, 0, k_off // (wk // LANES) + j // k_div)),
                pl.BlockSpec((1, s, wv), lambda b, j, i: (b, 0, v_off // (wv // LANES) + j // k_div))]
    args = [q, k, v]
    if key_bias is not None:
        in_specs.append(pl.BlockSpec((1, pairs, s // tk, 2, tk), lambda b, j, i: (b, j, 0, 0, 0)))
        args.append(key_bias)
    if fin == "diff":
        in_specs += [pl.BlockSpec((4, LANES), lambda b, j, i: (0, 0)),
                     pl.BlockSpec((1, LANES), lambda b, j, i: (0, 0))]
        args += [lam, sub_gain]
    return pl.pallas_call(
        functools.partial(_flash_kernel, tq=tq, tk=tk, window=window, mode=mode, fin=fin,
                          has_bias=key_bias is not None, pairs=pairs, lam_init=lam_init),
        grid=(nb, n_inner, s // tq),
        in_specs=in_specs,
        out_specs=pl.BlockSpec((1, tq, wo), lambda b, j, i: (b, i, j)),
        out_shape=jax.ShapeDtypeStruct((nb, s, D_MODEL), BF16),
        scratch_shapes=[pltpu.VMEM((2 * pairs, tq, LANES), F32)] * 3,
        compiler_params=_cp("parallel", "parallel", "arbitrary"),
        name="flash_" + mode + "_" + fin,
    )(*args)


DIL_SUB = LANES
DIL_UNROLL = 4


def _rows(start, size, stride):
    return pl.ds(start, size) if stride == 1 else pl.ds(start, size, stride=stride)


def _dilated_kernel(*refs, seq):
    q_refs, k_refs, v_refs = refs[0:3], refs[3:6], refs[6:9]
    o_ref = refs[9]
    qf, kf, vf, m_st, l_st, acc_st = refs[10:16]
    sub = min(DIL_SUB, seq // DIL_PAIRS[-1][1])
    lane = _lane_iota(sub)
    low_half = lane < HEAD_DIM
    for g, (window, d) in enumerate(DIL_PAIRS):
        sd = seq // d
        span = min(2 * sub, sd)
        n_res = sd // sub
        win = window // d + 1
        assert win <= span - sub + 1 or span == sd
        qf[...] = q_refs[g][0].astype(F32)
        kf[...] = k_refs[g][0].astype(F32)
        vf[...] = v_refs[g][0].astype(F32)
        low_half_kv = _lane_iota(span) < HEAD_DIM
        ones = jnp.ones((span, LANES), BF16)

        def chain(t_idx, g=g, d=d, sd=sd, span=span, n_res=n_res, win=win, low_half_kv=low_half_kv, ones=ones):
            r = t_idx // n_res
            i0 = (t_idx % n_res) * sub
            ks = jnp.minimum(jnp.maximum(i0 - sub, 0), sd - span)
            q_rows = _rows(i0 * d + r, sub, d)
            kv_rows = _rows(ks * d + r, span, d)
            q = qf[q_rows, :].astype(BF16)
            kblk = kf[kv_rows, :].astype(BF16)
            vblk = vf[kv_rows, :].astype(BF16)
            zero = jnp.zeros_like(q)
            row = i0 + lax.broadcasted_iota(I32, (sub, LANES), 0)
            masks = []
            for c in range(span // LANES):
                col = ks + c * LANES + lane
                masks.append((col <= row) & ((row - col) < win))
            ms, accs = [], []
            for h in range(2):
                own = low_half if h == 0 else jnp.logical_not(low_half)
                own_kv = low_half_kv if h == 0 else jnp.logical_not(low_half_kv)
                s = _dot_nt(jnp.where(own, q, zero), kblk)
                chunks = [jnp.where(mk, s[:, c * LANES:(c + 1) * LANES], -jnp.inf) for c, mk in enumerate(masks)]
                m = jnp.broadcast_to(jnp.max(functools.reduce(jnp.maximum, chunks), axis=-1, keepdims=True),
                                     (sub, LANES))
                ps = [jnp.exp2(ch - m) for ch in chunks]
                p = ps[0] if len(ps) == 1 else jnp.concatenate(ps, axis=1)
                accs.append(_dot(p.astype(BF16), jnp.where(own_kv, vblk, ones)))
                ms.append(m)
            m_c = jnp.where(low_half, ms[0], ms[1])
            acc_c = jnp.where(low_half, accs[0], accs[1])
            l_c = jnp.where(low_half, pltpu.roll(accs[0], HEAD_DIM, 1), pltpu.roll(accs[1], HEAD_DIM, 1))
            if g == 0:
                m_st[q_rows, :] = m_c
                l_st[q_rows, :] = l_c
                acc_st[q_rows, :] = acc_c
            else:
                m_old = m_st[q_rows, :]
                m_new = jnp.maximum(m_old, m_c)
                a, bb = jnp.exp2(m_old - m_new), jnp.exp2(m_c - m_new)
                m_st[q_rows, :] = m_new
                l_st[q_rows, :] = a * l_st[q_rows, :] + bb * l_c
                acc_st[q_rows, :] = a * acc_st[q_rows, :] + bb * acc_c

        n_chain = seq // sub
        unroll = min(DIL_UNROLL, n_chain)

        def body(it, carry, chain=chain, unroll=unroll):
            for u in range(unroll):
                chain(it * unroll + u)
            return carry
        lax.fori_loop(0, n_chain // unroll, body, 0)
    o_ref[0] = (acc_st[...] * (1.0 / jnp.maximum(l_st[...], TINY))).astype(o_ref.dtype)


def dilated_groups_attention(qk, main):
    b, s, _ = qk.shape
    ng = len(DIL_PAIRS)
    blk = lambda off: pl.BlockSpec((1, s, LANES), lambda bi, j, off=off: (bi, 0, off + j))
    in_specs = ([blk(2 * g * N_PAIRS) for g in range(ng)] + [blk((2 * g + 1) * N_PAIRS) for g in range(ng)]
                + [blk((3 * g + 2) * N_PAIRS) for g in range(ng)])
    return pl.pallas_call(
        functools.partial(_dilated_kernel, seq=s),
        grid=(b, N_PAIRS),
        in_specs=in_specs,
        out_specs=pl.BlockSpec((1, s, LANES), lambda bi, j: (bi, 0, j)),
        out_shape=jax.ShapeDtypeStruct((b, s, D_MODEL), BF16),
        scratch_shapes=[pltpu.VMEM((s, LANES), F32)] * 6,
        compiler_params=_cp("parallel", "parallel"),
        name="dilated_groups_attention",
    )(*([qk] * (2 * ng) + [main] * ng))


def _nsa_prep_kernel(q_ref, ks_ref, vs_ref, kw_ref, vw_ref, g_ref, cos_ref, sin_ref,
                     qo_ref, ks2_ref, vs2_ref, kw2_ref, vw2_ref):
    ts = q_ref.shape[1]
    bd = _head_block_diag()
    cos, sin = cos_ref[...], sin_ref[...]
    lane = _lane_iota(ts)
    low_half = lane < HEAD_DIM
    for c in range(N_PAIRS):
        sl = slice(c * LANES, (c + 1) * LANES)
        qo_ref[0, :, sl] = _head_norm_rope(q_ref[0, :, sl].astype(F32), g_ref[0:1, :], cos, sin, bd,
                                           True).astype(qo_ref.dtype)
    t = pl.program_id(1) * ts + lax.broadcasted_iota(I32, (ts, LANES), 0)
    blk_onehot = ((t // NSA_BLOCK) == (lane % HEAD_DIM)).astype(F32)
    zeros = jnp.zeros((ts, LANES), F32)

    def spread(x, fill, out_ref, c):
        xr = pltpu.roll(x, HEAD_DIM, 1)
        base = 2 * c * 2 * LANES
        out_ref[0, :, base:base + LANES] = jnp.where(low_half, x, fill).astype(out_ref.dtype)
        out_ref[0, :, base + LANES:base + 2 * LANES] = jnp.where(low_half, fill, xr).astype(out_ref.dtype)
        out_ref[0, :, base + 2 * LANES:base + 3 * LANES] = jnp.where(low_half, xr, fill).astype(out_ref.dtype)
        out_ref[0, :, base + 3 * LANES:base + 4 * LANES] = jnp.where(low_half, fill, x).astype(out_ref.dtype)

    def dup(x, out_ref, c):
        xr = pltpu.roll(x, HEAD_DIM, 1)
        out_ref[0, :, 2 * c * LANES:(2 * c + 1) * LANES] = jnp.where(low_half, x, xr).astype(out_ref.dtype)
        out_ref[0, :, (2 * c + 1) * LANES:(2 * c + 2) * LANES] = jnp.where(low_half, xr, x).astype(out_ref.dtype)

    for c in range(NSA_GROUPS // 2):
        sl = slice(c * LANES, (c + 1) * LANES)
        ks = _head_norm_rope(ks_ref[0, :, sl].astype(F32), g_ref[1:2, :], cos, sin, bd, True)
        kw = _head_norm_rope(kw_ref[0, :, sl].astype(F32), g_ref[2:3, :], cos, sin, bd, True)
        spread(ks, blk_onehot, ks2_ref, c)
        spread(kw, zeros, kw2_ref, c)
        dup(vs_ref[0, :, sl].astype(F32), vs2_ref, c)
        dup(vw_ref[0, :, sl].astype(F32), vw2_ref, c)


def nsa_prep(main, gains, cos, sin, *, ts=512):
    b, s, _ = main.shape
    ts = min(ts, s)
    gw = NSA_GROUPS * HEAD_DIM

    def kv_spec(i):
        return pl.BlockSpec((1, ts, gw), lambda bi, si: (bi, si, i))

    return pl.pallas_call(
        _nsa_prep_kernel,
        grid=(b, s // ts),
        in_specs=[pl.BlockSpec((1, ts, D_MODEL), lambda bi, si: (bi, si, 0)),
                  kv_spec(6), kv_spec(7), kv_spec(8), kv_spec(9),
                  pl.BlockSpec((8, LANES), lambda bi, si: (0, 0)),
                  pl.BlockSpec((ts, LANES), lambda bi, si: (si, 0)),
                  pl.BlockSpec((ts, LANES), lambda bi, si: (si, 0))],
        out_specs=[pl.BlockSpec((1, ts, D_MODEL), lambda bi, si: (bi, si, 0)),
                   pl.BlockSpec((1, ts, NSA_GROUPS * 2 * LANES), lambda bi, si: (bi, si, 0)),
                   pl.BlockSpec((1, ts, NSA_GROUPS * LANES), lambda bi, si: (bi, si, 0)),
                   pl.BlockSpec((1, ts, NSA_GROUPS * 2 * LANES), lambda bi, si: (bi, si, 0)),
                   pl.BlockSpec((1, ts, NSA_GROUPS * LANES), lambda bi, si: (bi, si, 0))],
        out_shape=[jax.ShapeDtypeStruct((b, s, D_MODEL), BF16),
                   jax.ShapeDtypeStruct((b, s, NSA_GROUPS * 2 * LANES), BF16),
                   jax.ShapeDtypeStruct((b, s, NSA_GROUPS * LANES), BF16),
                   jax.ShapeDtypeStruct((b, s, NSA_GROUPS * 2 * LANES), BF16),
                   jax.ShapeDtypeStruct((b, s, NSA_GROUPS * LANES), BF16)],
        compiler_params=_cp("parallel", "parallel"),
        name="nsa_prep",
    )(main, main, main, main, main, gains, cos, sin)


def _nsa_compress_kernel(x_ref, pos_ref, w1_ref, w2_ref, g_ref, cos_ref, sin_ref, o_ref, *, is_key):
    x = (x_ref[...].astype(F32) + pos_ref[...]).astype(BF16)
    hid = _silu(_dot(x, w1_ref[...]))
    y = _dot(hid.astype(BF16), w2_ref[...])
    if is_key:
        ms = jnp.mean(y * y, axis=-1, keepdims=True)
        y = y * lax.rsqrt(ms + EPS) * g_ref[...]
        r = lax.broadcasted_iota(I32, (HEAD_DIM, HEAD_DIM), 0)
        c = lax.broadcasted_iota(I32, (HEAD_DIM, HEAD_DIM), 1)
        swap = (((r + HEAD_DIM // 2) % HEAD_DIM) == c).astype(BF16)
        y = y * cos_ref[...] + _dot_f32_exact_rhs(y, swap) * sin_ref[...]
    o_ref[...] = y


def nsa_compress(x, pos, w1, w2, gain, cos_blk, sin_blk, *, is_key):
    rows, k = x.shape
    nb = cos_blk.shape[0]
    hid = w1.shape[1]
    return pl.pallas_call(
        functools.partial(_nsa_compress_kernel, is_key=is_key),
        grid=(rows // nb,),
        in_specs=[pl.BlockSpec((nb, k), lambda i: (i, 0)),
                  pl.BlockSpec((1, k), lambda i: (0, 0)),
                  pl.BlockSpec((k, hid), lambda i: (0, 0)),
                  pl.BlockSpec((hid, HEAD_DIM), lambda i: (0, 0)),
                  pl.BlockSpec((1, HEAD_DIM), lambda i: (0, 0)),
                  pl.BlockSpec((nb, HEAD_DIM), lambda i: (0, 0)),
                  pl.BlockSpec((nb, HEAD_DIM), lambda i: (0, 0))],
        out_specs=pl.BlockSpec((nb, HEAD_DIM), lambda i: (i, 0)),
        out_shape=jax.ShapeDtypeStruct((rows, HEAD_DIM), F32),
        compiler_params=_cp("parallel"),
        name="nsa_compress",
    )(x, pos, w1, w2, gain, cos_blk, sin_blk)


def _nsa_cmp_kernel(q_ref, kc_ref, vc_ref, o_ref, qa_ref, *, n_sel):
    tq = q_ref.shape[1]
    lane = _lane_iota(tq)
    low_half = lane < HEAD_DIM
    blk = lane % HEAD_DIM
    t = pl.program_id(2) * tq + lax.broadcasted_iota(I32, (tq, LANES), 0)
    cmask = (blk + 1) * NSA_BLOCK <= t + 1
    kc = kc_ref[0, 0]
    vc = vc_ref[0, 0]
    imp = jnp.zeros((tq, LANES), F32)
    qblk = [q_ref[0, :, 0:LANES], q_ref[0, :, LANES:2 * LANES]]
    zero = jnp.zeros_like(qblk[0])
    outs = [None, None]
    for p in range(NSA_HPG):
        in_low = (p % 2) == 0
        qb = qblk[p // 2]
        qm = jnp.where(low_half, qb, zero) if in_low else jnp.where(low_half, zero, qb)
        s = jnp.where(cmask, _dot_nt(qm, kc), NEG_INF)
        m = jnp.max(s, axis=-1, keepdims=True)
        e = jnp.where(cmask, jnp.exp2(s - m), 0.0)
        den = jnp.maximum(0.5 * jnp.sum(e, axis=-1, keepdims=True), TINY)
        pc = e / den
        imp = imp + pc
        o = _dot(pc.astype(BF16), vc)
        prev = outs[p // 2]
        outs[p // 2] = o if prev is None else jnp.where(low_half, prev, o)
    o_ref[0, :, 0:LANES] = outs[0].astype(o_ref.dtype)
    o_ref[0, :, LANES:2 * LANES] = outs[1].astype(o_ref.dtype)

    cur = t // NSA_BLOCK
    forced = (blk == 0) | (blk == cur) | (blk == cur - 1)
    x = jnp.where(blk > cur, -1.0, jnp.where(forced, NSA_HPG + 1.0, imp))
    blk_f = blk.astype(F32)
    selected = jnp.zeros((tq, LANES), jnp.bool_)
    for _ in range(n_sel):
        mx = jnp.max(x, axis=-1, keepdims=True)
        first = jnp.min(jnp.where(x == mx, blk_f, float(LANES)), axis=-1, keepdims=True)
        hit = blk_f == first
        selected = selected | hit
        x = jnp.where(hit, -2.0, x)
    sel_bias = jnp.where(selected, 0.0, NEG_INF).astype(qa_ref.dtype)
    for p in range(NSA_HPG):
        qb = qblk[p // 2]
        qa = jnp.where(low_half, qb, sel_bias) if p % 2 == 0 else jnp.where(low_half, sel_bias, qb)
        qa_ref[0, :, p * LANES:(p + 1) * LANES] = qa


def nsa_compressed_attention(q, kc2, vc2, *, n_sel, tq=1024):
    b, s, _ = q.shape
    tq = min(tq, s)
    gq = NSA_HPG * HEAD_DIM
    return pl.pallas_call(
        functools.partial(_nsa_cmp_kernel, n_sel=n_sel),
        grid=(b, NSA_GROUPS, s // tq),
        in_specs=[pl.BlockSpec((1, tq, gq), lambda bi, g, i: (bi, i, g)),
                  pl.BlockSpec((1, 1, LANES, LANES), lambda bi, g, i: (bi, g, 0, 0)),
                  pl.BlockSpec((1, 1, LANES, LANES), lambda bi, g, i: (bi, g, 0, 0))],
        out_specs=[pl.BlockSpec((1, tq, gq), lambda bi, g, i: (bi, i, g)),
                   pl.BlockSpec((1, tq, 2 * gq), lambda bi, g, i: (bi, i, g))],
        out_shape=[jax.ShapeDtypeStruct((b, s, D_MODEL), BF16),
                   jax.ShapeDtypeStruct((b, s, 2 * D_MODEL), BF16)],
        compiler_params=_cp("parallel", "parallel", "parallel"),
        name="nsa_compressed_attention",
    )(q, kc2, vc2)


def _route(logits_t, rb):
    scores = _sigmoid(logits_t)
    sel = scores + rb
    rows = [sel[i:i + 1, :] for i in range(N_EXPERTS)]
    srows = [scores[i:i + 1, :] for i in range(N_EXPERTS)]
    best = grp = None
    for g in range(N_EXPERT_GROUPS):
        a, b, c, d = rows[4 * g:4 * g + 4]
        hi1, lo1, hi2, lo2 = jnp.maximum(a, b), jnp.minimum(a, b), jnp.maximum(c, d), jnp.minimum(c, d)
        gs = jnp.maximum(hi1, hi2) + jnp.maximum(jnp.minimum(hi1, hi2), jnp.maximum(lo1, lo2))
        if g == 0:
            best, grp = gs, jnp.zeros(gs.shape, I32)
        else:
            better = gs > best
            grp = jnp.where(better, g, grp)
            best = jnp.where(better, gs, best)

    def pick(vals, i):
        out = vals[i]
        for g in range(1, N_EXPERT_GROUPS):
            out = jnp.where(grp == g, vals[4 * g + i], out)
        return out

    v = [pick(rows, i) for i in range(EXPERTS_PER_GROUP)]
    w = [pick(srows, i) for i in range(EXPERTS_PER_GROUP)]
    l1, b1, w1 = jnp.zeros(grp.shape, I32), v[0], w[0]
    for i in range(1, EXPERTS_PER_GROUP):
        better = v[i] > b1
        l1 = jnp.where(better, i, l1)
        b1 = jnp.where(better, v[i], b1)
        w1 = jnp.where(better, w[i], w1)
    have = jnp.zeros(grp.shape, jnp.bool_)
    l2, b2, w2 = jnp.zeros(grp.shape, I32), jnp.zeros_like(b1), jnp.zeros_like(w1)
    for i in range(EXPERTS_PER_GROUP):
        valid = l1 != i
        better = valid & (jnp.logical_not(have) | (v[i] > b2))
        l2 = jnp.where(better, i, l2)
        b2 = jnp.where(better, v[i], b2)
        w2 = jnp.where(better, w[i], w2)
        have = have | valid
    wsum = w1 + w2
    return (grp * EXPERTS_PER_GROUP + l1, grp * EXPERTS_PER_GROUP + l2), (w1 / wsum, w2 / wsum)


def _outproj_router_kernel(*refs, mix):
    it = iter(refs)
    if mix == "plain":
        x = next(it)[0]
    elif mix == "nsa":
        o_refs = [next(it), next(it), next(it)]
        gl = _sigmoid(next(it)[0])
        g_hi = gl.astype(BF16)
        g_lo = (gl - g_hi.astype(F32)).astype(BF16)
        r = lax.broadcasted_iota(I32, (LANES, D_MODEL), 0)
        c = lax.broadcasted_iota(I32, (LANES, D_MODEL), 1) // HEAD_DIM
        x = None
        for i in range(3):
            expand = (r == c + i * N_HEADS).astype(BF16)
            term = (_dot(g_hi, expand) + _dot(g_lo, expand)) * o_refs[i][0].astype(F32)
            x = term if x is None else x + term
        x = x.astype(BF16)
    else:
        raise ValueError(mix)
    h_ref, mod_ref, w_ref, g2_ref, rw_ref, rb_ref = (next(it) for _ in range(6))
    ho_ref, u_ref, e_ref, wt_ref = (next(it) for _ in range(4))
    mod = mod_ref[0]
    h_new = h_ref[0] + mod[2:3, :] * _dot(x, w_ref[...])
    ho_ref[0] = h_new
    u = _modulated_norm(h_new, g2_ref[...], mod, 3, 4)
    _store_token_tiles(u_ref, u)
    logits_t = _dot_f32(rw_ref[...], u, dot=_dot_nt)
    eidx, wts = _route(logits_t, rb_ref[:, 0:1])
    for k in range(TOP_K):
        e_ref[0, k:k + 1, :] = eidx[k]
        wt_ref[0, k:k + 1, :] = wts[k]


def outproj_router(attn_inputs, h, mod, w_out, gain2, router_wt, router_b, *, mix, tm=512):
    b, s, d = h.shape
    tm = min(tm, s)
    row_spec = pl.BlockSpec((1, tm, d), lambda bi, si: (bi, si, 0))
    in_specs, args = [], []
    for a in attn_inputs:
        in_specs.append(pl.BlockSpec((1, tm, a.shape[2]), lambda bi, si: (bi, si, 0)))
        args.append(a)
    in_specs += [row_spec,
                 pl.BlockSpec((1, 6, d), lambda bi, si: (bi, 0, 0)),
                 pl.BlockSpec((d, d), lambda bi, si: (0, 0)),
                 pl.BlockSpec((1, d), lambda bi, si: (0, 0)),
                 pl.BlockSpec((N_EXPERTS, d), lambda bi, si: (0, 0)),
                 pl.BlockSpec((N_EXPERTS, LANES), lambda bi, si: (0, 0))]
    args += [h, mod, w_out, gain2.reshape(1, d), router_wt, router_b]
    return pl.pallas_call(
        functools.partial(_outproj_router_kernel, mix=mix),
        grid=(b, s // tm),
        in_specs=in_specs,
        out_specs=[row_spec, pl.BlockSpec((tm * ROW_TILE, LANES), lambda bi, si: (bi * (s // tm) + si, 0)),
                   pl.BlockSpec((1, TOP_K, tm), lambda bi, si: (bi, 0, si)),
                   pl.BlockSpec((1, TOP_K, tm), lambda bi, si: (bi, 0, si))],
        out_shape=[jax.ShapeDtypeStruct((b, s, d), F32), jax.ShapeDtypeStruct((b * s * ROW_TILE, LANES), F32),
                   jax.ShapeDtypeStruct((b, TOP_K, s), I32), jax.ShapeDtypeStruct((b, TOP_K, s), F32)],
        compiler_params=_cp("parallel", "parallel"),
        name="outproj_router_" + mix,
    )(*args)


ROW_TILE = D_MODEL // LANES
FFN_PIECES = 6
FFN_SLOTS = 3


def _store_token_tiles(ref, x):
    rows = x.shape[0]
    for c in range(ROW_TILE):
        ref[pl.ds(c, rows, stride=ROW_TILE), :] = x[:, c * LANES:(c + 1) * LANES]


def _load_token_tiles(ref, rows):
    return jnp.concatenate([ref[pl.ds(c, rows, stride=ROW_TILE), :] for c in range(ROW_TILE)], axis=1)


def _tile_copy(src_hbm, index, dst, r, sem):
    start = pl.multiple_of(index * ROW_TILE, ROW_TILE)
    return pltpu.make_async_copy(src_hbm.at[pl.ds(start, ROW_TILE), :], dst.at[pl.ds(r * ROW_TILE, ROW_TILE), :], sem)


def _gather_tiles_loop(src_hbm, index_of, dst, n_rows, sem):
    def one(r, carry):
        _tile_copy(src_hbm, index_of(r), dst, r, sem).start()
        return carry
    lax.fori_loop(0, n_rows, one, 0)


def _wait_tiles(src_hbm, dst, sem):
    pltpu.make_async_copy(src_hbm.at[pl.ds(0, dst.shape[0]), :], dst, sem).wait()


def _moe_ffn_kernel(be_ref, tok_cur, tok_next, tok_ahead, u_hbm, wg_ref, wu_ref, wd_ref, y_ref, xbuf, wg_sc, wu_sc,
                    wd_sc, sem):
    i = pl.program_id(0)
    n = pl.num_programs(0)
    rows = MOE_ROWS
    slot = i % FFN_SLOTS
    nxt = (i + 2) % FFN_SLOTS

    @pl.when(i == 0)
    def _():
        _gather_tiles_loop(u_hbm, lambda r: tok_cur[r], xbuf.at[0], rows, sem.at[0])
        _gather_tiles_loop(u_hbm, lambda r: tok_next[r], xbuf.at[1], rows, sem.at[1])

    @pl.when((i == 0) | (be_ref[i] != be_ref[jnp.maximum(i - 1, 0)]))
    def _():
        wg_sc[...] = wg_ref[0, 0].astype(BF16)
        wu_sc[...] = wu_ref[0, 0].astype(BF16)
        wd_sc[...] = wd_ref[0, 0].astype(BF16)

    _wait_tiles(u_hbm, xbuf.at[slot], sem.at[slot])

    per = rows // FFN_PIECES + 1
    issued = [0]

    def issue_some():
        lo, hi = issued[0], min(issued[0] + per, rows)
        for r in range(lo, hi):
            _tile_copy(u_hbm, tok_ahead[r], xbuf.at[nxt], r, sem.at[nxt]).start(priority=r % 2)
        issued[0] = hi

    x = _load_token_tiles(xbuf.at[slot], rows).astype(BF16)
    g = _dot(x, wg_sc[...])
    issue_some()
    u = _dot(x, wu_sc[...])
    issue_some()
    hid = (_silu(g) * u).astype(BF16)
    d = wd_sc.shape[1]
    n_out = FFN_PIECES - 2
    w = d // n_out
    for c in range(n_out):
        y = _dot(hid, wd_sc[:, c * w:(c + 1) * w])
        for j in range(w // LANES):
            y_ref[pl.ds(c * (w // LANES) + j, rows, stride=ROW_TILE), :] = y[:, j * LANES:(j + 1) * LANES]
        issue_some()
    assert issued[0] == rows

    @pl.when(i == n - 1)
    def _():
        for ahead in (1, 2):
            s = (i + ahead) % FFN_SLOTS
            _wait_tiles(u_hbm, xbuf.at[s], sem.at[s])


def moe_expert_ffn(u_tiles, row_token, blk_expert, w_gate, w_up, w_down, layer):
    r_total = row_token.shape[0]
    nblk = r_total // MOE_ROWS
    d, ff = w_gate.shape[2], w_gate.shape[3]
    smem_blk = functools.partial(pl.BlockSpec, (MOE_ROWS,), memory_space=pltpu.SMEM)
    return pl.pallas_call(
        _moe_ffn_kernel,
        grid_spec=pltpu.PrefetchScalarGridSpec(
            num_scalar_prefetch=1,
            grid=(nblk,),
            in_specs=[smem_blk(lambda i, be: (i,)),
                      smem_blk(lambda i, be: (jnp.minimum(i + 1, nblk - 1),)),
                      smem_blk(lambda i, be: (jnp.minimum(i + 2, nblk - 1),)),
                      pl.BlockSpec(memory_space=pl.ANY),
                      pl.BlockSpec((1, 1, d, ff), lambda i, be: (layer, be[i], 0, 0)),
                      pl.BlockSpec((1, 1, d, ff), lambda i, be: (layer, be[i], 0, 0)),
                      pl.BlockSpec((1, 1, ff, d), lambda i, be: (layer, be[i], 0, 0))],
            out_specs=pl.BlockSpec((MOE_ROWS * ROW_TILE, LANES), lambda i, be: (i, 0)),
            scratch_shapes=[pltpu.VMEM((FFN_SLOTS, MOE_ROWS * ROW_TILE, LANES), F32), pltpu.VMEM((d, ff), BF16),
                            pltpu.VMEM((d, ff), BF16), pltpu.VMEM((ff, d), BF16),
                            pltpu.SemaphoreType.DMA((FFN_SLOTS,))]),
        out_shape=jax.ShapeDtypeStruct((r_total * ROW_TILE, LANES), F32),
        compiler_params=_cp("arbitrary"),
        name="moe_expert_ffn",
    )(blk_expert, row_token, row_token, row_token, u_tiles, w_gate, w_up, w_down)


def _moe_combine_kernel(d_cur, d_next, ys_hbm, h_ref, mod_ref, w_ref, o_ref, buf, sem):
    i = pl.program_id(0)
    n = pl.num_programs(0)
    tm = h_ref.shape[0]
    slot = i % 2
    nxt = 1 - slot

    @pl.when(i == 0)
    def _():
        for k in range(TOP_K):
            _gather_tiles_loop(ys_hbm, lambda r, k=k: d_cur[k, r], buf.at[0, k], tm, sem.at[0])

    for r in range(tm):
        for k in range(TOP_K):
            _tile_copy(ys_hbm, d_next[k, r], buf.at[nxt, k], r, sem.at[nxt]).start(priority=k)

    for k in range(TOP_K):
        _wait_tiles(ys_hbm, buf.at[slot, k], sem.at[slot])
    w = w_ref[...]
    y = w[:, 0:1] * _load_token_tiles(buf.at[slot, 0], tm) + w[:, 1:2] * _load_token_tiles(buf.at[slot, 1], tm)
    o_ref[...] = h_ref[...] + mod_ref[0, 5:6, :] * y

    @pl.when(i == n - 1)
    def _():
        for k in range(TOP_K):
            _wait_tiles(ys_hbm, buf.at[nxt, k], sem.at[nxt])


def moe_combine(ys, dest, h, mod, wts, *, tm=256):
    b, s, d = h.shape
    t = b * s
    tm = min(tm, s)
    per_b = s // tm
    n = t // tm
    smem_blk = functools.partial(pl.BlockSpec, (TOP_K, tm), memory_space=pltpu.SMEM)
    out = pl.pallas_call(
        _moe_combine_kernel,
        grid=(n,),
        in_specs=[smem_blk(lambda i: (0, i)),
                  smem_blk(lambda i: (0, jnp.minimum(i + 1, n - 1))),
                  pl.BlockSpec(memory_space=pl.ANY),
                  pl.BlockSpec((tm, d), lambda i: (i, 0)),
                  pl.BlockSpec((1, 6, d), lambda i: (i // per_b, 0, 0)),
                  pl.BlockSpec((tm, TOP_K), lambda i: (i, 0))],
        out_specs=pl.BlockSpec((tm, d), lambda i: (i, 0)),
        out_shape=jax.ShapeDtypeStruct((t, d), F32),
        scratch_shapes=[pltpu.VMEM((2, TOP_K, tm * ROW_TILE, LANES), F32), pltpu.SemaphoreType.DMA((2,))],
        compiler_params=_cp("arbitrary"),
        name="moe_combine",
    )(dest, dest, ys, h.reshape(t, d), mod, wts)
    return out.reshape(b, s, d)


def moe_layer(h, u, eidx, wts, mod, w_gate, w_up, w_down, layer):
    b, s, d = h.shape
    t = b * s
    e_flat = eidx.transpose(0, 2, 1).reshape(-1)
    n_pairs = t * TOP_K
    r_total = n_pairs + N_EXPERTS * MOE_ROWS
    nblk = r_total // MOE_ROWS
    onehot = (e_flat[:, None] == jnp.arange(N_EXPERTS, dtype=I32)[None, :]).astype(I32)
    csum = jnp.cumsum(onehot, axis=0)
    counts = csum[-1]
    rank = jnp.take_along_axis(csum, e_flat[:, None], axis=1)[:, 0] - 1
    padded = (counts + MOE_ROWS - 1) // MOE_ROWS * MOE_ROWS
    pad_end = jnp.cumsum(padded)
    pad_start = pad_end - padded
    dest = (pad_start[e_flat] + rank).astype(I32)
    row_token = jnp.zeros((r_total,), I32).at[dest].set(jnp.arange(n_pairs, dtype=I32) // TOP_K)
    blk_expert = jnp.minimum(jnp.searchsorted(pad_end, jnp.arange(nblk, dtype=I32) * MOE_ROWS, side="right"),
                             N_EXPERTS - 1).astype(I32)
    ys = moe_expert_ffn(u, row_token, blk_expert, w_gate, w_up, w_down, layer)
    dest2 = dest.reshape(t, TOP_K).T
    w_tok = wts.transpose(0, 2, 1).reshape(t, TOP_K)
    return moe_combine(ys, dest2, h, mod, w_tok)


def _pad_cols(w, n):
    return jnp.pad(w, ((0, 0), (0, n - w.shape[1])))


def fox_attention(h, mod, gain, w_in, b_f, q_gain, k_gain, cos, sin):
    b, s, _ = h.shape
    n_main = 3 * D_MODEL
    main, tail = norm_proj(h, mod, gain, w_in[:, :n_main].astype(BF16),
                           _pad_cols(w_in[:, n_main:], LANES).astype(BF16))
    gains = jnp.stack([_tile_gain(q_gain, Q_SCALE), _tile_gain(k_gain)])
    qk = head_prep(main, gains, (0, 1), cos, sin, rope=False)
    tk = min(FLASH_TILE, s)
    cum = fox_cumulative_gate(tail, b_f)
    key_bias = cum.reshape(b, N_PAIRS, 2, s // tk, tk).transpose(0, 1, 3, 2, 4)
    return flash_attention(qk, qk, main, q_off=0, k_off=N_PAIRS, v_off=2 * N_PAIRS, key_bias=key_bias, tk=tk)


def diff_attention(h, mod, gain, w_in, q_gain, k_gain, lambdas, sub_gain, layer_idx, cos, sin):
    b, s, _ = h.shape
    main = norm_proj(h, mod, gain, w_in.astype(BF16))
    gains = jnp.stack([_tile_gain(q_gain, Q_SCALE), _tile_gain(k_gain)])
    qk = head_prep(main, gains, (0, 1), cos, sin, rope=True)
    lam_init = 0.8 - 0.6 * math.exp(-0.3 * layer_idx)
    lam = jnp.pad(lambdas.astype(F32), ((0, 0), (0, LANES - HEAD_DIM)))
    return flash_attention(qk, qk, main, q_off=0, k_off=N_PAIRS, v_off=2 * N_PAIRS, fin="diff", lam=lam,
                           sub_gain=sub_gain.astype(F32).reshape(1, LANES), lam_init=lam_init)


def dilated_attention(h, mod, gain, w_in, q_gain, k_gain, cos, sin):
    b, s, _ = h.shape
    ng = len(DIL_PAIRS)
    main = norm_proj(h, mod, gain, w_in.astype(BF16), tn=3 * D_MODEL)
    gq, gk = _tile_gain(q_gain, Q_SCALE), _tile_gain(k_gain)
    qk = head_prep(main, jnp.stack([gq, gk] * ng), tuple(3 * g + j for g in range(ng) for j in range(2)),
                   cos, sin, rope=True)
    return dilated_groups_attention(qk, main)


def nsa_attention(h, mod, gain, w_in, q_gain, k_gain, cmp_pos, cmp_w1, cmp_w2, cos, sin, cos_h, sin_h):
    b, s, _ = h.shape
    nblk = s // NSA_BLOCK
    n_main = D_MODEL + 6 * NSA_GROUPS * HEAD_DIM
    main, tail = norm_proj(h, mod, gain, w_in[:, :n_main].astype(BF16),
                           _pad_cols(w_in[:, n_main:], LANES).astype(BF16))
    gains = jnp.zeros((8, LANES), F32)
    gains = gains.at[0].set(jnp.tile(q_gain.astype(F32) * Q_SCALE, 2))
    gains = gains.at[1].set(jnp.tile(k_gain[1].astype(F32), 2)).at[2].set(jnp.tile(k_gain[2].astype(F32), 2))
    q, ks2, vs2, kw2, vw2 = nsa_prep(main, gains, cos, sin)

    def to_block_rows(col0):
        x = main[:, :, col0:col0 + NSA_GROUPS * HEAD_DIM].reshape(b, nblk, NSA_BLOCK, NSA_GROUPS, HEAD_DIM)
        return x.transpose(0, 3, 1, 2, 4).reshape(b * NSA_GROUPS * nblk, NSA_BLOCK * HEAD_DIM)

    cos_b = jnp.tile(cos_h[NSA_BLOCK - 1::NSA_BLOCK], (1, 2))
    sin_b = jnp.concatenate([-sin_h[NSA_BLOCK - 1::NSA_BLOCK], sin_h[NSA_BLOCK - 1::NSA_BLOCK]], axis=-1)
    kc = nsa_compress(to_block_rows(D_MODEL), cmp_pos[0].reshape(1, -1), cmp_w1[0].astype(BF16),
                      cmp_w2[0].astype(BF16), k_gain[0].astype(F32).reshape(1, HEAD_DIM), cos_b, sin_b, is_key=True)
    vc = nsa_compress(to_block_rows(D_MODEL + NSA_GROUPS * HEAD_DIM), cmp_pos[1].reshape(1, -1),
                      cmp_w1[1].astype(BF16), cmp_w2[1].astype(BF16),
                      k_gain[0].astype(F32).reshape(1, HEAD_DIM), cos_b, sin_b, is_key=False)
    kc = kc.reshape(b, NSA_GROUPS, nblk, HEAD_DIM)
    vc = vc.reshape(b, NSA_GROUPS, nblk, HEAD_DIM)
    pad_rows = ((0, 0), (0, 0), (0, HEAD_DIM - nblk), (0, 0))
    kc = jnp.pad(kc, pad_rows)
    vc = jnp.pad(vc, pad_rows)
    kc2 = jnp.tile(kc, (1, 1, 2, 2)).astype(BF16)
    vc2 = jnp.concatenate([jnp.tile(vc, (1, 1, 1, 2)), jnp.zeros_like(jnp.tile(vc, (1, 1, 1, 2)))],
                          axis=2).astype(BF16)
    o_cmp, q_aug = nsa_compressed_attention(q, kc2, vc2, n_sel=min(NSA_TOPN, nblk))
    common = dict(q_off=0, k_off=0, v_off=0, k_div=NSA_HPG // (2 * FLASH_PAIRS), mode="aug")
    o_sel = flash_attention(q_aug, ks2, vs2, **common)
    o_win = flash_attention(q_aug, kw2, vw2, window=NSA_WINDOW, **common)
    return [o_cmp, o_sel, o_win, tail]


def kernel(x, c, fox_w_in, fox_b_f, fox_q_gain, fox_k_gain, fox_w_out, nsa_w_in, nsa_q_gain, nsa_k_gain, nsa_cmp_pos, nsa_cmp_w1, nsa_cmp_w2, nsa_w_out, dil_w_in, dil_q_gain, dil_k_gain, dil_w_out, diff_w_in, diff_q_gain, diff_k_gain, diff_lambda, diff_sub_gain, diff_w_out, norm_gain, ada_w, ada_b, router_w, router_b, moe_w_gate, moe_w_up, moe_w_down):
    b, s, d = x.shape
    depth = norm_gain.shape[0]
    cos, sin, cos_h, sin_h = rope_lane_tables(s)
    mods = ada_modulation(c, ada_w, ada_b).reshape(depth, b, 6, d)
    router_wt = router_w.T.astype(F32)
    router_bb = jnp.broadcast_to(router_b.astype(F32)[:, None], (N_EXPERTS, LANES))
    h = x
    for i in range(depth):
        mod = mods[i]
        kind, j = i % 4, i // 4
        g1 = norm_gain[i, 0]
        if kind == 0:
            attn = [fox_attention(h, mod, g1, fox_w_in[j], fox_b_f[j], fox_q_gain[j], fox_k_gain[j], cos, sin)]
            w_out, mix = fox_w_out[j], "plain"
        elif kind == 1:
            attn = nsa_attention(h, mod, g1, nsa_w_in[j], nsa_q_gain[j], nsa_k_gain[j], nsa_cmp_pos[j],
                                 nsa_cmp_w1[j], nsa_cmp_w2[j], cos, sin, cos_h, sin_h)
            w_out, mix = nsa_w_out[j], "nsa"
        elif kind == 2:
            attn = [dilated_attention(h, mod, g1, dil_w_in[j], dil_q_gain[j], dil_k_gain[j], cos, sin)]
            w_out, mix = dil_w_out[j], "plain"
        else:
            attn = [diff_attention(h, mod, g1, diff_w_in[j], diff_q_gain[j], diff_k_gain[j], diff_lambda[j],
                                   diff_sub_gain[j], i, cos, sin)]
            w_out, mix = diff_w_out[j], "plain"
        h, u, eidx, wts = outproj_router(attn, h, mod, w_out.astype(BF16), norm_gain[i, 1], router_wt,
                                         router_bb, mix=mix)
        h = moe_layer(h, u, eidx, wts, mod, moe_w_gate, moe_w_up, moe_w_down, i)
    return h
```

```python
import functools
import math

import jax
import jax.numpy as jnp
from jax import lax
from jax.experimental import pallas as pl
from jax.experimental.pallas import tpu as pltpu

F32 = jnp.float32
BF16 = jnp.bfloat16
I32 = jnp.int32

D_MODEL = 1024
HEAD_DIM = 64
LANES = 128
N_HEADS = D_MODEL // HEAD_DIM
N_PAIRS = D_MODEL // LANES
ROPE_THETA = 10000.0
EPS = 1e-6
NEG_INF = -1e30
TINY = 1e-30
M_INIT = -1e29
LOG2E = 1.4426950408889634
Q_SCALE = HEAD_DIM ** -0.5 * LOG2E

NSA_GROUPS = 4
NSA_HPG = N_HEADS // NSA_GROUPS
NSA_BLOCK = 64
NSA_TOPN = 16
NSA_WINDOW = 512
DIL_PAIRS = ((128, 1), (512, 4), (2048, 16))

N_EXPERTS = 16
N_EXPERT_GROUPS = 4
EXPERTS_PER_GROUP = 4
TOP_K = 2
EXPERT_FF = 512
MOE_ROWS = 512
FLASH_TILE = 512
FLASH_PAIRS = 2

VMEM_LIMIT = 52 * 1024 * 1024


def _cp(*sem, vmem=VMEM_LIMIT):
    return pltpu.CompilerParams(dimension_semantics=sem, vmem_limit_bytes=vmem)


def _split3(a):
    hi = a.astype(BF16)
    r1 = a - hi.astype(F32)
    mid = r1.astype(BF16)
    lo = (r1 - mid.astype(F32)).astype(BF16)
    return hi, mid, lo


def _dot(a, b):
    return jnp.dot(a, b, preferred_element_type=F32)


def _dot_nt(a, b):
    return lax.dot_general(a, b, (((1,), (1,)), ((), ())), preferred_element_type=F32)


def _dot_f32(a, b, dot=_dot):
    ah, am, al = _split3(a)
    bh, bm, bl = _split3(b)
    return (dot(ah, bh) + (dot(ah, bm) + dot(am, bh))
            + (dot(ah, bl) + dot(al, bh) + dot(am, bm)))


def _dot_f32_exact_rhs(a, b_bf16):
    ah, am, al = _split3(a)
    return _dot(ah, b_bf16) + _dot(am, b_bf16) + _dot(al, b_bf16)


def _sigmoid(x):
    return 1.0 / (1.0 + jnp.exp(-x))


def _silu(x):
    return x * _sigmoid(x)


def _ada_kernel(c_ref, w_ref, b_ref, o_ref):
    c = c_ref[...]
    o_ref[0] = _dot_f32(_silu(c), w_ref[0]) + b_ref[0]


def ada_modulation(c, ada_w, ada_b):
    depth, d, n = ada_w.shape
    b = c.shape[0]
    tn = 1024
    return pl.pallas_call(
        _ada_kernel,
        grid=(depth, n // tn),
        in_specs=[pl.BlockSpec((b, d), lambda i, j: (0, 0)),
                  pl.BlockSpec((1, d, tn), lambda i, j: (i, 0, j)),
                  pl.BlockSpec((1, 1, tn), lambda i, j: (i, 0, j))],
        out_specs=pl.BlockSpec((1, b, tn), lambda i, j: (i, 0, j)),
        out_shape=jax.ShapeDtypeStruct((depth, b, n), F32),
        compiler_params=_cp("parallel", "parallel"),
        name="ada_modulation",
    )(c, ada_w, ada_b.reshape(depth, 1, n))


def _modulated_norm(x, gain, mod, shift_row, scale_row):
    ms = jnp.mean(x * x, axis=-1, keepdims=True)
    y = x * lax.rsqrt(ms + EPS) * gain
    return y * (1.0 + mod[scale_row:scale_row + 1, :]) + mod[shift_row:shift_row + 1, :]


def _norm_proj_kernel(h_ref, mod_ref, g_ref, w_ref, *rest, has_tail):
    u = _modulated_norm(h_ref[0], g_ref[...], mod_ref[0], 0, 1).astype(BF16)
    if has_tail:
        wt_ref, main_ref, tail_ref = rest
        tail_ref[0] = _dot(u, wt_ref[...])
    else:
        (main_ref,) = rest
    main_ref[0] = _dot(u, w_ref[...]).astype(main_ref.dtype)


def norm_proj(h, mod, gain, w_main, w_tail=None, *, tn=None, tm=512):
    b, s, d = h.shape
    n = w_main.shape[1]
    tn = tn or n
    tm = min(tm, s)
    in_specs = [pl.BlockSpec((1, tm, d), lambda j, bi, si: (bi, si, 0)),
                pl.BlockSpec((1, 6, d), lambda j, bi, si: (bi, 0, 0)),
                pl.BlockSpec((1, d), lambda j, bi, si: (0, 0)),
                pl.BlockSpec((d, tn), lambda j, bi, si: (0, j))]
    out_specs = [pl.BlockSpec((1, tm, tn), lambda j, bi, si: (bi, si, j))]
    out_shape = [jax.ShapeDtypeStruct((b, s, n), BF16)]
    args = [h, mod, gain.reshape(1, d), w_main]
    if w_tail is not None:
        in_specs.append(pl.BlockSpec((d, LANES), lambda j, bi, si: (0, 0)))
        out_specs.append(pl.BlockSpec((1, tm, LANES), lambda j, bi, si: (bi, si, 0)))
        out_shape.append(jax.ShapeDtypeStruct((b, s, LANES), F32))
        args.append(w_tail)
    outs = pl.pallas_call(
        functools.partial(_norm_proj_kernel, has_tail=w_tail is not None),
        grid=(n // tn, b, s // tm),
        in_specs=in_specs, out_specs=out_specs, out_shape=out_shape,
        compiler_params=_cp("parallel", "parallel", "parallel"),
        name="norm_proj",
    )(*args)
    return outs if w_tail is not None else outs[0]


def _lane_iota(rows):
    return lax.broadcasted_iota(I32, (rows, LANES), 1)


def _head_block_diag():
    r = lax.broadcasted_iota(I32, (LANES, LANES), 0) // HEAD_DIM
    c = lax.broadcasted_iota(I32, (LANES, LANES), 1) // HEAD_DIM
    return (r == c).astype(BF16)


def _head_norm_rope(x, gain, cos, sin, bd, rope):
    y = x * x
    hi = y.astype(BF16)
    lo = (y - hi.astype(F32)).astype(BF16)
    seg = _dot(hi, bd) + _dot(lo, bd)
    xn = x * lax.rsqrt(seg * (1.0 / HEAD_DIM) + EPS) * gain
    if rope:
        first_half = (_lane_iota(x.shape[0]) % HEAD_DIM) < HEAD_DIM // 2
        partner = jnp.where(first_half, pltpu.roll(xn, LANES - HEAD_DIM // 2, 1),
                            pltpu.roll(xn, HEAD_DIM // 2, 1))
        xn = xn * cos + partner * sin
    return xn


def _prep_kernel(cb_ref, x_ref, g_ref, cos_ref, sin_ref, o_ref, *, rope):
    del cb_ref
    bd = _head_block_diag()
    cos = cos_ref[...]
    sin = sin_ref[...]
    for c in range(x_ref.shape[2] // LANES):
        sl = slice(c * LANES, (c + 1) * LANES)
        x = x_ref[0, :, sl].astype(F32)
        o_ref[0, :, sl] = _head_norm_rope(x, g_ref[0, :, sl], cos, sin, bd, rope).astype(o_ref.dtype)


def head_prep(src, gains, col_blocks, cos, sin, *, rope, ts=512):
    b, s, _ = src.shape
    n = len(col_blocks)
    ts = min(ts, s)
    cb = jnp.asarray(col_blocks, I32)

    def x_map(bi, si, ci, cb_ref):
        return (bi, si, cb_ref[ci])

    return pl.pallas_call(
        functools.partial(_prep_kernel, rope=rope),
        grid_spec=pltpu.PrefetchScalarGridSpec(
            num_scalar_prefetch=1,
            grid=(b, s // ts, n),
            in_specs=[pl.BlockSpec((1, ts, D_MODEL), x_map),
                      pl.BlockSpec((1, 1, D_MODEL), lambda bi, si, ci, cb_ref: (ci, 0, 0)),
                      pl.BlockSpec((ts, LANES), lambda bi, si, ci, cb_ref: (si, 0)),
                      pl.BlockSpec((ts, LANES), lambda bi, si, ci, cb_ref: (si, 0))],
            out_specs=pl.BlockSpec((1, ts, D_MODEL), lambda bi, si, ci, cb_ref: (bi, si, ci))),
        out_shape=jax.ShapeDtypeStruct((b, s, n * D_MODEL), BF16),
        compiler_params=_cp("parallel", "parallel", "arbitrary"),
        name="head_prep",
    )(cb, src, gains, cos, sin)


def rope_lane_tables(s):
    inv = ROPE_THETA ** (-jnp.arange(0, HEAD_DIM, 2, dtype=F32) / HEAD_DIM)
    ang = jnp.arange(s).astype(F32)[:, None] * inv[None, :]
    cos, sin = jnp.cos(ang), jnp.sin(ang)
    return jnp.tile(cos, (1, 4)), jnp.tile(jnp.concatenate([-sin, sin], axis=-1), (1, 2)), cos, sin


def _tile_gain(g, scale=1.0):
    return jnp.tile(g.astype(F32) * scale, N_HEADS).reshape(1, D_MODEL)


def _fox_cum_kernel(f_ref, b_ref, o_ref, carry_ref):
    si = pl.program_id(1)
    ts = f_ref.shape[1]

    @pl.when(si == 0)
    def _():
        carry_ref[...] = jnp.zeros_like(carry_ref)

    z = f_ref[0] + b_ref[...]
    log_f = -(jnp.maximum(-z, 0.0) + jnp.log1p(jnp.exp(-jnp.abs(z))))
    r = lax.broadcasted_iota(I32, (ts, ts), 0)
    c = lax.broadcasted_iota(I32, (ts, ts), 1)
    upper = (r <= c).astype(BF16)
    cum = _dot_f32_exact_rhs(log_f.T, upper) + carry_ref[:, 0:1]
    o_ref[0] = cum[0:N_HEADS, :] * LOG2E
    carry_ref[...] = jnp.broadcast_to(cum[:, ts - 1:ts], carry_ref.shape)


def fox_cumulative_gate(tail, b_f, *, ts=256):
    b, s, _ = tail.shape
    ts = min(ts, s)
    bias = jnp.zeros((1, LANES), F32).at[0, :N_HEADS].set(b_f.astype(F32))
    return pl.pallas_call(
        _fox_cum_kernel,
        grid=(b, s // ts),
        in_specs=[pl.BlockSpec((1, ts, LANES), lambda bi, si: (bi, si, 0)),
                  pl.BlockSpec((1, LANES), lambda bi, si: (0, 0))],
        out_specs=pl.BlockSpec((1, N_HEADS, ts), lambda bi, si: (bi, 0, si)),
        out_shape=jax.ShapeDtypeStruct((b, N_HEADS, s), F32),
        scratch_shapes=[pltpu.VMEM((LANES, LANES), F32)],
        compiler_params=_cp("parallel", "arbitrary"),
        name="fox_cumulative_gate",
    )(tail, bias)


def _flash_kernel(*refs, tq, tk, window, mode, fin, has_bias, pairs, lam_init):
    it = iter(refs)
    q_ref, k_ref, v_ref = next(it), next(it), next(it)
    kb_ref = next(it) if has_bias else None
    if fin == "diff":
        lam_ref, sg_ref = next(it), next(it)
    o_ref = next(it)
    m_sc, l_sc, acc_sc = next(it), next(it), next(it)

    n_heads = 2 * pairs
    q_start = pl.program_id(2) * tq
    lane = _lane_iota(tq)
    low_half = lane < HEAD_DIM
    qh = []
    for h in range(n_heads):
        if mode == "pair":
            q = q_ref[0, :, (h // 2) * LANES:(h // 2 + 1) * LANES]
            zero = jnp.zeros_like(q)
            qh.append(jnp.where(low_half, q, zero) if h % 2 == 0 else jnp.where(low_half, zero, q))
        else:
            qh.append(q_ref[0, :, h * LANES:(h + 1) * LANES])

    m_sc[...] = jnp.full(m_sc.shape, M_INIT, F32)
    l_sc[...] = jnp.zeros(l_sc.shape, F32)
    acc_sc[...] = jnp.zeros(acc_sc.shape, F32)
    row = q_start + lax.broadcasted_iota(I32, (tq, LANES), 0)
    n_chunk = tk // LANES

    def step(kv, masked):
        ks = pl.multiple_of(kv * tk, tk)
        kblk = k_ref[0, pl.ds(ks, tk), :]
        vblk = v_ref[0, pl.ds(ks, tk), :]
        if masked:
            masks = []
            for c in range(n_chunk):
                col = ks + c * LANES + lane
                mk = col <= row
                if window:
                    mk = mk & ((row - col) < window)
                masks.append(mk)
        for h in range(n_heads):
            if mode == "pair":
                kk = kblk[:, (h // 2) * LANES:(h // 2 + 1) * LANES]
                vv = vblk[:, (h // 2) * LANES:(h // 2 + 1) * LANES]
            else:
                kk = kblk[:, (h % 2) * LANES:(h % 2 + 1) * LANES]
                vv = vblk
            s = _dot_nt(qh[h], kk)
            if has_bias:
                s = s - kb_ref[0, h // 2, kv][h % 2:h % 2 + 1, :]
            chunks = [s[:, c * LANES:(c + 1) * LANES] for c in range(n_chunk)]
            if masked:
                chunks = [jnp.where(mk, ch, -jnp.inf) for mk, ch in zip(masks, chunks)]
            mb = functools.reduce(jnp.maximum, chunks)
            m_old = m_sc[h]
            m_new = jnp.maximum(m_old, jnp.broadcast_to(jnp.max(mb, axis=-1, keepdims=True), (tq, LANES)))
            alpha = jnp.exp2(m_old - m_new)
            ps = [jnp.exp2(ch - m_new) for ch in chunks]
            l_sc[h] = alpha * l_sc[h] + functools.reduce(jnp.add, ps)
            p = ps[0] if n_chunk == 1 else jnp.concatenate(ps, axis=1)
            acc_sc[h] = alpha * acc_sc[h] + _dot(p.astype(BF16), vv)
            m_sc[h] = m_new

    def loop(lo, hi, masked):
        def body(kv, carry):
            step(kv, masked)
            return carry
        lax.fori_loop(lo, hi, body, 0)

    last_blk = (q_start + (tq - 1)) // tk
    full_hi = (q_start + 1) // tk
    if window:
        first_blk = jnp.maximum(q_start - (window - 1), 0) // tk
        full_lo = jnp.maximum(q_start + (tq - 1) - window + tk, 0) // tk
        full_lo = jnp.maximum(jnp.minimum(full_lo, full_hi), first_blk)
        loop(first_blk, full_lo, True)
    else:
        full_lo = 0
    loop(full_lo, full_hi, False)
    loop(jnp.maximum(full_hi, full_lo), last_blk + 1, True)

    if fin == "diff":
        lam_rows = lam_ref[...]
        lam = (jnp.exp(jnp.sum(lam_rows[0:1] * lam_rows[1:2], axis=-1, keepdims=True))
               - jnp.exp(jnp.sum(lam_rows[2:3] * lam_rows[3:4], axis=-1, keepdims=True)) + lam_init)
    for pi in range(pairs):
        l0 = jnp.maximum(jnp.sum(l_sc[2 * pi], axis=-1, keepdims=True), TINY)
        l1 = jnp.maximum(jnp.sum(l_sc[2 * pi + 1], axis=-1, keepdims=True), TINY)
        o0 = acc_sc[2 * pi] * (1.0 / l0)
        o1 = acc_sc[2 * pi + 1] * (1.0 / l1)
        if fin == "select":
            o = jnp.where(low_half, o0, o1)
        else:
            o = o0 - lam * o1
            ms = jnp.mean(o * o, axis=-1, keepdims=True)
            o = o * lax.rsqrt(ms + EPS) * sg_ref[...] * (1.0 - lam_init)
        o_ref[0, :, pi * LANES:(pi + 1) * LANES] = o.astype(o_ref.dtype)


def flash_attention(q, k, v, *, q_off, k_off, v_off, k_div=1, mode="pair", fin="select", window=0,
                    key_bias=None, lam=None, sub_gain=None, lam_init=0.0, pairs=FLASH_PAIRS,
                    tq=FLASH_TILE, tk=FLASH_TILE):
    nb, s, _ = q.shape
    tq, tk = min(tq, s), min(tk, s)
    n_inner = N_PAIRS // pairs
    wo = pairs * LANES
    if mode == "pair":
        wq = wk = wv = wo
    else:
        wq, wk, wv = 2 * wo, 2 * LANES, LANES
    in_specs = [pl.BlockSpec((1, tq, wq), lambda b, j, i: (b, i, q_off // (wq // LANES) + j)),
                pl.BlockSpec((1, s, wk), lambda b, j, i: (b, 0, k_off // (wk // LANES) + j // k_div)),
                pl.BlockSpec((1, s, wv), lambda b, j, i: (b, 0, v_off // (wv // LANES) + j // k_div))]
    args = [q, k, v]
    if key_bias is not None:
        in_specs.append(pl.BlockSpec((1, pairs, s // tk, 2, tk), lambda b, j, i: (b, j, 0, 0, 0)))
        args.append(key_bias)
    if fin == "diff":
        in_specs += [pl.BlockSpec((4, LANES), lambda b, j, i: (0, 0)),
                     pl.BlockSpec((1, LANES), lambda b, j, i: (0, 0))]
        args += [lam, sub_gain]
    return pl.pallas_call(
        functools.partial(_flash_kernel, tq=tq, tk=tk, window=window, mode=mode, fin=fin,
                          has_bias=key_bias is not None, pairs=pairs, lam_init=lam_init),
        grid=(nb, n_inner, s // tq),
        in_specs=in_specs,
        out_specs=pl.BlockSpec((1, tq, wo), lambda b, j, i: (b, i, j)),
        out_shape=jax.ShapeDtypeStruct((nb, s, D_MODEL), BF16),
        scratch_shapes=[pltpu.VMEM((2 * pairs, tq, LANES), F32)] * 3,
        compiler_params=_cp("parallel", "parallel", "arbitrary"),
        name="flash_" + mode + "_" + fin,
    )(*args)


DIL_SUB = LANES
DIL_UNROLL = 4


def _rows(start, size, stride):
    return pl.ds(start, size) if stride == 1 else pl.ds(start, size, stride=stride)


def _dilated_kernel(*refs, seq):
    q_refs, k_refs, v_refs = refs[0:3], refs[3:6], refs[6:9]
    o_ref = refs[9]
    qf, kf, vf, m_st, l_st, acc_st = refs[10:16]
    sub = min(DIL_SUB, seq // DIL_PAIRS[-1][1])
    lane = _lane_iota(sub)
    low_half = lane < HEAD_DIM
    for g, (window, d) in enumerate(DIL_PAIRS):
        sd = seq // d
        span = min(2 * sub, sd)
        n_res = sd // sub
        win = window // d + 1
        assert win <= span - sub + 1 or span == sd
        qf[...] = q_refs[g][0].astype(F32)
        kf[...] = k_refs[g][0].astype(F32)
        vf[...] = v_refs[g][0].astype(F32)
        low_half_kv = _lane_iota(span) < HEAD_DIM
        ones = jnp.ones((span, LANES), BF16)

        def chain(t_idx, g=g, d=d, sd=sd, span=span, n_res=n_res, win=win, low_half_kv=low_half_kv, ones=ones):
            r = t_idx // n_res
            i0 = (t_idx % n_res) * sub
            ks = jnp.minimum(jnp.maximum(i0 - sub, 0), sd - span)
            q_rows = _rows(i0 * d + r, sub, d)
            kv_rows = _rows(ks * d + r, span, d)
            q = qf[q_rows, :].astype(BF16)
            kblk = kf[kv_rows, :].astype(BF16)
            vblk = vf[kv_rows, :].astype(BF16)
            zero = jnp.zeros_like(q)
            row = i0 + lax.broadcasted_iota(I32, (sub, LANES), 0)
            masks = []
            for c in range(span // LANES):
                col = ks + c * LANES + lane
                masks.append((col <= row) & ((row - col) < win))
            ms, accs = [], []
            for h in range(2):
                own = low_half if h == 0 else jnp.logical_not(low_half)
                own_kv = low_half_kv if h == 0 else jnp.logical_not(low_half_kv)
                s = _dot_nt(jnp.where(own, q, zero), kblk)
                chunks = [jnp.where(mk, s[:, c * LANES:(c + 1) * LANES], -jnp.inf) for c, mk in enumerate(masks)]
                m = jnp.broadcast_to(jnp.max(functools.reduce(jnp.maximum, chunks), axis=-1, keepdims=True),
                                     (sub, LANES))
                ps = [jnp.exp2(ch - m) for ch in chunks]
                p = ps[0] if len(ps) == 1 else jnp.concatenate(ps, axis=1)
                accs.append(_dot(p.astype(BF16), jnp.where(own_kv, vblk, ones)))
                ms.append(m)
            m_c = jnp.where(low_half, ms[0], ms[1])
            acc_c = jnp.where(low_half, accs[0], accs[1])
            l_c = jnp.where(low_half, pltpu.roll(accs[0], HEAD_DIM, 1), pltpu.roll(accs[1], HEAD_DIM, 1))
            if g == 0:
                m_st[q_rows, :] = m_c
                l_st[q_rows, :] = l_c
                acc_st[q_rows, :] = acc_c
            else:
                m_old = m_st[q_rows, :]
                m_new = jnp.maximum(m_old, m_c)
                a, bb = jnp.exp2(m_old - m_new), jnp.exp2(m_c - m_new)
                m_st[q_rows, :] = m_new
                l_st[q_rows, :] = a * l_st[q_rows, :] + bb * l_c
                acc_st[q_rows, :] = a * acc_st[q_rows, :] + bb * acc_c

        n_chain = seq // sub
        unroll = min(DIL_UNROLL, n_chain)

        def body(it, carry, chain=chain, unroll=unroll):
            for u in range(unroll):
                chain(it * unroll + u)
            return carry
        lax.fori_loop(0, n_chain // unroll, body, 0)
    o_ref[0] = (acc_st[...] * (1.0 / jnp.maximum(l_st[...], TINY))).astype(o_ref.dtype)


def dilated_groups_attention(qk, main):
    b, s, _ = qk.shape
    ng = len(DIL_PAIRS)
    blk = lambda off: pl.BlockSpec((1, s, LANES), lambda bi, j, off=off: (bi, 0, off + j))
    in_specs = ([blk(2 * g * N_PAIRS) for g in range(ng)] + [blk((2 * g + 1) * N_PAIRS) for g in range(ng)]
                + [blk((3 * g + 2) * N_PAIRS) for g in range(ng)])
    return pl.pallas_call(
        functools.partial(_dilated_kernel, seq=s),
        grid=(b, N_PAIRS),
        in_specs=in_specs,
        out_specs=pl.BlockSpec((1, s, LANES), lambda bi, j: (bi, 0, j)),
        out_shape=jax.ShapeDtypeStruct((b, s, D_MODEL), BF16),
        scratch_shapes=[pltpu.VMEM((s, LANES), F32)] * 6,
        compiler_params=_cp("parallel", "parallel"),
        name="dilated_groups_attention",
    )(*([qk] * (2 * ng) + [main] * ng))


def _nsa_prep_kernel(q_ref, ks_ref, vs_ref, kw_ref, vw_ref, g_ref, cos_ref, sin_ref,
                     qo_ref, ks2_ref, vs2_ref, kw2_ref, vw2_ref):
    ts = q_ref.shape[1]
    bd = _head_block_diag()
    cos, sin = cos_ref[...], sin_ref[...]
    lane = _lane_iota(ts)
    low_half = lane < HEAD_DIM
    for c in range(N_PAIRS):
        sl = slice(c * LANES, (c + 1) * LANES)
        qo_ref[0, :, sl] = _head_norm_rope(q_ref[0, :, sl].astype(F32), g_ref[0:1, :], cos, sin, bd,
                                           True).astype(qo_ref.dtype)
    t = pl.program_id(1) * ts + lax.broadcasted_iota(I32, (ts, LANES), 0)
    blk_onehot = ((t // NSA_BLOCK) == (lane % HEAD_DIM)).astype(F32)
    zeros = jnp.zeros((ts, LANES), F32)

    def spread(x, fill, out_ref, c):
        xr = pltpu.roll(x, HEAD_DIM, 1)
        base = 2 * c * 2 * LANES
        out_ref[0, :, base:base + LANES] = jnp.where(low_half, x, fill).astype(out_ref.dtype)
        out_ref[0, :, base + LANES:base + 2 * LANES] = jnp.where(low_half, fill, xr).astype(out_ref.dtype)
        out_ref[0, :, base + 2 * LANES:base + 3 * LANES] = jnp.where(low_half, xr, fill).astype(out_ref.dtype)
        out_ref[0, :, base + 3 * LANES:base + 4 * LANES] = jnp.where(low_half, fill, x).astype(out_ref.dtype)

    def dup(x, out_ref, c):
        xr = pltpu.roll(x, HEAD_DIM, 1)
        out_ref[0, :, 2 * c * LANES:(2 * c + 1) * LANES] = jnp.where(low_half, x, xr).astype(out_ref.dtype)
        out_ref[0, :, (2 * c + 1) * LANES:(2 * c + 2) * LANES] = jnp.where(low_half, xr, x).astype(out_ref.dtype)

    for c in range(NSA_GROUPS // 2):
        sl = slice(c * LANES, (c + 1) * LANES)
        ks = _head_norm_rope(ks_ref[0, :, sl].astype(F32), g_ref[1:2, :], cos, sin, bd, True)
        kw = _head_norm_rope(kw_ref[0, :, sl].astype(F32), g_ref[2:3, :], cos, sin, bd, True)
        spread(ks, blk_onehot, ks2_ref, c)
        spread(kw, zeros, kw2_ref, c)
        dup(vs_ref[0, :, sl].astype(F32), vs2_ref, c)
        dup(vw_ref[0, :, sl].astype(F32), vw2_ref, c)


def nsa_prep(main, gains, cos, sin, *, ts=512):
    b, s, _ = main.shape
    ts = min(ts, s)
    gw = NSA_GROUPS * HEAD_DIM

    def kv_spec(i):
        return pl.BlockSpec((1, ts, gw), lambda bi, si: (bi, si, i))

    return pl.pallas_call(
        _nsa_prep_kernel,
        grid=(b, s // ts),
        in_specs=[pl.BlockSpec((1, ts, D_MODEL), lambda bi, si: (bi, si, 0)),
                  kv_spec(6), kv_spec(7), kv_spec(8), kv_spec(9),
                  pl.BlockSpec((8, LANES), lambda bi, si: (0, 0)),
                  pl.BlockSpec((ts, LANES), lambda bi, si: (si, 0)),
                  pl.BlockSpec((ts, LANES), lambda bi, si: (si, 0))],
        out_specs=[pl.BlockSpec((1, ts, D_MODEL), lambda bi, si: (bi, si, 0)),
                   pl.BlockSpec((1, ts, NSA_GROUPS * 2 * LANES), lambda bi, si: (bi, si, 0)),
                   pl.BlockSpec((1, ts, NSA_GROUPS * LANES), lambda bi, si: (bi, si, 0)),
                   pl.BlockSpec((1, ts, NSA_GROUPS * 2 * LANES), lambda bi, si: (bi, si, 0)),
                   pl.BlockSpec((1, ts, NSA_GROUPS * LANES), lambda bi, si: (bi, si, 0))],
        out_shape=[jax.ShapeDtypeStruct((b, s, D_MODEL), BF16),
                   jax.ShapeDtypeStruct((b, s, NSA_GROUPS * 2 * LANES), BF16),
                   jax.ShapeDtypeStruct((b, s, NSA_GROUPS * LANES), BF16),
                   jax.ShapeDtypeStruct((b, s, NSA_GROUPS * 2 * LANES), BF16),
                   jax.ShapeDtypeStruct((b, s, NSA_GROUPS * LANES), BF16)],
        compiler_params=_cp("parallel", "parallel"),
        name="nsa_prep",
    )(main, main, main, main, main, gains, cos, sin)


def _nsa_compress_kernel(x_ref, pos_ref, w1_ref, w2_ref, g_ref, cos_ref, sin_ref, o_ref, *, is_key):
    x = (x_ref[...].astype(F32) + pos_ref[...]).astype(BF16)
    hid = _silu(_dot(x, w1_ref[...]))
    y = _dot(hid.astype(BF16), w2_ref[...])
    if is_key:
        ms = jnp.mean(y * y, axis=-1, keepdims=True)
        y = y * lax.rsqrt(ms + EPS) * g_ref[...]
        r = lax.broadcasted_iota(I32, (HEAD_DIM, HEAD_DIM), 0)
        c = lax.broadcasted_iota(I32, (HEAD_DIM, HEAD_DIM), 1)
        swap = (((r + HEAD_DIM // 2) % HEAD_DIM) == c).astype(BF16)
        y = y * cos_ref[...] + _dot_f32_exact_rhs(y, swap) * sin_ref[...]
    o_ref[...] = y


def nsa_compress(x, pos, w1, w2, gain, cos_blk, sin_blk, *, is_key):
    rows, k = x.shape
    nb = cos_blk.shape[0]
    hid = w1.shape[1]
    return pl.pallas_call(
        functools.partial(_nsa_compress_kernel, is_key=is_key),
        grid=(rows // nb,),
        in_specs=[pl.BlockSpec((nb, k), lambda i: (i, 0)),
                  pl.BlockSpec((1, k), lambda i: (0, 0)),
                  pl.BlockSpec((k, hid), lambda i: (0, 0)),
                  pl.BlockSpec((hid, HEAD_DIM), lambda i: (0, 0)),
                  pl.BlockSpec((1, HEAD_DIM), lambda i: (0, 0)),
                  pl.BlockSpec((nb, HEAD_DIM), lambda i: (0, 0)),
                  pl.BlockSpec((nb, HEAD_DIM), lambda i: (0, 0))],
        out_specs=pl.BlockSpec((nb, HEAD_DIM), lambda i: (i, 0)),
        out_shape=jax.ShapeDtypeStruct((rows, HEAD_DIM), F32),
        compiler_params=_cp("parallel"),
        name="nsa_compress",
    )(x, pos, w1, w2, gain, cos_blk, sin_blk)


def _nsa_cmp_kernel(q_ref, kc_ref, vc_ref, o_ref, qa_ref, *, n_sel):
    tq = q_ref.shape[1]
    lane = _lane_iota(tq)
    low_half = lane < HEAD_DIM
    blk = lane % HEAD_DIM
    t = pl.program_id(2) * tq + lax.broadcasted_iota(I32, (tq, LANES), 0)
    cmask = (blk + 1) * NSA_BLOCK <= t + 1
    kc = kc_ref[0, 0]
    vc = vc_ref[0, 0]
    imp = jnp.zeros((tq, LANES), F32)
    qblk = [q_ref[0, :, 0:LANES], q_ref[0, :, LANES:2 * LANES]]
    zero = jnp.zeros_like(qblk[0])
    outs = [None, None]
    for p in range(NSA_HPG):
        in_low = (p % 2) == 0
        qb = qblk[p // 2]
        qm = jnp.where(low_half, qb, zero) if in_low else jnp.where(low_half, zero, qb)
        s = jnp.where(cmask, _dot_nt(qm, kc), NEG_INF)
        m = jnp.max(s, axis=-1, keepdims=True)
        e = jnp.where(cmask, jnp.exp2(s - m), 0.0)
        den = jnp.maximum(0.5 * jnp.sum(e, axis=-1, keepdims=True), TINY)
        pc = e / den
        imp = imp + pc
        o = _dot(pc.astype(BF16), vc)
        prev = outs[p // 2]
        outs[p // 2] = o if prev is None else jnp.where(low_half, prev, o)
    o_ref[0, :, 0:LANES] = outs[0].astype(o_ref.dtype)
    o_ref[0, :, LANES:2 * LANES] = outs[1].astype(o_ref.dtype)

    cur = t // NSA_BLOCK
    forced = (blk == 0) | (blk == cur) | (blk == cur - 1)
    x = jnp.where(blk > cur, -1.0, jnp.where(forced, NSA_HPG + 1.0, imp))
    blk_f = blk.astype(F32)
    selected = jnp.zeros((tq, LANES), jnp.bool_)
    for _ in range(n_sel):
        mx = jnp.max(x, axis=-1, keepdims=True)
        first = jnp.min(jnp.where(x == mx, blk_f, float(LANES)), axis=-1, keepdims=True)
        hit = blk_f == first
        selected = selected | hit
        x = jnp.where(hit, -2.0, x)
    sel_bias = jnp.where(selected, 0.0, NEG_INF).astype(qa_ref.dtype)
    for p in range(NSA_HPG):
        qb = qblk[p // 2]
        qa = jnp.where(low_half, qb, sel_bias) if p % 2 == 0 else jnp.where(low_half, sel_bias, qb)
        qa_ref[0, :, p * LANES:(p + 1) * LANES] = qa


def nsa_compressed_attention(q, kc2, vc2, *, n_sel, tq=1024):
    b, s, _ = q.shape
    tq = min(tq, s)
    gq = NSA_HPG * HEAD_DIM
    return pl.pallas_call(
        functools.partial(_nsa_cmp_kernel, n_sel=n_sel),
        grid=(b, NSA_GROUPS, s // tq),
        in_specs=[pl.BlockSpec((1, tq, gq), lambda bi, g, i: (bi, i, g)),
                  pl.BlockSpec((1, 1, LANES, LANES), lambda bi, g, i: (bi, g, 0, 0)),
                  pl.BlockSpec((1, 1, LANES, LANES), lambda bi, g, i: (bi, g, 0, 0))],
        out_specs=[pl.BlockSpec((1, tq, gq), lambda bi, g, i: (bi, i, g)),
                   pl.BlockSpec((1, tq, 2 * gq), lambda bi, g, i: (bi, i, g))],
        out_shape=[jax.ShapeDtypeStruct((b, s, D_MODEL), BF16),
                   jax.ShapeDtypeStruct((b, s, 2 * D_MODEL), BF16)],
        compiler_params=_cp("parallel", "parallel", "parallel"),
        name="nsa_compressed_attention",
    )(q, kc2, vc2)


def _route(logits_t, rb):
    scores = _sigmoid(logits_t)
    sel = scores + rb
    rows = [sel[i:i + 1, :] for i in range(N_EXPERTS)]
    srows = [scores[i:i + 1, :] for i in range(N_EXPERTS)]
    best = grp = None
    for g in range(N_EXPERT_GROUPS):
        a, b, c, d = rows[4 * g:4 * g + 4]
        hi1, lo1, hi2, lo2 = jnp.maximum(a, b), jnp.minimum(a, b), jnp.maximum(c, d), jnp.minimum(c, d)
        gs = jnp.maximum(hi1, hi2) + jnp.maximum(jnp.minimum(hi1, hi2), jnp.maximum(lo1, lo2))
        if g == 0:
            best, grp = gs, jnp.zeros(gs.shape, I32)
        else:
            better = gs > best
            grp = jnp.where(better, g, grp)
            best = jnp.where(better, gs, best)

    def pick(vals, i):
        out = vals[i]
        for g in range(1, N_EXPERT_GROUPS):
            out = jnp.where(grp == g, vals[4 * g + i], out)
        return out

    v = [pick(rows, i) for i in range(EXPERTS_PER_GROUP)]
    w = [pick(srows, i) for i in range(EXPERTS_PER_GROUP)]
    l1, b1, w1 = jnp.zeros(grp.shape, I32), v[0], w[0]
    for i in range(1, EXPERTS_PER_GROUP):
        better = v[i] > b1
        l1 = jnp.where(better, i, l1)
        b1 = jnp.where(better, v[i], b1)
        w1 = jnp.where(better, w[i], w1)
    have = jnp.zeros(grp.shape, jnp.bool_)
    l2, b2, w2 = jnp.zeros(grp.shape, I32), jnp.zeros_like(b1), jnp.zeros_like(w1)
    for i in range(EXPERTS_PER_GROUP):
        valid = l1 != i
        better = valid & (jnp.logical_not(have) | (v[i] > b2))
        l2 = jnp.where(better, i, l2)
        b2 = jnp.where(better, v[i], b2)
        w2 = jnp.where(better, w[i], w2)
        have = have | valid
    wsum = w1 + w2
    return (grp * EXPERTS_PER_GROUP + l1, grp * EXPERTS_PER_GROUP + l2), (w1 / wsum, w2 / wsum)


def _outproj_router_kernel(*refs, mix):
    it = iter(refs)
    if mix == "plain":
        x = next(it)[0]
    elif mix == "nsa":
        o_refs = [next(it), next(it), next(it)]
        gl = _sigmoid(next(it)[0])
        g_hi = gl.astype(BF16)
        g_lo = (gl - g_hi.astype(F32)).astype(BF16)
        r = lax.broadcasted_iota(I32, (LANES, D_MODEL), 0)
        c = lax.broadcasted_iota(I32, (LANES, D_MODEL), 1) // HEAD_DIM
        x = None
        for i in range(3):
            expand = (r == c + i * N_HEADS).astype(BF16)
            term = (_dot(g_hi, expand) + _dot(g_lo, expand)) * o_refs[i][0].astype(F32)
            x = term if x is None else x + term
        x = x.astype(BF16)
    else:
        raise ValueError(mix)
    h_ref, mod_ref, w_ref, g2_ref, rw_ref, rb_ref = (next(it) for _ in range(6))
    ho_ref, u_ref, e_ref, wt_ref = (next(it) for _ in range(4))
    mod = mod_ref[0]
    h_new = h_ref[0] + mod[2:3, :] * _dot(x, w_ref[...])
    ho_ref[0] = h_new
    u = _modulated_norm(h_new, g2_ref[...], mod, 3, 4)
    _store_token_tiles(u_ref, u)
    logits_t = _dot_f32(rw_ref[...], u, dot=_dot_nt)
    eidx, wts = _route(logits_t, rb_ref[:, 0:1])
    for k in range(TOP_K):
        e_ref[0, k:k + 1, :] = eidx[k]
        wt_ref[0, k:k + 1, :] = wts[k]


def outproj_router(attn_inputs, h, mod, w_out, gain2, router_wt, router_b, *, mix, tm=512):
    b, s, d = h.shape
    tm = min(tm, s)
    row_spec = pl.BlockSpec((1, tm, d), lambda bi, si: (bi, si, 0))
    in_specs, args = [], []
    for a in attn_inputs:
        in_specs.append(pl.BlockSpec((1, tm, a.shape[2]), lambda bi, si: (bi, si, 0)))
        args.append(a)
    in_specs += [row_spec,
                 pl.BlockSpec((1, 6, d), lambda bi, si: (bi, 0, 0)),
                 pl.BlockSpec((d, d), lambda bi, si: (0, 0)),
                 pl.BlockSpec((1, d), lambda bi, si: (0, 0)),
                 pl.BlockSpec((N_EXPERTS, d), lambda bi, si: (0, 0)),
                 pl.BlockSpec((N_EXPERTS, LANES), lambda bi, si: (0, 0))]
    args += [h, mod, w_out, gain2.reshape(1, d), router_wt, router_b]
    return pl.pallas_call(
        functools.partial(_outproj_router_kernel, mix=mix),
        grid=(b, s // tm),
        in_specs=in_specs,
        out_specs=[row_spec, pl.BlockSpec((tm * ROW_TILE, LANES), lambda bi, si: (bi * (s // tm) + si, 0)),
                   pl.BlockSpec((1, TOP_K, tm), lambda bi, si: (bi, 0, si)),
                   pl.BlockSpec((1, TOP_K, tm), lambda bi, si: (bi, 0, si))],
        out_shape=[jax.ShapeDtypeStruct((b, s, d), F32), jax.ShapeDtypeStruct((b * s * ROW_TILE, LANES), F32),
                   jax.ShapeDtypeStruct((b, TOP_K, s), I32), jax.ShapeDtypeStruct((b, TOP_K, s), F32)],
        compiler_params=_cp("parallel", "parallel"),
        name="outproj_router_" + mix,
    )(*args)


ROW_TILE = D_MODEL // LANES
FFN_PIECES = 6
FFN_SLOTS = 3


def _store_token_tiles(ref, x):
    rows = x.shape[0]
    for c in range(ROW_TILE):
        ref[pl.ds(c, rows, stride=ROW_TILE), :] = x[:, c * LANES:(c + 1) * LANES]


def _load_token_tiles(ref, rows):
    return jnp.concatenate([ref[pl.ds(c, rows, stride=ROW_TILE), :] for c in range(ROW_TILE)], axis=1)


def _tile_copy(src_hbm, index, dst, r, sem):
    start = pl.multiple_of(index * ROW_TILE, ROW_TILE)
    return pltpu.make_async_copy(src_hbm.at[pl.ds(start, ROW_TILE), :], dst.at[pl.ds(r * ROW_TILE, ROW_TILE), :], sem)


def _gather_tiles_loop(src_hbm, index_of, dst, n_rows, sem):
    def one(r, carry):
        _tile_copy(src_hbm, index_of(r), dst, r, sem).start()
        return carry
    lax.fori_loop(0, n_rows, one, 0)


def _wait_tiles(src_hbm, dst, sem):
    pltpu.make_async_copy(src_hbm.at[pl.ds(0, dst.shape[0]), :], dst, sem).wait()


def _moe_ffn_kernel(be_ref, tok_cur, tok_next, tok_ahead, u_hbm, wg_ref, wu_ref, wd_ref, y_ref, xbuf, wg_sc, wu_sc,
                    wd_sc, sem):
    i = pl.program_id(0)
    n = pl.num_programs(0)
    rows = MOE_ROWS
    slot = i % FFN_SLOTS
    nxt = (i + 2) % FFN_SLOTS

    @pl.when(i == 0)
    def _():
        _gather_tiles_loop(u_hbm, lambda r: tok_cur[r], xbuf.at[0], rows, sem.at[0])
        _gather_tiles_loop(u_hbm, lambda r: tok_next[r], xbuf.at[1], rows, sem.at[1])

    @pl.when((i == 0) | (be_ref[i] != be_ref[jnp.maximum(i - 1, 0)]))
    def _():
        wg_sc[...] = wg_ref[0, 0].astype(BF16)
        wu_sc[...] = wu_ref[0, 0].astype(BF16)
        wd_sc[...] = wd_ref[0, 0].astype(BF16)

    _wait_tiles(u_hbm, xbuf.at[slot], sem.at[slot])

    per = rows // FFN_PIECES + 1
    issued = [0]

    def issue_some():
        lo, hi = issued[0], min(issued[0] + per, rows)
        for r in range(lo, hi):
            _tile_copy(u_hbm, tok_ahead[r], xbuf.at[nxt], r, sem.at[nxt]).start(priority=r % 2)
        issued[0] = hi

    x = _load_token_tiles(xbuf.at[slot], rows).astype(BF16)
    g = _dot(x, wg_sc[...])
    issue_some()
    u = _dot(x, wu_sc[...])
    issue_some()
    hid = (_silu(g) * u).astype(BF16)
    d = wd_sc.shape[1]
    n_out = FFN_PIECES - 2
    w = d // n_out
    for c in range(n_out):
        y = _dot(hid, wd_sc[:, c * w:(c + 1) * w])
        for j in range(w // LANES):
            y_ref[pl.ds(c * (w // LANES) + j, rows, stride=ROW_TILE), :] = y[:, j * LANES:(j + 1) * LANES]
        issue_some()
    assert issued[0] == rows

    @pl.when(i == n - 1)
    def _():
        for ahead in (1, 2):
            s = (i + ahead) % FFN_SLOTS
            _wait_tiles(u_hbm, xbuf.at[s], sem.at[s])


def moe_expert_ffn(u_tiles, row_token, blk_expert, w_gate, w_up, w_down, layer):
    r_total = row_token.shape[0]
    nblk = r_total // MOE_ROWS
    d, ff = w_gate.shape[2], w_gate.shape[3]
    smem_blk = functools.partial(pl.BlockSpec, (MOE_ROWS,), memory_space=pltpu.SMEM)
    return pl.pallas_call(
        _moe_ffn_kernel,
        grid_spec=pltpu.PrefetchScalarGridSpec(
            num_scalar_prefetch=1,
            grid=(nblk,),
            in_specs=[smem_blk(lambda i, be: (i,)),
                      smem_blk(lambda i, be: (jnp.minimum(i + 1, nblk - 1),)),
                      smem_blk(lambda i, be: (jnp.minimum(i + 2, nblk - 1),)),
                      pl.BlockSpec(memory_space=pl.ANY),
                      pl.BlockSpec((1, 1, d, ff), lambda i, be: (layer, be[i], 0, 0)),
                      pl.BlockSpec((1, 1, d, ff), lambda i, be: (layer, be[i], 0, 0)),
                      pl.BlockSpec((1, 1, ff, d), lambda i, be: (layer, be[i], 0, 0))],
            out_specs=pl.BlockSpec((MOE_ROWS * ROW_TILE, LANES), lambda i, be: (i, 0)),
            scratch_shapes=[pltpu.VMEM((FFN_SLOTS, MOE_ROWS * ROW_TILE, LANES), F32), pltpu.VMEM((d, ff), BF16),
                            pltpu.VMEM((d, ff), BF16), pltpu.VMEM((ff, d), BF16),
                            pltpu.SemaphoreType.DMA((FFN_SLOTS,))]),
        out_shape=jax.ShapeDtypeStruct((r_total * ROW_TILE, LANES), F32),
        compiler_params=_cp("arbitrary"),
        name="moe_expert_ffn",
    )(blk_expert, row_token, row_token, row_token, u_tiles, w_gate, w_up, w_down)


def _moe_combine_kernel(d_cur, d_next, ys_hbm, h_ref, mod_ref, w_ref, o_ref, buf, sem):
    i = pl.program_id(0)
    n = pl.num_programs(0)
    tm = h_ref.shape[0]
    slot = i % 2
    nxt = 1 - slot

    @pl.when(i == 0)
    def _():
        for k in range(TOP_K):
            _gather_tiles_loop(ys_hbm, lambda r, k=k: d_cur[k, r], buf.at[0, k], tm, sem.at[0])

    for r in range(tm):
        for k in range(TOP_K):
            _tile_copy(ys_hbm, d_next[k, r], buf.at[nxt, k], r, sem.at[nxt]).start(priority=k)

    for k in range(TOP_K):
        _wait_tiles(ys_hbm, buf.at[slot, k], sem.at[slot])
    w = w_ref[...]
    y = w[:, 0:1] * _load_token_tiles(buf.at[slot, 0], tm) + w[:, 1:2] * _load_token_tiles(buf.at[slot, 1], tm)
    o_ref[...] = h_ref[...] + mod_ref[0, 5:6, :] * y

    @pl.when(i == n - 1)
    def _():
        for k in range(TOP_K):
            _wait_tiles(ys_hbm, buf.at[nxt, k], sem.at[nxt])


def moe_combine(ys, dest, h, mod, wts, *, tm=256):
    b, s, d = h.shape
    t = b * s
    tm = min(tm, s)
    per_b = s // tm
    n = t // tm
    smem_blk = functools.partial(pl.BlockSpec, (TOP_K, tm), memory_space=pltpu.SMEM)
    out = pl.pallas_call(
        _moe_combine_kernel,
        grid=(n,),
        in_specs=[smem_blk(lambda i: (0, i)),
                  smem_blk(lambda i: (0, jnp.minimum(i + 1, n - 1))),
                  pl.BlockSpec(memory_space=pl.ANY),
                  pl.BlockSpec((tm, d), lambda i: (i, 0)),
                  pl.BlockSpec((1, 6, d), lambda i: (i // per_b, 0, 0)),
                  pl.BlockSpec((tm, TOP_K), lambda i: (i, 0))],
        out_specs=pl.BlockSpec((tm, d), lambda i: (i, 0)),
        out_shape=jax.ShapeDtypeStruct((t, d), F32),
        scratch_shapes=[pltpu.VMEM((2, TOP_K, tm * ROW_TILE, LANES), F32), pltpu.SemaphoreType.DMA((2,))],
        compiler_params=_cp("arbitrary"),
        name="moe_combine",
    )(dest, dest, ys, h.reshape(t, d), mod, wts)
    return out.reshape(b, s, d)


def moe_layer(h, u, eidx, wts, mod, w_gate, w_up, w_down, layer):
    b, s, d = h.shape
    t = b * s
    e_flat = eidx.transpose(0, 2, 1).reshape(-1)
    n_pairs = t * TOP_K
    r_total = n_pairs + N_EXPERTS * MOE_ROWS
    nblk = r_total // MOE_ROWS
    onehot = (e_flat[:, None] == jnp.arange(N_EXPERTS, dtype=I32)[None, :]).astype(I32)
    csum = jnp.cumsum(onehot, axis=0)
    counts = csum[-1]
    rank = jnp.take_along_axis(csum, e_flat[:, None], axis=1)[:, 0] - 1
    padded = (counts + MOE_ROWS - 1) // MOE_ROWS * MOE_ROWS
    pad_end = jnp.cumsum(padded)
    pad_start = pad_end - padded
    dest = (pad_start[e_flat] + rank).astype(I32)
    row_token = jnp.zeros((r_total,), I32).at[dest].set(jnp.arange(n_pairs, dtype=I32) // TOP_K)
    blk_expert = jnp.minimum(jnp.searchsorted(pad_end, jnp.arange(nblk, dtype=I32) * MOE_ROWS, side="right"),
                             N_EXPERTS - 1).astype(I32)
    ys = moe_expert_ffn(u, row_token, blk_expert, w_gate, w_up, w_down, layer)
    dest2 = dest.reshape(t, TOP_K).T
    w_tok = wts.transpose(0, 2, 1).reshape(t, TOP_K)
    return moe_combine(ys, dest2, h, mod, w_tok)


def _pad_cols(w, n):
    return jnp.pad(w, ((0, 0), (0, n - w.shape[1])))


def fox_attention(h, mod, gain, w_in, b_f, q_gain, k_gain, cos, sin):
    b, s, _ = h.shape
    n_main = 3 * D_MODEL
    main, tail = norm_proj(h, mod, gain, w_in[:, :n_main].astype(BF16),
                           _pad_cols(w_in[:, n_main:], LANES).astype(BF16))
    gains = jnp.stack([_tile_gain(q_gain, Q_SCALE), _tile_gain(k_gain)])
    qk = head_prep(main, gains, (0, 1), cos, sin, rope=False)
    tk = min(FLASH_TILE, s)
    cum = fox_cumulative_gate(tail, b_f)
    key_bias = cum.reshape(b, N_PAIRS, 2, s // tk, tk).transpose(0, 1, 3, 2, 4)
    return flash_attention(qk, qk, main, q_off=0, k_off=N_PAIRS, v_off=2 * N_PAIRS, key_bias=key_bias, tk=tk)


def diff_attention(h, mod, gain, w_in, q_gain, k_gain, lambdas, sub_gain, layer_idx, cos, sin):
    b, s, _ = h.shape
    main = norm_proj(h, mod, gain, w_in.astype(BF16))
    gains = jnp.stack([_tile_gain(q_gain, Q_SCALE), _tile_gain(k_gain)])
    qk = head_prep(main, gains, (0, 1), cos, sin, rope=True)
    lam_init = 0.8 - 0.6 * math.exp(-0.3 * layer_idx)
    lam = jnp.pad(lambdas.astype(F32), ((0, 0), (0, LANES - HEAD_DIM)))
    return flash_attention(qk, qk, main, q_off=0, k_off=N_PAIRS, v_off=2 * N_PAIRS, fin="diff", lam=lam,
                           sub_gain=sub_gain.astype(F32).reshape(1, LANES), lam_init=lam_init)


def dilated_attention(h, mod, gain, w_in, q_gain, k_gain, cos, sin):
    b, s, _ = h.shape
    ng = len(DIL_PAIRS)
    main = norm_proj(h, mod, gain, w_in.astype(BF16), tn=3 * D_MODEL)
    gq, gk = _tile_gain(q_gain, Q_SCALE), _tile_gain(k_gain)
    qk = head_prep(main, jnp.stack([gq, gk] * ng), tuple(3 * g + j for g in range(ng) for j in range(2)),
                   cos, sin, rope=True)
    return dilated_groups_attention(qk, main)


def nsa_attention(h, mod, gain, w_in, q_gain, k_gain, cmp_pos, cmp_w1, cmp_w2, cos, sin, cos_h, sin_h):
    b, s, _ = h.shape
    nblk = s // NSA_BLOCK
    n_main = D_MODEL + 6 * NSA_GROUPS * HEAD_DIM
    main, tail = norm_proj(h, mod, gain, w_in[:, :n_main].astype(BF16),
                           _pad_cols(w_in[:, n_main:], LANES).astype(BF16))
    gains = jnp.zeros((8, LANES), F32)
    gains = gains.at[0].set(jnp.tile(q_gain.astype(F32) * Q_SCALE, 2))
    gains = gains.at[1].set(jnp.tile(k_gain[1].astype(F32), 2)).at[2].set(jnp.tile(k_gain[2].astype(F32), 2))
    q, ks2, vs2, kw2, vw2 = nsa_prep(main, gains, cos, sin)

    def to_block_rows(col0):
        x = main[:, :, col0:col0 + NSA_GROUPS * HEAD_DIM].reshape(b, nblk, NSA_BLOCK, NSA_GROUPS, HEAD_DIM)
        return x.transpose(0, 3, 1, 2, 4).reshape(b * NSA_GROUPS * nblk, NSA_BLOCK * HEAD_DIM)

    cos_b = jnp.tile(cos_h[NSA_BLOCK - 1::NSA_BLOCK], (1, 2))
    sin_b = jnp.concatenate([-sin_h[NSA_BLOCK - 1::NSA_BLOCK], sin_h[NSA_BLOCK - 1::NSA_BLOCK]], axis=-1)
    kc = nsa_compress(to_block_rows(D_MODEL), cmp_pos[0].reshape(1, -1), cmp_w1[0].astype(BF16),
                      cmp_w2[0].astype(BF16), k_gain[0].astype(F32).reshape(1, HEAD_DIM), cos_b, sin_b, is_key=True)
    vc = nsa_compress(to_block_rows(D_MODEL + NSA_GROUPS * HEAD_DIM), cmp_pos[1].reshape(1, -1),
                      cmp_w1[1].astype(BF16), cmp_w2[1].astype(BF16),
                      k_gain[0].astype(F32).reshape(1, HEAD_DIM), cos_b, sin_b, is_key=False)
    kc = kc.reshape(b, NSA_GROUPS, nblk, HEAD_DIM)
    vc = vc.reshape(b, NSA_GROUPS, nblk, HEAD_DIM)
    pad_rows = ((0, 0), (0, 0), (0, HEAD_DIM - nblk), (0, 0))
    kc = jnp.pad(kc, pad_rows)
    vc = jnp.pad(vc, pad_rows)
    kc2 = jnp.tile(kc, (1, 1, 2, 2)).astype(BF16)
    vc2 = jnp.concatenate([jnp.tile(vc, (1, 1, 1, 2)), jnp.zeros_like(jnp.tile(vc, (1, 1, 1, 2)))],
                          axis=2).astype(BF16)
    o_cmp, q_aug = nsa_compressed_attention(q, kc2, vc2, n_sel=min(NSA_TOPN, nblk))
    common = dict(q_off=0, k_off=0, v_off=0, k_div=NSA_HPG // (2 * FLASH_PAIRS), mode="aug")
    o_sel = flash_attention(q_aug, ks2, vs2, **common)
    o_win = flash_attention(q_aug, kw2, vw2, window=NSA_WINDOW, **common)
    return [o_cmp, o_sel, o_win, tail]


def kernel(x, c, fox_w_in, fox_b_f, fox_q_gain, fox_k_gain, fox_w_out, nsa_w_in, nsa_q_gain, nsa_k_gain, nsa_cmp_pos, nsa_cmp_w1, nsa_cmp_w2, nsa_w_out, dil_w_in, dil_q_gain, dil_k_gain, dil_w_out, diff_w_in, diff_q_gain, diff_k_gain, diff_lambda, diff_sub_gain, diff_w_out, norm_gain, ada_w, ada_b, router_w, router_b, moe_w_gate, moe_w_up, moe_w_down):
    b, s, d = x.shape
    depth = norm_gain.shape[0]
    cos, sin, cos_h, sin_h = rope_lane_tables(s)
    mods = ada_modulation(c, ada_w, ada_b).reshape(depth, b, 6, d)
    router_wt = router_w.T.astype(F32)
    router_bb = jnp.broadcast_to(router_b.astype(F32)[:, None], (N_EXPERTS, LANES))
    h = x
    for i in range(depth):
        mod = mods[i]
        kind, j = i % 4, i // 4
        g1 = norm_gain[i, 0]
        if kind == 0:
            attn = [fox_attention(h, mod, g1, fox_w_in[j], fox_b_f[j], fox_q_gain[j], fox_k_gain[j], cos, sin)]
            w_out, mix = fox_w_out[j], "plain"
        elif kind == 1:
            attn = nsa_attention(h, mod, g1, nsa_w_in[j], nsa_q_gain[j], nsa_k_gain[j], nsa_cmp_pos[j],
                                 nsa_cmp_w1[j], nsa_cmp_w2[j], cos, sin, cos_h, sin_h)
            w_out, mix = nsa_w_out[j], "nsa"
        elif kind == 2:
            attn = [dilated_attention(h, mod, g1, dil_w_in[j], dil_q_gain[j], dil_k_gain[j], cos, sin)]
            w_out, mix = dil_w_out[j], "plain"
        else:
            attn = [diff_attention(h, mod, g1, diff_w_in[j], diff_q_gain[j], diff_k_gain[j], diff_lambda[j],
                                   diff_sub_gain[j], i, cos, sin)]
            w_out, mix = diff_w_out[j], "plain"
        h, u, eidx, wts = outproj_router(attn, h, mod, w_out.astype(BF16), norm_gain[i, 1], router_wt,
                                         router_bb, mix=mix)
        h = moe_layer(h, u, eidx, wts, mod, moe_w_gate, moe_w_up, moe_w_down, i)
    return h
```

```python
import functools
import math

import jax
import jax.numpy as jnp
from jax import lax
from jax.experimental import pallas as pl
from jax.experimental.pallas import tpu as pltpu

F32 = jnp.float32
BF16 = jnp.bfloat16
I32 = jnp.int32

D_MODEL = 1024
HEAD_DIM = 64
LANES = 128
N_HEADS = D_MODEL // HEAD_DIM
N_PAIRS = D_MODEL // LANES
ROPE_THETA = 10000.0
EPS = 1e-6
NEG_INF = -1e30
TINY = 1e-30
M_INIT = -1e29
LOG2E = 1.4426950408889634
Q_SCALE = HEAD_DIM ** -0.5 * LOG2E

NSA_GROUPS = 4
NSA_HPG = N_HEADS // NSA_GROUPS
NSA_BLOCK = 64
NSA_TOPN = 16
NSA_WINDOW = 512
DIL_PAIRS = ((128, 1), (512, 4), (2048, 16))

N_EXPERTS = 16
N_EXPERT_GROUPS = 4
EXPERTS_PER_GROUP = 4
TOP_K = 2
EXPERT_FF = 512
MOE_ROWS = 256
FLASH_TILE = 512
FLASH_PAIRS = 2

VMEM_LIMIT = 52 * 1024 * 1024


def _cp(*sem, vmem=VMEM_LIMIT):
    return pltpu.CompilerParams(dimension_semantics=sem, vmem_limit_bytes=vmem)


def _split3(a):
    hi = a.astype(BF16)
    r1 = a - hi.astype(F32)
    mid = r1.astype(BF16)
    lo = (r1 - mid.astype(F32)).astype(BF16)
    return hi, mid, lo


def _dot(a, b):
    return jnp.dot(a, b, preferred_element_type=F32)


def _dot_nt(a, b):
    return lax.dot_general(a, b, (((1,), (1,)), ((), ())), preferred_element_type=F32)


def _dot_f32(a, b, dot=_dot):
    ah, am, al = _split3(a)
    bh, bm, bl = _split3(b)
    return (dot(ah, bh) + (dot(ah, bm) + dot(am, bh))
            + (dot(ah, bl) + dot(al, bh) + dot(am, bm)))


def _dot_f32_exact_rhs(a, b_bf16):
    ah, am, al = _split3(a)
    return _dot(ah, b_bf16) + _dot(am, b_bf16) + _dot(al, b_bf16)


def _sigmoid(x):
    return 1.0 / (1.0 + jnp.exp(-x))


def _silu(x):
    return x * _sigmoid(x)


def _ada_kernel(c_ref, w_ref, b_ref, o_ref):
    c = c_ref[...]
    o_ref[0] = _dot_f32(_silu(c), w_ref[0]) + b_ref[0]


def ada_modulation(c, ada_w, ada_b):
    depth, d, n = ada_w.shape
    b = c.shape[0]
    tn = 1024
    return pl.pallas_call(
        _ada_kernel,
        grid=(depth, n // tn),
        in_specs=[pl.BlockSpec((b, d), lambda i, j: (0, 0)),
                  pl.BlockSpec((1, d, tn), lambda i, j: (i, 0, j)),
                  pl.BlockSpec((1, 1, tn), lambda i, j: (i, 0, j))],
        out_specs=pl.BlockSpec((1, b, tn), lambda i, j: (i, 0, j)),
        out_shape=jax.ShapeDtypeStruct((depth, b, n), F32),
        compiler_params=_cp("parallel", "parallel"),
        name="ada_modulation",
    )(c, ada_w, ada_b.reshape(depth, 1, n))


def _modulated_norm(x, gain, mod, shift_row, scale_row):
    ms = jnp.mean(x * x, axis=-1, keepdims=True)
    y = x * lax.rsqrt(ms + EPS) * gain
    return y * (1.0 + mod[scale_row:scale_row + 1, :]) + mod[shift_row:shift_row + 1, :]


def _norm_proj_kernel(h_ref, mod_ref, g_ref, w_ref, *rest, has_tail):
    u = _modulated_norm(h_ref[0], g_ref[...], mod_ref[0], 0, 1).astype(BF16)
    if has_tail:
        wt_ref, main_ref, tail_ref = rest
        tail_ref[0] = _dot(u, wt_ref[...])
    else:
        (main_ref,) = rest
    main_ref[0] = _dot(u, w_ref[...]).astype(main_ref.dtype)


def norm_proj(h, mod, gain, w_main, w_tail=None, *, tn=None, tm=512):
    b, s, d = h.shape
    n = w_main.shape[1]
    tn = tn or n
    tm = min(tm, s)
    in_specs = [pl.BlockSpec((1, tm, d), lambda j, bi, si: (bi, si, 0)),
                pl.BlockSpec((1, 6, d), lambda j, bi, si: (bi, 0, 0)),
                pl.BlockSpec((1, d), lambda j, bi, si: (0, 0)),
                pl.BlockSpec((d, tn), lambda j, bi, si: (0, j))]
    out_specs = [pl.BlockSpec((1, tm, tn), lambda j, bi, si: (bi, si, j))]
    out_shape = [jax.ShapeDtypeStruct((b, s, n), BF16)]
    args = [h, mod, gain.reshape(1, d), w_main]
    if w_tail is not None:
        in_specs.append(pl.BlockSpec((d, LANES), lambda j, bi, si: (0, 0)))
        out_specs.append(pl.BlockSpec((1, tm, LANES), lambda j, bi, si: (bi, si, 0)))
        out_shape.append(jax.ShapeDtypeStruct((b, s, LANES), F32))
        args.append(w_tail)
    outs = pl.pallas_call(
        functools.partial(_norm_proj_kernel, has_tail=w_tail is not None),
        grid=(n // tn, b, s // tm),
        in_specs=in_specs, out_specs=out_specs, out_shape=out_shape,
        compiler_params=_cp("parallel", "parallel", "parallel"),
        name="norm_proj",
    )(*args)
    return outs if w_tail is not None else outs[0]


def _lane_iota(rows):
    return lax.broadcasted_iota(I32, (rows, LANES), 1)


def _head_block_diag():
    r = lax.broadcasted_iota(I32, (LANES, LANES), 0) // HEAD_DIM
    c = lax.broadcasted_iota(I32, (LANES, LANES), 1) // HEAD_DIM
    return (r == c).astype(BF16)


def _head_norm_rope(x, gain, cos, sin, bd, rope):
    y = x * x
    hi = y.astype(BF16)
    lo = (y - hi.astype(F32)).astype(BF16)
    seg = _dot(hi, bd) + _dot(lo, bd)
    xn = x * lax.rsqrt(seg * (1.0 / HEAD_DIM) + EPS) * gain
    if rope:
        first_half = (_lane_iota(x.shape[0]) % HEAD_DIM) < HEAD_DIM // 2
        partner = jnp.where(first_half, pltpu.roll(xn, LANES - HEAD_DIM // 2, 1),
                            pltpu.roll(xn, HEAD_DIM // 2, 1))
        xn = xn * cos + partner * sin
    return xn


def _prep_kernel(cb_ref, x_ref, g_ref, cos_ref, sin_ref, o_ref, *, rope):
    del cb_ref
    bd = _head_block_diag()
    cos = cos_ref[...]
    sin = sin_ref[...]
    for c in range(x_ref.shape[2] // LANES):
        sl = slice(c * LANES, (c + 1) * LANES)
        x = x_ref[0, :, sl].astype(F32)
        o_ref[0, :, sl] = _head_norm_rope(x, g_ref[0, :, sl], cos, sin, bd, rope).astype(o_ref.dtype)


def head_prep(src, gains, col_blocks, cos, sin, *, rope, ts=512):
    b, s, _ = src.shape
    n = len(col_blocks)
    ts = min(ts, s)
    cb = jnp.asarray(col_blocks, I32)

    def x_map(bi, si, ci, cb_ref):
        return (bi, si, cb_ref[ci])

    return pl.pallas_call(
        functools.partial(_prep_kernel, rope=rope),
        grid_spec=pltpu.PrefetchScalarGridSpec(
            num_scalar_prefetch=1,
            grid=(b, s // ts, n),
            in_specs=[pl.BlockSpec((1, ts, D_MODEL), x_map),
                      pl.BlockSpec((1, 1, D_MODEL), lambda bi, si, ci, cb_ref: (ci, 0, 0)),
                      pl.BlockSpec((ts, LANES), lambda bi, si, ci, cb_ref: (si, 0)),
                      pl.BlockSpec((ts, LANES), lambda bi, si, ci, cb_ref: (si, 0))],
            out_specs=pl.BlockSpec((1, ts, D_MODEL), lambda bi, si, ci, cb_ref: (bi, si, ci))),
        out_shape=jax.ShapeDtypeStruct((b, s, n * D_MODEL), BF16),
        compiler_params=_cp("parallel", "parallel", "arbitrary"),
        name="head_prep",
    )(cb, src, gains, cos, sin)


def rope_lane_tables(s):
    inv = ROPE_THETA ** (-jnp.arange(0, HEAD_DIM, 2, dtype=F32) / HEAD_DIM)
    ang = jnp.arange(s).astype(F32)[:, None] * inv[None, :]
    cos, sin = jnp.cos(ang), jnp.sin(ang)
    return jnp.tile(cos, (1, 4)), jnp.tile(jnp.concatenate([-sin, sin], axis=-1), (1, 2)), cos, sin


def _tile_gain(g, scale=1.0):
    return jnp.tile(g.astype(F32) * scale, N_HEADS).reshape(1, D_MODEL)


def _fox_cum_kernel(f_ref, b_ref, o_ref, carry_ref):
    si = pl.program_id(1)
    ts = f_ref.shape[1]

    @pl.when(si == 0)
    def _():
        carry_ref[...] = jnp.zeros_like(carry_ref)

    z = f_ref[0] + b_ref[...]
    log_f = -(jnp.maximum(-z, 0.0) + jnp.log1p(jnp.exp(-jnp.abs(z))))
    r = lax.broadcasted_iota(I32, (ts, ts), 0)
    c = lax.broadcasted_iota(I32, (ts, ts), 1)
    upper = (r <= c).astype(BF16)
    cum = _dot_f32_exact_rhs(log_f.T, upper) + carry_ref[:, 0:1]
    o_ref[0] = cum[0:N_HEADS, :] * LOG2E
    carry_ref[...] = jnp.broadcast_to(cum[:, ts - 1:ts], carry_ref.shape)


def fox_cumulative_gate(tail, b_f, *, ts=256):
    b, s, _ = tail.shape
    ts = min(ts, s)
    bias = jnp.zeros((1, LANES), F32).at[0, :N_HEADS].set(b_f.astype(F32))
    return pl.pallas_call(
        _fox_cum_kernel,
        grid=(b, s // ts),
        in_specs=[pl.BlockSpec((1, ts, LANES), lambda bi, si: (bi, si, 0)),
                  pl.BlockSpec((1, LANES), lambda bi, si: (0, 0))],
        out_specs=pl.BlockSpec((1, N_HEADS, ts), lambda bi, si: (bi, 0, si)),
        out_shape=jax.ShapeDtypeStruct((b, N_HEADS, s), F32),
        scratch_shapes=[pltpu.VMEM((LANES, LANES), F32)],
        compiler_params=_cp("parallel", "arbitrary"),
        name="fox_cumulative_gate",
    )(tail, bias)


def _flash_kernel(*refs, tq, tk, window, mode, fin, has_bias, pairs, lam_init):
    it = iter(refs)
    q_ref, k_ref, v_ref = next(it), next(it), next(it)
    kb_ref = next(it) if has_bias else None
    if fin == "diff":
        lam_ref, sg_ref = next(it), next(it)
    o_ref = next(it)
    m_sc, l_sc, acc_sc = next(it), next(it), next(it)
    s_even, s_odd = next(it), next(it)

    n_heads = 2 * pairs
    q_start = pl.program_id(2) * tq
    lane = _lane_iota(tq)
    low_half = lane < HEAD_DIM
    qh = []
    for h in range(n_heads):
        if mode == "pair":
            q = q_ref[0, :, (h // 2) * LANES:(h // 2 + 1) * LANES]
            zero = jnp.zeros_like(q)
            qh.append(jnp.where(low_half, q, zero) if h % 2 == 0 else jnp.where(low_half, zero, q))
        else:
            qh.append(q_ref[0, :, h * LANES:(h + 1) * LANES])

    m_sc[...] = jnp.full(m_sc.shape, M_INIT, F32)
    l_sc[...] = jnp.zeros(l_sc.shape, F32)
    acc_sc[...] = jnp.zeros(acc_sc.shape, F32)
    row = q_start + lax.broadcasted_iota(I32, (tq, LANES), 0)
    n_chunk = tk // LANES

    def scores(kv, dst):
        ks = pl.multiple_of(kv * tk, tk)
        kblk = k_ref[0, pl.ds(ks, tk), :]
        for h in range(n_heads):
            kk = kblk[:, (h // 2) * LANES:(h // 2 + 1) * LANES] if mode == "pair" else \
                kblk[:, (h % 2) * LANES:(h % 2 + 1) * LANES]
            s = _dot_nt(qh[h], kk)
            if has_bias:
                s = s - kb_ref[0, h // 2, kv][h % 2:h % 2 + 1, :]
            dst[h] = s

    def step(kv, masked, prefetch, cur, nxt):
        if prefetch:
            scores(kv + 1, nxt)
        ks = pl.multiple_of(kv * tk, tk)
        vblk = v_ref[0, pl.ds(ks, tk), :]
        if masked:
            masks = []
            for c in range(n_chunk):
                col = ks + c * LANES + lane
                mk = col <= row
                if window:
                    mk = mk & ((row - col) < window)
                masks.append(mk)
        for h in range(n_heads):
            vv = vblk[:, (h // 2) * LANES:(h // 2 + 1) * LANES] if mode == "pair" else vblk
            s = cur[h]
            chunks = [s[:, c * LANES:(c + 1) * LANES] for c in range(n_chunk)]
            if masked:
                chunks = [jnp.where(mk, ch, -jnp.inf) for mk, ch in zip(masks, chunks)]
            mb = functools.reduce(jnp.maximum, chunks)
            m_old = m_sc[h]
            m_new = jnp.maximum(m_old, jnp.broadcast_to(jnp.max(mb, axis=-1, keepdims=True), (tq, LANES)))
            alpha = jnp.exp2(m_old - m_new)
            ps = [jnp.exp2(ch - m_new) for ch in chunks]
            l_sc[h] = alpha * l_sc[h] + functools.reduce(jnp.add, ps)
            p = ps[0] if n_chunk == 1 else jnp.concatenate(ps, axis=1)
            acc_sc[h] = alpha * acc_sc[h] + _dot(p.astype(BF16), vv)
            m_sc[h] = m_new

    def step_by_parity(kv, masked, prefetch):
        @pl.when(kv % 2 == 0)
        def _():
            step(kv, masked, prefetch, s_even, s_odd)

        @pl.when(kv % 2 == 1)
        def _():
            step(kv, masked, prefetch, s_odd, s_even)

    def loop(lo, hi, masked):
        def body(kv, carry):
            step_by_parity(kv, masked, True)
            return carry
        lax.fori_loop(lo, hi, body, 0)

    last_blk = (q_start + (tq - 1)) // tk
    full_hi = (q_start + 1) // tk
    if window:
        first_blk = jnp.maximum(q_start - (window - 1), 0) // tk
        full_lo = jnp.maximum(q_start + (tq - 1) - window + tk, 0) // tk
        full_lo = jnp.maximum(jnp.minimum(full_lo, full_hi), first_blk)
    else:
        first_blk = full_lo = 0

    @pl.when(first_blk % 2 == 0)
    def _():
        scores(first_blk, s_even)

    @pl.when(first_blk % 2 == 1)
    def _():
        scores(first_blk, s_odd)

    if window:
        loop(first_blk, full_lo, True)
    loop(full_lo, full_hi, False)
    loop(jnp.maximum(full_hi, full_lo), last_blk, True)
    step_by_parity(last_blk, True, False)

    if fin == "diff":
        lam_rows = lam_ref[...]
        lam = (jnp.exp(jnp.sum(lam_rows[0:1] * lam_rows[1:2], axis=-1, keepdims=True))
               - jnp.exp(jnp.sum(lam_rows[2:3] * lam_rows[3:4], axis=-1, keepdims=True)) + lam_init)
    for pi in range(pairs):
        l0 = jnp.maximum(jnp.sum(l_sc[2 * pi], axis=-1, keepdims=True), TINY)
        l1 = jnp.maximum(jnp.sum(l_sc[2 * pi + 1], axis=-1, keepdims=True), TINY)
        o0 = acc_sc[2 * pi] * (1.0 / l0)
        o1 = acc_sc[2 * pi + 1] * (1.0 / l1)
        if fin == "select":
            o = jnp.where(low_half, o0, o1)
        else:
            o = o0 - lam * o1
            ms = jnp.mean(o * o, axis=-1, keepdims=True)
            o = o * lax.rsqrt(ms + EPS) * sg_ref[...] * (1.0 - lam_init)
        o_ref[0, :, pi * LANES:(pi + 1) * LANES] = o.astype(o_ref.dtype)


def flash_attention(q, k, v, *, q_off, k_off, v_off, k_div=1, mode="pair", fin="select", window=0,
                    key_bias=None, lam=None, sub_gain=None, lam_init=0.0, pairs=FLASH_PAIRS,
                    tq=FLASH_TILE, tk=FLASH_TILE):
    nb, s, _ = q.shape
    tq, tk = min(tq, s), min(tk, s)
    n_inner = N_PAIRS // pairs
    wo = pairs * LANES
    if mode == "pair":
        wq = wk = wv = wo
    else:
        wq, wk, wv = 2 * wo, 2 * LANES, LANES
    in_specs = [pl.BlockSpec((1, tq, wq), lambda b, j, i: (b, i, q_off // (wq // LANES) + j)),
                pl.BlockSpec((1, s, wk), lambda b, j, i: (b, 0, k_off // (wk // LANES) + j // k_div)),
                pl.BlockSpec((1, s, wv), lambda b, j, i: (b, 0, v_off // (wv // LANES) + j // k_div))]
    args = [q, k, v]
    if key_bias is not None:
        in_specs.append(pl.BlockSpec((1, pairs, s // tk, 2, tk), lambda b, j, i: (b, j, 0, 0, 0)))
        args.append(key_bias)
    if fin == "diff":
        in_specs += [pl.BlockSpec((4, LANES), lambda b, j, i: (0, 0)),
                     pl.BlockSpec((1, LANES), lambda b, j, i: (0, 0))]
        args += [lam, sub_gain]
    return pl.pallas_call(
        functools.partial(_flash_kernel, tq=tq, tk=tk, window=window, mode=mode, fin=fin,
                          has_bias=key_bias is not None, pairs=pairs, lam_init=lam_init),
        grid=(nb, n_inner, s // tq),
        in_specs=in_specs,
        out_specs=pl.BlockSpec((1, tq, wo), lambda b, j, i: (b, i, j)),
        out_shape=jax.ShapeDtypeStruct((nb, s, D_MODEL), BF16),
        scratch_shapes=[pltpu.VMEM((2 * pairs, tq, LANES), F32)] * 3 + [pltpu.VMEM((2 * pairs, tq, tk), F32)] * 2,
        compiler_params=_cp("parallel", "parallel", "arbitrary"),
        name="flash_" + mode + "_" + fin,
    )(*args)


DIL_SUB = LANES
DIL_UNROLL = 4


def _rows(start, size, stride):
    return pl.ds(start, size) if stride == 1 else pl.ds(start, size, stride=stride)


def _dilated_kernel(*refs, seq):
    q_refs, k_refs, v_refs = refs[0:3], refs[3:6], refs[6:9]
    o_ref = refs[9]
    qf, kf, vf, m_st, l_st, acc_st = refs[10:16]
    sub = min(DIL_SUB, seq // DIL_PAIRS[-1][1])
    lane = _lane_iota(sub)
    low_half = lane < HEAD_DIM
    for g, (window, d) in enumerate(DIL_PAIRS):
        sd = seq // d
        span = min(2 * sub, sd)
        n_res = sd // sub
        win = window // d + 1
        assert win <= span - sub + 1 or span == sd
        qf[...] = q_refs[g][0].astype(F32)
        kf[...] = k_refs[g][0].astype(F32)
        vf[...] = v_refs[g][0].astype(F32)
        low_half_kv = _lane_iota(span) < HEAD_DIM
        ones = jnp.ones((span, LANES), BF16)

        def chain(t_idx, g=g, d=d, sd=sd, span=span, n_res=n_res, win=win, low_half_kv=low_half_kv, ones=ones):
            r = t_idx // n_res
            i0 = (t_idx % n_res) * sub
            ks = jnp.minimum(jnp.maximum(i0 - sub, 0), sd - span)
            q_rows = _rows(i0 * d + r, sub, d)
            kv_rows = _rows(ks * d + r, span, d)
            q = qf[q_rows, :].astype(BF16)
            kblk = kf[kv_rows, :].astype(BF16)
            vblk = vf[kv_rows, :].astype(BF16)
            zero = jnp.zeros_like(q)
            row = i0 + lax.broadcasted_iota(I32, (sub, LANES), 0)
            masks = []
            for c in range(span // LANES):
                col = ks + c * LANES + lane
                masks.append((col <= row) & ((row - col) < win))
            ms, accs = [], []
            for h in range(2):
                own = low_half if h == 0 else jnp.logical_not(low_half)
                own_kv = low_half_kv if h == 0 else jnp.logical_not(low_half_kv)
                s = _dot_nt(jnp.where(own, q, zero), kblk)
                chunks = [jnp.where(mk, s[:, c * LANES:(c + 1) * LANES], -jnp.inf) for c, mk in enumerate(masks)]
                m = jnp.broadcast_to(jnp.max(functools.reduce(jnp.maximum, chunks), axis=-1, keepdims=True),
                                     (sub, LANES))
                ps = [jnp.exp2(ch - m) for ch in chunks]
                p = ps[0] if len(ps) == 1 else jnp.concatenate(ps, axis=1)
                accs.append(_dot(p.astype(BF16), jnp.where(own_kv, vblk, ones)))
                ms.append(m)
            m_c = jnp.where(low_half, ms[0], ms[1])
            acc_c = jnp.where(low_half, accs[0], accs[1])
            l_c = jnp.where(low_half, pltpu.roll(accs[0], HEAD_DIM, 1), pltpu.roll(accs[1], HEAD_DIM, 1))
            if g == 0:
                m_st[q_rows, :] = m_c
                l_st[q_rows, :] = l_c
                acc_st[q_rows, :] = acc_c
            else:
                m_old = m_st[q_rows, :]
                m_new = jnp.maximum(m_old, m_c)
                a, bb = jnp.exp2(m_old - m_new), jnp.exp2(m_c - m_new)
                m_st[q_rows, :] = m_new
                l_st[q_rows, :] = a * l_st[q_rows, :] + bb * l_c
                acc_st[q_rows, :] = a * acc_st[q_rows, :] + bb * acc_c

        n_chain = seq // sub
        unroll = min(DIL_UNROLL, n_chain)

        def body(it, carry, chain=chain, unroll=unroll):
            for u in range(unroll):
                chain(it * unroll + u)
            return carry
        lax.fori_loop(0, n_chain // unroll, body, 0)
    o_ref[0] = (acc_st[...] * (1.0 / jnp.maximum(l_st[...], TINY))).astype(o_ref.dtype)


def dilated_groups_attention(qk, main):
    b, s, _ = qk.shape
    ng = len(DIL_PAIRS)
    blk = lambda off: pl.BlockSpec((1, s, LANES), lambda bi, j, off=off: (bi, 0, off + j))
    in_specs = ([blk(2 * g * N_PAIRS) for g in range(ng)] + [blk((2 * g + 1) * N_PAIRS) for g in range(ng)]
                + [blk((3 * g + 2) * N_PAIRS) for g in range(ng)])
    return pl.pallas_call(
        functools.partial(_dilated_kernel, seq=s),
        grid=(b, N_PAIRS),
        in_specs=in_specs,
        out_specs=pl.BlockSpec((1, s, LANES), lambda bi, j: (bi, 0, j)),
        out_shape=jax.ShapeDtypeStruct((b, s, D_MODEL), BF16),
        scratch_shapes=[pltpu.VMEM((s, LANES), F32)] * 6,
        compiler_params=_cp("parallel", "parallel"),
        name="dilated_groups_attention",
    )(*([qk] * (2 * ng) + [main] * ng))


def _nsa_prep_kernel(q_ref, ks_ref, vs_ref, kw_ref, vw_ref, g_ref, cos_ref, sin_ref,
                     qo_ref, ks2_ref, vs2_ref, kw2_ref, vw2_ref):
    ts = q_ref.shape[1]
    bd = _head_block_diag()
    cos, sin = cos_ref[...], sin_ref[...]
    lane = _lane_iota(ts)
    low_half = lane < HEAD_DIM
    for c in range(N_PAIRS):
        sl = slice(c * LANES, (c + 1) * LANES)
        qo_ref[0, :, sl] = _head_norm_rope(q_ref[0, :, sl].astype(F32), g_ref[0:1, :], cos, sin, bd,
                                           True).astype(qo_ref.dtype)
    t = pl.program_id(1) * ts + lax.broadcasted_iota(I32, (ts, LANES), 0)
    blk_onehot = ((t // NSA_BLOCK) == (lane % HEAD_DIM)).astype(F32)
    zeros = jnp.zeros((ts, LANES), F32)

    def spread(x, fill, out_ref, c):
        xr = pltpu.roll(x, HEAD_DIM, 1)
        base = 2 * c * 2 * LANES
        out_ref[0, :, base:base + LANES] = jnp.where(low_half, x, fill).astype(out_ref.dtype)
        out_ref[0, :, base + LANES:base + 2 * LANES] = jnp.where(low_half, fill, xr).astype(out_ref.dtype)
        out_ref[0, :, base + 2 * LANES:base + 3 * LANES] = jnp.where(low_half, xr, fill).astype(out_ref.dtype)
        out_ref[0, :, base + 3 * LANES:base + 4 * LANES] = jnp.where(low_half, fill, x).astype(out_ref.dtype)

    def dup(x, out_ref, c):
        xr = pltpu.roll(x, HEAD_DIM, 1)
        out_ref[0, :, 2 * c * LANES:(2 * c + 1) * LANES] = jnp.where(low_half, x, xr).astype(out_ref.dtype)
        out_ref[0, :, (2 * c + 1) * LANES:(2 * c + 2) * LANES] = jnp.where(low_half, xr, x).astype(out_ref.dtype)

    for c in range(NSA_GROUPS // 2):
        sl = slice(c * LANES, (c + 1) * LANES)
        ks = _head_norm_rope(ks_ref[0, :, sl].astype(F32), g_ref[1:2, :], cos, sin, bd, True)
        kw = _head_norm_rope(kw_ref[0, :, sl].astype(F32), g_ref[2:3, :], cos, sin, bd, True)
        spread(ks, blk_onehot, ks2_ref, c)
        spread(kw, zeros, kw2_ref, c)
        dup(vs_ref[0, :, sl].astype(F32), vs2_ref, c)
        dup(vw_ref[0, :, sl].astype(F32), vw2_ref, c)


def nsa_prep(main, gains, cos, sin, *, ts=512):
    b, s, _ = main.shape
    ts = min(ts, s)
    gw = NSA_GROUPS * HEAD_DIM

    def kv_spec(i):
        return pl.BlockSpec((1, ts, gw), lambda bi, si: (bi, si, i))

    return pl.pallas_call(
        _nsa_prep_kernel,
        grid=(b, s // ts),
        in_specs=[pl.BlockSpec((1, ts, D_MODEL), lambda bi, si: (bi, si, 0)),
                  kv_spec(6), kv_spec(7), kv_spec(8), kv_spec(9),
                  pl.BlockSpec((8, LANES), lambda bi, si: (0, 0)),
                  pl.BlockSpec((ts, LANES), lambda bi, si: (si, 0)),
                  pl.BlockSpec((ts, LANES), lambda bi, si: (si, 0))],
        out_specs=[pl.BlockSpec((1, ts, D_MODEL), lambda bi, si: (bi, si, 0)),
                   pl.BlockSpec((1, ts, NSA_GROUPS * 2 * LANES), lambda bi, si: (bi, si, 0)),
                   pl.BlockSpec((1, ts, NSA_GROUPS * LANES), lambda bi, si: (bi, si, 0)),
                   pl.BlockSpec((1, ts, NSA_GROUPS * 2 * LANES), lambda bi, si: (bi, si, 0)),
                   pl.BlockSpec((1, ts, NSA_GROUPS * LANES), lambda bi, si: (bi, si, 0))],
        out_shape=[jax.ShapeDtypeStruct((b, s, D_MODEL), BF16),
                   jax.ShapeDtypeStruct((b, s, NSA_GROUPS * 2 * LANES), BF16),
                   jax.ShapeDtypeStruct((b, s, NSA_GROUPS * LANES), BF16),
                   jax.ShapeDtypeStruct((b, s, NSA_GROUPS * 2 * LANES), BF16),
                   jax.ShapeDtypeStruct((b, s, NSA_GROUPS * LANES), BF16)],
        compiler_params=_cp("parallel", "parallel"),
        name="nsa_prep",
    )(main, main, main, main, main, gains, cos, sin)


def _nsa_compress_kernel(x_ref, pos_ref, w1_ref, w2_ref, g_ref, cos_ref, sin_ref, o_ref, *, is_key):
    x = (x_ref[...].astype(F32) + pos_ref[...]).astype(BF16)
    hid = _silu(_dot(x, w1_ref[...]))
    y = _dot(hid.astype(BF16), w2_ref[...])
    if is_key:
        ms = jnp.mean(y * y, axis=-1, keepdims=True)
        y = y * lax.rsqrt(ms + EPS) * g_ref[...]
        r = lax.broadcasted_iota(I32, (HEAD_DIM, HEAD_DIM), 0)
        c = lax.broadcasted_iota(I32, (HEAD_DIM, HEAD_DIM), 1)
        swap = (((r + HEAD_DIM // 2) % HEAD_DIM) == c).astype(BF16)
        y = y * cos_ref[...] + _dot_f32_exact_rhs(y, swap) * sin_ref[...]
    o_ref[...] = y


def nsa_compress(x, pos, w1, w2, gain, cos_blk, sin_blk, *, is_key):
    rows, k = x.shape
    nb = cos_blk.shape[0]
    hid = w1.shape[1]
    return pl.pallas_call(
        functools.partial(_nsa_compress_kernel, is_key=is_key),
        grid=(rows // nb,),
        in_specs=[pl.BlockSpec((nb, k), lambda i: (i, 0)),
                  pl.BlockSpec((1, k), lambda i: (0, 0)),
                  pl.BlockSpec((k, hid), lambda i: (0, 0)),
                  pl.BlockSpec((hid, HEAD_DIM), lambda i: (0, 0)),
                  pl.BlockSpec((1, HEAD_DIM), lambda i: (0, 0)),
                  pl.BlockSpec((nb, HEAD_DIM), lambda i: (0, 0)),
                  pl.BlockSpec((nb, HEAD_DIM), lambda i: (0, 0))],
        out_specs=pl.BlockSpec((nb, HEAD_DIM), lambda i: (i, 0)),
        out_shape=jax.ShapeDtypeStruct((rows, HEAD_DIM), F32),
        compiler_params=_cp("parallel"),
        name="nsa_compress",
    )(x, pos, w1, w2, gain, cos_blk, sin_blk)


def _nsa_cmp_kernel(q_ref, kc_ref, vc_ref, o_ref, qa_ref, *, n_sel):
    tq = q_ref.shape[1]
    lane = _lane_iota(tq)
    low_half = lane < HEAD_DIM
    blk = lane % HEAD_DIM
    t = pl.program_id(2) * tq + lax.broadcasted_iota(I32, (tq, LANES), 0)
    cmask = (blk + 1) * NSA_BLOCK <= t + 1
    kc = kc_ref[0, 0]
    vc = vc_ref[0, 0]
    imp = jnp.zeros((tq, LANES), F32)
    qblk = [q_ref[0, :, 0:LANES], q_ref[0, :, LANES:2 * LANES]]
    zero = jnp.zeros_like(qblk[0])
    outs = [None, None]
    for p in range(NSA_HPG):
        in_low = (p % 2) == 0
        qb = qblk[p // 2]
        qm = jnp.where(low_half, qb, zero) if in_low else jnp.where(low_half, zero, qb)
        s = jnp.where(cmask, _dot_nt(qm, kc), NEG_INF)
        m = jnp.max(s, axis=-1, keepdims=True)
        e = jnp.where(cmask, jnp.exp2(s - m), 0.0)
        den = jnp.maximum(0.5 * jnp.sum(e, axis=-1, keepdims=True), TINY)
        pc = e / den
        imp = imp + pc
        o = _dot(pc.astype(BF16), vc)
        prev = outs[p // 2]
        outs[p // 2] = o if prev is None else jnp.where(low_half, prev, o)
    o_ref[0, :, 0:LANES] = outs[0].astype(o_ref.dtype)
    o_ref[0, :, LANES:2 * LANES] = outs[1].astype(o_ref.dtype)

    cur = t // NSA_BLOCK
    forced = (blk == 0) | (blk == cur) | (blk == cur - 1)
    x = jnp.where(blk > cur, -1.0, jnp.where(forced, NSA_HPG + 1.0, imp))
    blk_f = blk.astype(F32)
    selected = jnp.zeros((tq, LANES), jnp.bool_)
    for _ in range(n_sel):
        mx = jnp.max(x, axis=-1, keepdims=True)
        first = jnp.min(jnp.where(x == mx, blk_f, float(LANES)), axis=-1, keepdims=True)
        hit = blk_f == first
        selected = selected | hit
        x = jnp.where(hit, -2.0, x)
    sel_bias = jnp.where(selected, 0.0, NEG_INF).astype(qa_ref.dtype)
    for p in range(NSA_HPG):
        qb = qblk[p // 2]
        qa = jnp.where(low_half, qb, sel_bias) if p % 2 == 0 else jnp.where(low_half, sel_bias, qb)
        qa_ref[0, :, p * LANES:(p + 1) * LANES] = qa


def nsa_compressed_attention(q, kc2, vc2, *, n_sel, tq=1024):
    b, s, _ = q.shape
    tq = min(tq, s)
    gq = NSA_HPG * HEAD_DIM
    return pl.pallas_call(
        functools.partial(_nsa_cmp_kernel, n_sel=n_sel),
        grid=(b, NSA_GROUPS, s // tq),
        in_specs=[pl.BlockSpec((1, tq, gq), lambda bi, g, i: (bi, i, g)),
                  pl.BlockSpec((1, 1, LANES, LANES), lambda bi, g, i: (bi, g, 0, 0)),
                  pl.BlockSpec((1, 1, LANES, LANES), lambda bi, g, i: (bi, g, 0, 0))],
        out_specs=[pl.BlockSpec((1, tq, gq), lambda bi, g, i: (bi, i, g)),
                   pl.BlockSpec((1, tq, 2 * gq), lambda bi, g, i: (bi, i, g))],
        out_shape=[jax.ShapeDtypeStruct((b, s, D_MODEL), BF16),
                   jax.ShapeDtypeStruct((b, s, 2 * D_MODEL), BF16)],
        compiler_params=_cp("parallel", "parallel", "parallel"),
        name="nsa_compressed_attention",
    )(q, kc2, vc2)


def _route(logits_t, rb):
    scores = _sigmoid(logits_t)
    sel = scores + rb
    rows = [sel[i:i + 1, :] for i in range(N_EXPERTS)]
    srows = [scores[i:i + 1, :] for i in range(N_EXPERTS)]
    best = grp = None
    for g in range(N_EXPERT_GROUPS):
        a, b, c, d = rows[4 * g:4 * g + 4]
        hi1, lo1, hi2, lo2 = jnp.maximum(a, b), jnp.minimum(a, b), jnp.maximum(c, d), jnp.minimum(c, d)
        gs = jnp.maximum(hi1, hi2) + jnp.maximum(jnp.minimum(hi1, hi2), jnp.maximum(lo1, lo2))
        if g == 0:
            best, grp = gs, jnp.zeros(gs.shape, I32)
        else:
            better = gs > best
            grp = jnp.where(better, g, grp)
            best = jnp.where(better, gs, best)

    def pick(vals, i):
        out = vals[i]
        for g in range(1, N_EXPERT_GROUPS):
            out = jnp.where(grp == g, vals[4 * g + i], out)
        return out

    v = [pick(rows, i) for i in range(EXPERTS_PER_GROUP)]
    w = [pick(srows, i) for i in range(EXPERTS_PER_GROUP)]
    l1, b1, w1 = jnp.zeros(grp.shape, I32), v[0], w[0]
    for i in range(1, EXPERTS_PER_GROUP):
        better = v[i] > b1
        l1 = jnp.where(better, i, l1)
        b1 = jnp.where(better, v[i], b1)
        w1 = jnp.where(better, w[i], w1)
    have = jnp.zeros(grp.shape, jnp.bool_)
    l2, b2, w2 = jnp.zeros(grp.shape, I32), jnp.zeros_like(b1), jnp.zeros_like(w1)
    for i in range(EXPERTS_PER_GROUP):
        valid = l1 != i
        better = valid & (jnp.logical_not(have) | (v[i] > b2))
        l2 = jnp.where(better, i, l2)
        b2 = jnp.where(better, v[i], b2)
        w2 = jnp.where(better, w[i], w2)
        have = have | valid
    wsum = w1 + w2
    return (grp * EXPERTS_PER_GROUP + l1, grp * EXPERTS_PER_GROUP + l2), (w1 / wsum, w2 / wsum)


def _outproj_router_kernel(*refs, mix):
    it = iter(refs)
    if mix == "plain":
        x = next(it)[0]
    elif mix == "nsa":
        o_refs = [next(it), next(it), next(it)]
        gl = _sigmoid(next(it)[0])
        g_hi = gl.astype(BF16)
        g_lo = (gl - g_hi.astype(F32)).astype(BF16)
        r = lax.broadcasted_iota(I32, (LANES, D_MODEL), 0)
        c = lax.broadcasted_iota(I32, (LANES, D_MODEL), 1) // HEAD_DIM
        x = None
        for i in range(3):
            expand = (r == c + i * N_HEADS).astype(BF16)
            term = (_dot(g_hi, expand) + _dot(g_lo, expand)) * o_refs[i][0].astype(F32)
            x = term if x is None else x + term
        x = x.astype(BF16)
    else:
        raise ValueError(mix)
    h_ref, mod_ref, w_ref, g2_ref, rw_ref, rb_ref = (next(it) for _ in range(6))
    ho_ref, u_ref, e_ref, wt_ref = (next(it) for _ in range(4))
    mod = mod_ref[0]
    h_new = h_ref[0] + mod[2:3, :] * _dot(x, w_ref[...])
    ho_ref[0] = h_new
    u = _modulated_norm(h_new, g2_ref[...], mod, 3, 4)
    _store_token_tiles(u_ref, u)
    logits_t = _dot_f32(rw_ref[...], u, dot=_dot_nt)
    eidx, wts = _route(logits_t, rb_ref[:, 0:1])
    for k in range(TOP_K):
        e_ref[0, k:k + 1, :] = eidx[k]
        wt_ref[0, k:k + 1, :] = wts[k]


def outproj_router(attn_inputs, h, mod, w_out, gain2, router_wt, router_b, *, mix, tm=512):
    b, s, d = h.shape
    tm = min(tm, s)
    row_spec = pl.BlockSpec((1, tm, d), lambda bi, si: (bi, si, 0))
    in_specs, args = [], []
    for a in attn_inputs:
        in_specs.append(pl.BlockSpec((1, tm, a.shape[2]), lambda bi, si: (bi, si, 0)))
        args.append(a)
    in_specs += [row_spec,
                 pl.BlockSpec((1, 6, d), lambda bi, si: (bi, 0, 0)),
                 pl.BlockSpec((d, d), lambda bi, si: (0, 0)),
                 pl.BlockSpec((1, d), lambda bi, si: (0, 0)),
                 pl.BlockSpec((N_EXPERTS, d), lambda bi, si: (0, 0)),
                 pl.BlockSpec((N_EXPERTS, LANES), lambda bi, si: (0, 0))]
    args += [h, mod, w_out, gain2.reshape(1, d), router_wt, router_b]
    return pl.pallas_call(
        functools.partial(_outproj_router_kernel, mix=mix),
        grid=(b, s // tm),
        in_specs=in_specs,
        out_specs=[row_spec, pl.BlockSpec((tm * ROW_TILE, LANES), lambda bi, si: (bi * (s // tm) + si, 0)),
                   pl.BlockSpec((1, TOP_K, tm), lambda bi, si: (bi, 0, si)),
                   pl.BlockSpec((1, TOP_K, tm), lambda bi, si: (bi, 0, si))],
        out_shape=[jax.ShapeDtypeStruct((b, s, d), F32), jax.ShapeDtypeStruct((b * s * ROW_TILE, LANES), F32),
                   jax.ShapeDtypeStruct((b, TOP_K, s), I32), jax.ShapeDtypeStruct((b, TOP_K, s), F32)],
        compiler_params=_cp("parallel", "parallel"),
        name="outproj_router_" + mix,
    )(*args)


ROW_TILE = D_MODEL // LANES
FFN_PIECES = 6
FFN_SLOTS = 3


def _store_token_tiles(ref, x):
    rows = x.shape[0]
    for c in range(ROW_TILE):
        ref[pl.ds(c, rows, stride=ROW_TILE), :] = x[:, c * LANES:(c + 1) * LANES]


def _load_token_tiles(ref, rows):
    return jnp.concatenate([ref[pl.ds(c, rows, stride=ROW_TILE), :] for c in range(ROW_TILE)], axis=1)


def _tile_copy(src_hbm, index, dst, r, sem):
    start = pl.multiple_of(index * ROW_TILE, ROW_TILE)
    return pltpu.make_async_copy(src_hbm.at[pl.ds(start, ROW_TILE), :], dst.at[pl.ds(r * ROW_TILE, ROW_TILE), :], sem)


def _gather_tiles_loop(src_hbm, index_of, dst, n_rows, sem):
    def one(r, carry):
        _tile_copy(src_hbm, index_of(r), dst, r, sem).start()
        return carry
    lax.fori_loop(0, n_rows, one, 0)


def _wait_tiles(src_hbm, dst, sem):
    pltpu.make_async_copy(src_hbm.at[pl.ds(0, dst.shape[0]), :], dst, sem).wait()


def _moe_ffn_kernel(be_ref, tok_cur, tok_next, tok_ahead, u_hbm, wg_ref, wu_ref, wd_ref, y_ref, xbuf, wg_sc, wu_sc,
                    wd_sc, sem):
    i = pl.program_id(0)
    n = pl.num_programs(0)
    rows = MOE_ROWS
    slot = i % FFN_SLOTS
    nxt = (i + 2) % FFN_SLOTS

    @pl.when(i == 0)
    def _():
        _gather_tiles_loop(u_hbm, lambda r: tok_cur[r], xbuf.at[0], rows, sem.at[0])
        _gather_tiles_loop(u_hbm, lambda r: tok_next[r], xbuf.at[1], rows, sem.at[1])

    @pl.when((i == 0) | (be_ref[i] != be_ref[jnp.maximum(i - 1, 0)]))
    def _():
        wg_sc[...] = wg_ref[0, 0].astype(BF16)
        wu_sc[...] = wu_ref[0, 0].astype(BF16)
        wd_sc[...] = wd_ref[0, 0].astype(BF16)

    _wait_tiles(u_hbm, xbuf.at[slot], sem.at[slot])

    per = rows // FFN_PIECES + 1
    issued = [0]

    def issue_some():
        lo, hi = issued[0], min(issued[0] + per, rows)
        for r in range(lo, hi):
            _tile_copy(u_hbm, tok_ahead[r], xbuf.at[nxt], r, sem.at[nxt]).start(priority=r % 2)
        issued[0] = hi

    x = _load_token_tiles(xbuf.at[slot], rows).astype(BF16)
    g = _dot(x, wg_sc[...])
    issue_some()
    u = _dot(x, wu_sc[...])
    issue_some()
    hid = (_silu(g) * u).astype(BF16)
    d = wd_sc.shape[1]
    n_out = FFN_PIECES - 2
    w = d // n_out
    for c in range(n_out):
        y = _dot(hid, wd_sc[:, c * w:(c + 1) * w])
        for j in range(w // LANES):
            y_ref[pl.ds(c * (w // LANES) + j, rows, stride=ROW_TILE), :] = y[:, j * LANES:(j + 1) * LANES]
        issue_some()
    assert issued[0] == rows

    @pl.when(i == n - 1)
    def _():
        for ahead in (1, 2):
            s = (i + ahead) % FFN_SLOTS
            _wait_tiles(u_hbm, xbuf.at[s], sem.at[s])


def moe_expert_ffn(u_tiles, row_token, blk_expert, w_gate, w_up, w_down, layer):
    r_total = row_token.shape[0]
    nblk = r_total // MOE_ROWS
    d, ff = w_gate.shape[2], w_gate.shape[3]
    smem_blk = functools.partial(pl.BlockSpec, (MOE_ROWS,), memory_space=pltpu.SMEM)
    return pl.pallas_call(
        _moe_ffn_kernel,
        grid_spec=pltpu.PrefetchScalarGridSpec(
            num_scalar_prefetch=1,
            grid=(nblk,),
            in_specs=[smem_blk(lambda i, be: (i,)),
                      smem_blk(lambda i, be: (jnp.minimum(i + 1, nblk - 1),)),
                      smem_blk(lambda i, be: (jnp.minimum(i + 2, nblk - 1),)),
                      pl.BlockSpec(memory_space=pl.ANY),
                      pl.BlockSpec((1, 1, d, ff), lambda i, be: (layer, be[i], 0, 0)),
                      pl.BlockSpec((1, 1, d, ff), lambda i, be: (layer, be[i], 0, 0)),
                      pl.BlockSpec((1, 1, ff, d), lambda i, be: (layer, be[i], 0, 0))],
            out_specs=pl.BlockSpec((MOE_ROWS * ROW_TILE, LANES), lambda i, be: (i, 0)),
            scratch_shapes=[pltpu.VMEM((FFN_SLOTS, MOE_ROWS * ROW_TILE, LANES), F32), pltpu.VMEM((d, ff), BF16),
                            pltpu.VMEM((d, ff), BF16), pltpu.VMEM((ff, d), BF16),
                            pltpu.SemaphoreType.DMA((FFN_SLOTS,))]),
        out_shape=jax.ShapeDtypeStruct((r_total * ROW_TILE, LANES), F32),
        compiler_params=_cp("arbitrary"),
        name="moe_expert_ffn",
    )(blk_expert, row_token, row_token, row_token, u_tiles, w_gate, w_up, w_down)


def _moe_combine_kernel(d_cur, d_next, ys_hbm, h_ref, mod_ref, w_ref, o_ref, buf, sem):
    i = pl.program_id(0)
    n = pl.num_programs(0)
    tm = h_ref.shape[0]
    slot = i % 2
    nxt = 1 - slot

    @pl.when(i == 0)
    def _():
        for k in range(TOP_K):
            _gather_tiles_loop(ys_hbm, lambda r, k=k: d_cur[k, r], buf.at[0, k], tm, sem.at[0])

    for r in range(tm):
        for k in range(TOP_K):
            _tile_copy(ys_hbm, d_next[k, r], buf.at[nxt, k], r, sem.at[nxt]).start(priority=k)

    for k in range(TOP_K):
        _wait_tiles(ys_hbm, buf.at[slot, k], sem.at[slot])
    w = w_ref[...]
    y = w[:, 0:1] * _load_token_tiles(buf.at[slot, 0], tm) + w[:, 1:2] * _load_token_tiles(buf.at[slot, 1], tm)
    o_ref[...] = h_ref[...] + mod_ref[0, 5:6, :] * y

    @pl.when(i == n - 1)
    def _():
        for k in range(TOP_K):
            _wait_tiles(ys_hbm, buf.at[nxt, k], sem.at[nxt])


def moe_combine(ys, dest, h, mod, wts, *, tm=256):
    b, s, d = h.shape
    t = b * s
    tm = min(tm, s)
    per_b = s // tm
    n = t // tm
    smem_blk = functools.partial(pl.BlockSpec, (TOP_K, tm), memory_space=pltpu.SMEM)
    out = pl.pallas_call(
        _moe_combine_kernel,
        grid=(n,),
        in_specs=[smem_blk(lambda i: (0, i)),
                  smem_blk(lambda i: (0, jnp.minimum(i + 1, n - 1))),
                  pl.BlockSpec(memory_space=pl.ANY),
                  pl.BlockSpec((tm, d), lambda i: (i, 0)),
                  pl.BlockSpec((1, 6, d), lambda i: (i // per_b, 0, 0)),
                  pl.BlockSpec((tm, TOP_K), lambda i: (i, 0))],
        out_specs=pl.BlockSpec((tm, d), lambda i: (i, 0)),
        out_shape=jax.ShapeDtypeStruct((t, d), F32),
        scratch_shapes=[pltpu.VMEM((2, TOP_K, tm * ROW_TILE, LANES), F32), pltpu.SemaphoreType.DMA((2,))],
        compiler_params=_cp("arbitrary"),
        name="moe_combine",
    )(dest, dest, ys, h.reshape(t, d), mod, wts)
    return out.reshape(b, s, d)


def moe_layer(h, u, eidx, wts, mod, w_gate, w_up, w_down, layer):
    b, s, d = h.shape
    t = b * s
    e_flat = eidx.transpose(0, 2, 1).reshape(-1)
    n_pairs = t * TOP_K
    r_total = n_pairs + N_EXPERTS * MOE_ROWS
    nblk = r_total // MOE_ROWS
    onehot = (e_flat[:, None] == jnp.arange(N_EXPERTS, dtype=I32)[None, :]).astype(I32)
    csum = jnp.cumsum(onehot, axis=0)
    counts = csum[-1]
    rank = jnp.take_along_axis(csum, e_flat[:, None], axis=1)[:, 0] - 1
    padded = (counts + MOE_ROWS - 1) // MOE_ROWS * MOE_ROWS
    pad_end = jnp.cumsum(padded)
    pad_start = pad_end - padded
    dest = (pad_start[e_flat] + rank).astype(I32)
    row_token = jnp.zeros((r_total,), I32).at[dest].set(jnp.arange(n_pairs, dtype=I32) // TOP_K)
    blk_expert = jnp.minimum(jnp.searchsorted(pad_end, jnp.arange(nblk, dtype=I32) * MOE_ROWS, side="right"),
                             N_EXPERTS - 1).astype(I32)
    ys = moe_expert_ffn(u, row_token, blk_expert, w_gate, w_up, w_down, layer)
    dest2 = dest.reshape(t, TOP_K).T
    w_tok = wts.transpose(0, 2, 1).reshape(t, TOP_K)
    return moe_combine(ys, dest2, h, mod, w_tok)


def _pad_cols(w, n):
    return jnp.pad(w, ((0, 0), (0, n - w.shape[1])))


def fox_attention(h, mod, gain, w_in, b_f, q_gain, k_gain, cos, sin):
    b, s, _ = h.shape
    n_main = 3 * D_MODEL
    main, tail = norm_proj(h, mod, gain, w_in[:, :n_main].astype(BF16),
                           _pad_cols(w_in[:, n_main:], LANES).astype(BF16))
    gains = jnp.stack([_tile_gain(q_gain, Q_SCALE), _tile_gain(k_gain)])
    qk = head_prep(main, gains, (0, 1), cos, sin, rope=False)
    tk = min(FLASH_TILE, s)
    cum = fox_cumulative_gate(tail, b_f)
    key_bias = cum.reshape(b, N_PAIRS, 2, s // tk, tk).transpose(0, 1, 3, 2, 4)
    return flash_attention(qk, qk, main, q_off=0, k_off=N_PAIRS, v_off=2 * N_PAIRS, key_bias=key_bias, tk=tk)


def diff_attention(h, mod, gain, w_in, q_gain, k_gain, lambdas, sub_gain, layer_idx, cos, sin):
    b, s, _ = h.shape
    main = norm_proj(h, mod, gain, w_in.astype(BF16))
    gains = jnp.stack([_tile_gain(q_gain, Q_SCALE), _tile_gain(k_gain)])
    qk = head_prep(main, gains, (0, 1), cos, sin, rope=True)
    lam_init = 0.8 - 0.6 * math.exp(-0.3 * layer_idx)
    lam = jnp.pad(lambdas.astype(F32), ((0, 0), (0, LANES - HEAD_DIM)))
    return flash_attention(qk, qk, main, q_off=0, k_off=N_PAIRS, v_off=2 * N_PAIRS, fin="diff", lam=lam,
                           sub_gain=sub_gain.astype(F32).reshape(1, LANES), lam_init=lam_init)


def dilated_attention(h, mod, gain, w_in, q_gain, k_gain, cos, sin):
    b, s, _ = h.shape
    ng = len(DIL_PAIRS)
    main = norm_proj(h, mod, gain, w_in.astype(BF16), tn=3 * D_MODEL)
    gq, gk = _tile_gain(q_gain, Q_SCALE), _tile_gain(k_gain)
    qk = head_prep(main, jnp.stack([gq, gk] * ng), tuple(3 * g + j for g in range(ng) for j in range(2)),
                   cos, sin, rope=True)
    return dilated_groups_attention(qk, main)


def nsa_attention(h, mod, gain, w_in, q_gain, k_gain, cmp_pos, cmp_w1, cmp_w2, cos, sin, cos_h, sin_h):
    b, s, _ = h.shape
    nblk = s // NSA_BLOCK
    n_main = D_MODEL + 6 * NSA_GROUPS * HEAD_DIM
    main, tail = norm_proj(h, mod, gain, w_in[:, :n_main].astype(BF16),
                           _pad_cols(w_in[:, n_main:], LANES).astype(BF16))
    gains = jnp.zeros((8, LANES), F32)
    gains = gains.at[0].set(jnp.tile(q_gain.astype(F32) * Q_SCALE, 2))
    gains = gains.at[1].set(jnp.tile(k_gain[1].astype(F32), 2)).at[2].set(jnp.tile(k_gain[2].astype(F32), 2))
    q, ks2, vs2, kw2, vw2 = nsa_prep(main, gains, cos, sin)

    def to_block_rows(col0):
        x = main[:, :, col0:col0 + NSA_GROUPS * HEAD_DIM].reshape(b, nblk, NSA_BLOCK, NSA_GROUPS, HEAD_DIM)
        return x.transpose(0, 3, 1, 2, 4).reshape(b * NSA_GROUPS * nblk, NSA_BLOCK * HEAD_DIM)

    cos_b = jnp.tile(cos_h[NSA_BLOCK - 1::NSA_BLOCK], (1, 2))
    sin_b = jnp.concatenate([-sin_h[NSA_BLOCK - 1::NSA_BLOCK], sin_h[NSA_BLOCK - 1::NSA_BLOCK]], axis=-1)
    kc = nsa_compress(to_block_rows(D_MODEL), cmp_pos[0].reshape(1, -1), cmp_w1[0].astype(BF16),
                      cmp_w2[0].astype(BF16), k_gain[0].astype(F32).reshape(1, HEAD_DIM), cos_b, sin_b, is_key=True)
    vc = nsa_compress(to_block_rows(D_MODEL + NSA_GROUPS * HEAD_DIM), cmp_pos[1].reshape(1, -1),
                      cmp_w1[1].astype(BF16), cmp_w2[1].astype(BF16),
                      k_gain[0].astype(F32).reshape(1, HEAD_DIM), cos_b, sin_b, is_key=False)
    kc = kc.reshape(b, NSA_GROUPS, nblk, HEAD_DIM)
    vc = vc.reshape(b, NSA_GROUPS, nblk, HEAD_DIM)
    pad_rows = ((0, 0), (0, 0), (0, HEAD_DIM - nblk), (0, 0))
    kc = jnp.pad(kc, pad_rows)
    vc = jnp.pad(vc, pad_rows)
    kc2 = jnp.tile(kc, (1, 1, 2, 2)).astype(BF16)
    vc2 = jnp.concatenate([jnp.tile(vc, (1, 1, 1, 2)), jnp.zeros_like(jnp.tile(vc, (1, 1, 1, 2)))],
                          axis=2).astype(BF16)
    o_cmp, q_aug = nsa_compressed_attention(q, kc2, vc2, n_sel=min(NSA_TOPN, nblk))
    common = dict(q_off=0, k_off=0, v_off=0, k_div=NSA_HPG // (2 * FLASH_PAIRS), mode="aug")
    o_sel = flash_attention(q_aug, ks2, vs2, **common)
    o_win = flash_attention(q_aug, kw2, vw2, window=NSA_WINDOW, **common)
    return [o_cmp, o_sel, o_win, tail]


def kernel(x, c, fox_w_in, fox_b_f, fox_q_gain, fox_k_gain, fox_w_out, nsa_w_in, nsa_q_gain, nsa_k_gain, nsa_cmp_pos, nsa_cmp_w1, nsa_cmp_w2, nsa_w_out, dil_w_in, dil_q_gain, dil_k_gain, dil_w_out, diff_w_in, diff_q_gain, diff_k_gain, diff_lambda, diff_sub_gain, diff_w_out, norm_gain, ada_w, ada_b, router_w, router_b, moe_w_gate, moe_w_up, moe_w_down):
    b, s, d = x.shape
    depth = norm_gain.shape[0]
    cos, sin, cos_h, sin_h = rope_lane_tables(s)
    mods = ada_modulation(c, ada_w, ada_b).reshape(depth, b, 6, d)
    router_wt = router_w.T.astype(F32)
    router_bb = jnp.broadcast_to(router_b.astype(F32)[:, None], (N_EXPERTS, LANES))
    h = x
    for i in range(depth):
        mod = mods[i]
        kind, j = i % 4, i // 4
        g1 = norm_gain[i, 0]
        if kind == 0:
            attn = [fox_attention(h, mod, g1, fox_w_in[j], fox_b_f[j], fox_q_gain[j], fox_k_gain[j], cos, sin)]
            w_out, mix = fox_w_out[j], "plain"
        elif kind == 1:
            attn = nsa_attention(h, mod, g1, nsa_w_in[j], nsa_q_gain[j], nsa_k_gain[j], nsa_cmp_pos[j],
                                 nsa_cmp_w1[j], nsa_cmp_w2[j], cos, sin, cos_h, sin_h)
            w_out, mix = nsa_w_out[j], "nsa"
        elif kind == 2:
            attn = [dilated_attention(h, mod, g1, dil_w_in[j], dil_q_gain[j], dil_k_gain[j], cos, sin)]
            w_out, mix = dil_w_out[j], "plain"
        else:
            attn = [diff_attention(h, mod, g1, diff_w_in[j], diff_q_gain[j], diff_k_gain[j], diff_lambda[j],
                                   diff_sub_gain[j], i, cos, sin)]
            w_out, mix = diff_w_out[j], "plain"
        h, u, eidx, wts = outproj_router(attn, h, mod, w_out.astype(BF16), norm_gain[i, 1], router_wt,
                                         router_bb, mix=mix)
        h = moe_layer(h, u, eidx, wts, mod, moe_w_gate, moe_w_up, moe_w_down, i)
    return h
```

```python
import functools
import math

import jax
import jax.numpy as jnp
from jax import lax
from jax.experimental import pallas as pl
from jax.experimental.pallas import tpu as pltpu

F32 = jnp.float32
BF16 = jnp.bfloat16
I32 = jnp.int32

D_MODEL = 1024
HEAD_DIM = 64
LANES = 128
N_HEADS = D_MODEL // HEAD_DIM
N_PAIRS = D_MODEL // LANES
ROPE_THETA = 10000.0
EPS = 1e-6
NEG_INF = -1e30
TINY = 1e-30
M_INIT = -1e29
LOG2E = 1.4426950408889634
Q_SCALE = HEAD_DIM ** -0.5 * LOG2E

NSA_GROUPS = 4
NSA_HPG = N_HEADS // NSA_GROUPS
NSA_BLOCK = 64
NSA_TOPN = 16
NSA_WINDOW = 512
DIL_PAIRS = ((128, 1), (512, 4), (2048, 16))

N_EXPERTS = 16
N_EXPERT_GROUPS = 4
EXPERTS_PER_GROUP = 4
TOP_K = 2
EXPERT_FF = 512
MOE_ROWS = 256
FLASH_TILE = 512
FLASH_PAIRS = 2

VMEM_LIMIT = 52 * 1024 * 1024


def _cp(*sem, vmem=VMEM_LIMIT):
    return pltpu.CompilerParams(dimension_semantics=sem, vmem_limit_bytes=vmem)


def _split3(a):
    hi = a.astype(BF16)
    r1 = a - hi.astype(F32)
    mid = r1.astype(BF16)
    lo = (r1 - mid.astype(F32)).astype(BF16)
    return hi, mid, lo


def _dot(a, b):
    return jnp.dot(a, b, preferred_element_type=F32)


def _dot_nt(a, b):
    return lax.dot_general(a, b, (((1,), (1,)), ((), ())), preferred_element_type=F32)


def _dot_f32(a, b, dot=_dot):
    ah, am, al = _split3(a)
    bh, bm, bl = _split3(b)
    return (dot(ah, bh) + (dot(ah, bm) + dot(am, bh))
            + (dot(ah, bl) + dot(al, bh) + dot(am, bm)))


def _dot_f32_exact_rhs(a, b_bf16):
    ah, am, al = _split3(a)
    return _dot(ah, b_bf16) + _dot(am, b_bf16) + _dot(al, b_bf16)


def _sigmoid(x):
    return 1.0 / (1.0 + jnp.exp(-x))


def _silu(x):
    return x * _sigmoid(x)


def _ada_kernel(c_ref, w_ref, b_ref, o_ref):
    c = c_ref[...]
    o_ref[0] = _dot_f32(_silu(c), w_ref[0]) + b_ref[0]


def ada_modulation(c, ada_w, ada_b):
    depth, d, n = ada_w.shape
    b = c.shape[0]
    tn = 1024
    return pl.pallas_call(
        _ada_kernel,
        grid=(depth, n // tn),
        in_specs=[pl.BlockSpec((b, d), lambda i, j: (0, 0)),
                  pl.BlockSpec((1, d, tn), lambda i, j: (i, 0, j)),
                  pl.BlockSpec((1, 1, tn), lambda i, j: (i, 0, j))],
        out_specs=pl.BlockSpec((1, b, tn), lambda i, j: (i, 0, j)),
        out_shape=jax.ShapeDtypeStruct((depth, b, n), F32),
        compiler_params=_cp("parallel", "parallel"),
        name="ada_modulation",
    )(c, ada_w, ada_b.reshape(depth, 1, n))


def _modulated_norm(x, gain, mod, shift_row, scale_row):
    ms = jnp.mean(x * x, axis=-1, keepdims=True)
    y = x * lax.rsqrt(ms + EPS) * gain
    return y * (1.0 + mod[scale_row:scale_row + 1, :]) + mod[shift_row:shift_row + 1, :]


def _norm_proj_kernel(h_ref, mod_ref, g_ref, w_ref, *rest, has_tail):
    u = _modulated_norm(h_ref[0], g_ref[...], mod_ref[0], 0, 1).astype(BF16)
    if has_tail:
        wt_ref, main_ref, tail_ref = rest
        tail_ref[0] = _dot(u, wt_ref[...])
    else:
        (main_ref,) = rest
    main_ref[0] = _dot(u, w_ref[...]).astype(main_ref.dtype)


def norm_proj(h, mod, gain, w_main, w_tail=None, *, tn=None, tm=512):
    b, s, d = h.shape
    n = w_main.shape[1]
    tn = tn or n
    tm = min(tm, s)
    in_specs = [pl.BlockSpec((1, tm, d), lambda j, bi, si: (bi, si, 0)),
                pl.BlockSpec((1, 6, d), lambda j, bi, si: (bi, 0, 0)),
                pl.BlockSpec((1, d), lambda j, bi, si: (0, 0)),
                pl.BlockSpec((d, tn), lambda j, bi, si: (0, j))]
    out_specs = [pl.BlockSpec((1, tm, tn), lambda j, bi, si: (bi, si, j))]
    out_shape = [jax.ShapeDtypeStruct((b, s, n), BF16)]
    args = [h, mod, gain.reshape(1, d), w_main]
    if w_tail is not None:
        in_specs.append(pl.BlockSpec((d, LANES), lambda j, bi, si: (0, 0)))
        out_specs.append(pl.BlockSpec((1, tm, LANES), lambda j, bi, si: (bi, si, 0)))
        out_shape.append(jax.ShapeDtypeStruct((b, s, LANES), F32))
        args.append(w_tail)
    outs = pl.pallas_call(
        functools.partial(_norm_proj_kernel, has_tail=w_tail is not None),
        grid=(n // tn, b, s // tm),
        in_specs=in_specs, out_specs=out_specs, out_shape=out_shape,
        compiler_params=_cp("parallel", "parallel", "parallel"),
        name="norm_proj",
    )(*args)
    return outs if w_tail is not None else outs[0]


def _lane_iota(rows):
    return lax.broadcasted_iota(I32, (rows, LANES), 1)


def _head_block_diag():
    r = lax.broadcasted_iota(I32, (LANES, LANES), 0) // HEAD_DIM
    c = lax.broadcasted_iota(I32, (LANES, LANES), 1) // HEAD_DIM
    return (r == c).astype(BF16)


def _head_norm_rope(x, gain, cos, sin, bd, rope):
    y = x * x
    hi = y.astype(BF16)
    lo = (y - hi.astype(F32)).astype(BF16)
    seg = _dot(hi, bd) + _dot(lo, bd)
    xn = x * lax.rsqrt(seg * (1.0 / HEAD_DIM) + EPS) * gain
    if rope:
        first_half = (_lane_iota(x.shape[0]) % HEAD_DIM) < HEAD_DIM // 2
        partner = jnp.where(first_half, pltpu.roll(xn, LANES - HEAD_DIM // 2, 1),
                            pltpu.roll(xn, HEAD_DIM // 2, 1))
        xn = xn * cos + partner * sin
    return xn


def _prep_kernel(cb_ref, x_ref, g_ref, cos_ref, sin_ref, o_ref, *, rope):
    del cb_ref
    bd = _head_block_diag()
    cos = cos_ref[...]
    sin = sin_ref[...]
    for c in range(x_ref.shape[2] // LANES):
        sl = slice(c * LANES, (c + 1) * LANES)
        x = x_ref[0, :, sl].astype(F32)
        o_ref[0, :, sl] = _head_norm_rope(x, g_ref[0, :, sl], cos, sin, bd, rope).astype(o_ref.dtype)


def head_prep(src, gains, col_blocks, cos, sin, *, rope, ts=512):
    b, s, _ = src.shape
    n = len(col_blocks)
    ts = min(ts, s)
    cb = jnp.asarray(col_blocks, I32)

    def x_map(bi, si, ci, cb_ref):
        return (bi, si, cb_ref[ci])

    return pl.pallas_call(
        functools.partial(_prep_kernel, rope=rope),
        grid_spec=pltpu.PrefetchScalarGridSpec(
            num_scalar_prefetch=1,
            grid=(b, s // ts, n),
            in_specs=[pl.BlockSpec((1, ts, D_MODEL), x_map),
                      pl.BlockSpec((1, 1, D_MODEL), lambda bi, si, ci, cb_ref: (ci, 0, 0)),
                      pl.BlockSpec((ts, LANES), lambda bi, si, ci, cb_ref: (si, 0)),
                      pl.BlockSpec((ts, LANES), lambda bi, si, ci, cb_ref: (si, 0))],
            out_specs=pl.BlockSpec((1, ts, D_MODEL), lambda bi, si, ci, cb_ref: (bi, si, ci))),
        out_shape=jax.ShapeDtypeStruct((b, s, n * D_MODEL), BF16),
        compiler_params=_cp("parallel", "parallel", "arbitrary"),
        name="head_prep",
    )(cb, src, gains, cos, sin)


def rope_lane_tables(s):
    inv = ROPE_THETA ** (-jnp.arange(0, HEAD_DIM, 2, dtype=F32) / HEAD_DIM)
    ang = jnp.arange(s).astype(F32)[:, None] * inv[None, :]
    cos, sin = jnp.cos(ang), jnp.sin(ang)
    return jnp.tile(cos, (1, 4)), jnp.tile(jnp.concatenate([-sin, sin], axis=-1), (1, 2)), cos, sin


def _tile_gain(g, scale=1.0):
    return jnp.tile(g.astype(F32) * scale, N_HEADS).reshape(1, D_MODEL)


def _fox_cum_kernel(f_ref, b_ref, o_ref, carry_ref):
    si = pl.program_id(1)
    ts = f_ref.shape[1]

    @pl.when(si == 0)
    def _():
        carry_ref[...] = jnp.zeros_like(carry_ref)

    z = f_ref[0] + b_ref[...]
    log_f = -(jnp.maximum(-z, 0.0) + jnp.log1p(jnp.exp(-jnp.abs(z))))
    r = lax.broadcasted_iota(I32, (ts, ts), 0)
    c = lax.broadcasted_iota(I32, (ts, ts), 1)
    upper = (r <= c).astype(BF16)
    cum = _dot_f32_exact_rhs(log_f.T, upper) + carry_ref[:, 0:1]
    o_ref[0] = cum[0:N_HEADS, :] * LOG2E
    carry_ref[...] = jnp.broadcast_to(cum[:, ts - 1:ts], carry_ref.shape)


def fox_cumulative_gate(tail, b_f, *, ts=256):
    b, s, _ = tail.shape
    ts = min(ts, s)
    bias = jnp.zeros((1, LANES), F32).at[0, :N_HEADS].set(b_f.astype(F32))
    return pl.pallas_call(
        _fox_cum_kernel,
        grid=(b, s // ts),
        in_specs=[pl.BlockSpec((1, ts, LANES), lambda bi, si: (bi, si, 0)),
                  pl.BlockSpec((1, LANES), lambda bi, si: (0, 0))],
        out_specs=pl.BlockSpec((1, N_HEADS, ts), lambda bi, si: (bi, 0, si)),
        out_shape=jax.ShapeDtypeStruct((b, N_HEADS, s), F32),
        scratch_shapes=[pltpu.VMEM((LANES, LANES), F32)],
        compiler_params=_cp("parallel", "arbitrary"),
        name="fox_cumulative_gate",
    )(tail, bias)


def _flash_kernel(*refs, tq, tk, window, mode, fin, has_bias, pairs, lam_init):
    it = iter(refs)
    q_ref, k_ref, v_ref = next(it), next(it), next(it)
    kb_ref = next(it) if has_bias else None
    if fin == "diff":
        lam_ref, sg_ref = next(it), next(it)
    o_ref = next(it)
    m_sc, l_sc, acc_sc = next(it), next(it), next(it)
    s_even, s_odd = next(it), next(it)

    n_heads = 2 * pairs
    q_start = pl.program_id(2) * tq
    lane = _lane_iota(tq)
    low_half = lane < HEAD_DIM
    qh = []
    for h in range(n_heads):
        if mode == "pair":
            q = q_ref[0, :, (h // 2) * LANES:(h // 2 + 1) * LANES]
            zero = jnp.zeros_like(q)
            qh.append(jnp.where(low_half, q, zero) if h % 2 == 0 else jnp.where(low_half, zero, q))
        else:
            qh.append(q_ref[0, :, h * LANES:(h + 1) * LANES])

    m_sc[...] = jnp.full(m_sc.shape, M_INIT, F32)
    l_sc[...] = jnp.zeros(l_sc.shape, F32)
    acc_sc[...] = jnp.zeros(acc_sc.shape, F32)
    row = q_start + lax.broadcasted_iota(I32, (tq, LANES), 0)
    n_chunk = tk // LANES

    def scores(kv, dst):
        ks = pl.multiple_of(kv * tk, tk)
        kblk = k_ref[0, pl.ds(ks, tk), :]
        for h in range(n_heads):
            kk = kblk[:, (h // 2) * LANES:(h // 2 + 1) * LANES] if mode == "pair" else \
                kblk[:, (h % 2) * LANES:(h % 2 + 1) * LANES]
            s = _dot_nt(qh[h], kk)
            if has_bias:
                s = s - kb_ref[0, h // 2, kv][h % 2:h % 2 + 1, :]
            dst[h] = s

    def step(kv, masked, prefetch, cur, nxt):
        if prefetch:
            scores(kv + 1, nxt)
        ks = pl.multiple_of(kv * tk, tk)
        vblk = v_ref[0, pl.ds(ks, tk), :]
        if masked:
            masks = []
            for c in range(n_chunk):
                col = ks + c * LANES + lane
                mk = col <= row
                if window:
                    mk = mk & ((row - col) < window)
                masks.append(mk)
        for h in range(n_heads):
            vv = vblk[:, (h // 2) * LANES:(h // 2 + 1) * LANES] if mode == "pair" else vblk
            s = cur[h]
            chunks = [s[:, c * LANES:(c + 1) * LANES] for c in range(n_chunk)]
            if masked:
                chunks = [jnp.where(mk, ch, -jnp.inf) for mk, ch in zip(masks, chunks)]
            mb = functools.reduce(jnp.maximum, chunks)
            m_old = m_sc[h]
            m_new = jnp.maximum(m_old, jnp.broadcast_to(jnp.max(mb, axis=-1, keepdims=True), (tq, LANES)))
            alpha = jnp.exp2(m_old - m_new)
            ps = [jnp.exp2((ch - m_new).astype(BF16)) for ch in chunks]
            l_sc[h] = alpha * l_sc[h] + functools.reduce(jnp.add, ps).astype(F32)
            p = ps[0] if n_chunk == 1 else jnp.concatenate(ps, axis=1)
            acc_sc[h] = alpha * acc_sc[h] + _dot(p, vv)
            m_sc[h] = m_new

    def step_by_parity(kv, masked, prefetch):
        @pl.when(kv % 2 == 0)
        def _():
            step(kv, masked, prefetch, s_even, s_odd)

        @pl.when(kv % 2 == 1)
        def _():
            step(kv, masked, prefetch, s_odd, s_even)

    def loop(lo, hi, masked):
        def body(kv, carry):
            step_by_parity(kv, masked, True)
            return carry
        lax.fori_loop(lo, hi, body, 0)

    last_blk = (q_start + (tq - 1)) // tk
    full_hi = (q_start + 1) // tk
    if window:
        first_blk = jnp.maximum(q_start - (window - 1), 0) // tk
        full_lo = jnp.maximum(q_start + (tq - 1) - window + tk, 0) // tk
        full_lo = jnp.maximum(jnp.minimum(full_lo, full_hi), first_blk)
    else:
        first_blk = full_lo = 0

    @pl.when(first_blk % 2 == 0)
    def _():
        scores(first_blk, s_even)

    @pl.when(first_blk % 2 == 1)
    def _():
        scores(first_blk, s_odd)

    if window:
        loop(first_blk, full_lo, True)
    loop(full_lo, full_hi, False)
    loop(jnp.maximum(full_hi, full_lo), last_blk, True)
    step_by_parity(last_blk, True, False)

    if fin == "diff":
        lam_rows = lam_ref[...]
        lam = (jnp.exp(jnp.sum(lam_rows[0:1] * lam_rows[1:2], axis=-1, keepdims=True))
               - jnp.exp(jnp.sum(lam_rows[2:3] * lam_rows[3:4], axis=-1, keepdims=True)) + lam_init)
    for pi in range(pairs):
        l0 = jnp.maximum(jnp.sum(l_sc[2 * pi], axis=-1, keepdims=True), TINY)
        l1 = jnp.maximum(jnp.sum(l_sc[2 * pi + 1], axis=-1, keepdims=True), TINY)
        o0 = acc_sc[2 * pi] * (1.0 / l0)
        o1 = acc_sc[2 * pi + 1] * (1.0 / l1)
        if fin == "select":
            o = jnp.where(low_half, o0, o1)
        else:
            o = o0 - lam * o1
            ms = jnp.mean(o * o, axis=-1, keepdims=True)
            o = o * lax.rsqrt(ms + EPS) * sg_ref[...] * (1.0 - lam_init)
        o_ref[0, :, pi * LANES:(pi + 1) * LANES] = o.astype(o_ref.dtype)


def flash_attention(q, k, v, *, q_off, k_off, v_off, k_div=1, mode="pair", fin="select", window=0,
                    key_bias=None, lam=None, sub_gain=None, lam_init=0.0, pairs=FLASH_PAIRS,
                    tq=FLASH_TILE, tk=FLASH_TILE):
    nb, s, _ = q.shape
    tq, tk = min(tq, s), min(tk, s)
    n_inner = N_PAIRS // pairs
    wo = pairs * LANES
    if mode == "pair":
        wq = wk = wv = wo
    else:
        wq, wk, wv = 2 * wo, 2 * LANES, LANES
    in_specs = [pl.BlockSpec((1, tq, wq), lambda b, j, i: (b, i, q_off // (wq // LANES) + j)),
                pl.BlockSpec((1, s, wk), lambda b, j, i: (b, 0, k_off // (wk // LANES) + j // k_div)),
                pl.BlockSpec((1, s, wv), lambda b, j, i: (b, 0, v_off // (wv // LANES) + j // k_div))]
    args = [q, k, v]
    if key_bias is not None:
        in_specs.append(pl.BlockSpec((1, pairs, s // tk, 2, tk), lambda b, j, i: (b, j, 0, 0, 0)))
        args.append(key_bias)
    if fin == "diff":
        in_specs += [pl.BlockSpec((4, LANES), lambda b, j, i: (0, 0)),
                     pl.BlockSpec((1, LANES), lambda b, j, i: (0, 0))]
        args += [lam, sub_gain]
    return pl.pallas_call(
        functools.partial(_flash_kernel, tq=tq, tk=tk, window=window, mode=mode, fin=fin,
                          has_bias=key_bias is not None, pairs=pairs, lam_init=lam_init),
        grid=(nb, n_inner, s // tq),
        in_specs=in_specs,
        out_specs=pl.BlockSpec((1, tq, wo), lambda b, j, i: (b, i, j)),
        out_shape=jax.ShapeDtypeStruct((nb, s, D_MODEL), BF16),
        scratch_shapes=[pltpu.VMEM((2 * pairs, tq, LANES), F32)] * 3 + [pltpu.VMEM((2 * pairs, tq, tk), F32)] * 2,
        compiler_params=_cp("parallel", "parallel", "arbitrary"),
        name="flash_" + mode + "_" + fin,
    )(*args)


DIL_SUB = LANES
DIL_UNROLL = 4


def _rows(start, size, stride):
    return pl.ds(start, size) if stride == 1 else pl.ds(start, size, stride=stride)


def _dilated_kernel(*refs, seq):
    q_refs, k_refs, v_refs = refs[0:3], refs[3:6], refs[6:9]
    o_ref = refs[9]
    qf, kf, vf, m_st, l_st, acc_st = refs[10:16]
    sub = min(DIL_SUB, seq // DIL_PAIRS[-1][1])
    lane = _lane_iota(sub)
    low_half = lane < HEAD_DIM
    for g, (window, d) in enumerate(DIL_PAIRS):
        sd = seq // d
        span = min(2 * sub, sd)
        n_res = sd // sub
        win = window // d + 1
        assert win <= span - sub + 1 or span == sd
        qf[...] = q_refs[g][0].astype(F32)
        kf[...] = k_refs[g][0].astype(F32)
        vf[...] = v_refs[g][0].astype(F32)
        low_half_kv = _lane_iota(span) < HEAD_DIM
        ones = jnp.ones((span, LANES), BF16)

        def chain(t_idx, g=g, d=d, sd=sd, span=span, n_res=n_res, win=win, low_half_kv=low_half_kv, ones=ones):
            r = t_idx // n_res
            i0 = (t_idx % n_res) * sub
            ks = jnp.minimum(jnp.maximum(i0 - sub, 0), sd - span)
            q_rows = _rows(i0 * d + r, sub, d)
            kv_rows = _rows(ks * d + r, span, d)
            q = qf[q_rows, :].astype(BF16)
            kblk = kf[kv_rows, :].astype(BF16)
            vblk = vf[kv_rows, :].astype(BF16)
            zero = jnp.zeros_like(q)
            row = i0 + lax.broadcasted_iota(I32, (sub, LANES), 0)
            masks = []
            for c in range(span // LANES):
                col = ks + c * LANES + lane
                masks.append((col <= row) & ((row - col) < win))
            ms, accs = [], []
            for h in range(2):
                own = low_half if h == 0 else jnp.logical_not(low_half)
                own_kv = low_half_kv if h == 0 else jnp.logical_not(low_half_kv)
                s = _dot_nt(jnp.where(own, q, zero), kblk)
                chunks = [jnp.where(mk, s[:, c * LANES:(c + 1) * LANES], -jnp.inf) for c, mk in enumerate(masks)]
                m = jnp.broadcast_to(jnp.max(functools.reduce(jnp.maximum, chunks), axis=-1, keepdims=True),
                                     (sub, LANES))
                ps = [jnp.exp2(ch - m) for ch in chunks]
                p = ps[0] if len(ps) == 1 else jnp.concatenate(ps, axis=1)
                accs.append(_dot(p.astype(BF16), jnp.where(own_kv, vblk, ones)))
                ms.append(m)
            m_c = jnp.where(low_half, ms[0], ms[1])
            acc_c = jnp.where(low_half, accs[0], accs[1])
            l_c = jnp.where(low_half, pltpu.roll(accs[0], HEAD_DIM, 1), pltpu.roll(accs[1], HEAD_DIM, 1))
            if g == 0:
                m_st[q_rows, :] = m_c
                l_st[q_rows, :] = l_c
                acc_st[q_rows, :] = acc_c
            else:
                m_old = m_st[q_rows, :]
                m_new = jnp.maximum(m_old, m_c)
                a, bb = jnp.exp2(m_old - m_new), jnp.exp2(m_c - m_new)
                m_st[q_rows, :] = m_new
                l_st[q_rows, :] = a * l_st[q_rows, :] + bb * l_c
                acc_st[q_rows, :] = a * acc_st[q_rows, :] + bb * acc_c

        n_chain = seq // sub
        unroll = min(DIL_UNROLL, n_chain)

        def body(it, carry, chain=chain, unroll=unroll):
            for u in range(unroll):
                chain(it * unroll + u)
            return carry
        lax.fori_loop(0, n_chain // unroll, body, 0)
    o_ref[0] = (acc_st[...] * (1.0 / jnp.maximum(l_st[...], TINY))).astype(o_ref.dtype)


def dilated_groups_attention(qk, main):
    b, s, _ = qk.shape
    ng = len(DIL_PAIRS)
    blk = lambda off: pl.BlockSpec((1, s, LANES), lambda bi, j, off=off: (bi, 0, off + j))
    in_specs = ([blk(2 * g * N_PAIRS) for g in range(ng)] + [blk((2 * g + 1) * N_PAIRS) for g in range(ng)]
                + [blk((3 * g + 2) * N_PAIRS) for g in range(ng)])
    return pl.pallas_call(
        functools.partial(_dilated_kernel, seq=s),
        grid=(b, N_PAIRS),
        in_specs=in_specs,
        out_specs=pl.BlockSpec((1, s, LANES), lambda bi, j: (bi, 0, j)),
        out_shape=jax.ShapeDtypeStruct((b, s, D_MODEL), BF16),
        scratch_shapes=[pltpu.VMEM((s, LANES), F32)] * 6,
        compiler_params=_cp("parallel", "parallel"),
        name="dilated_groups_attention",
    )(*([qk] * (2 * ng) + [main] * ng))


def _nsa_prep_kernel(q_ref, ks_ref, vs_ref, kw_ref, vw_ref, g_ref, cos_ref, sin_ref,
                     qo_ref, ks2_ref, vs2_ref, kw2_ref, vw2_ref):
    ts = q_ref.shape[1]
    bd = _head_block_diag()
    cos, sin = cos_ref[...], sin_ref[...]
    lane = _lane_iota(ts)
    low_half = lane < HEAD_DIM
    for c in range(N_PAIRS):
        sl = slice(c * LANES, (c + 1) * LANES)
        qo_ref[0, :, sl] = _head_norm_rope(q_ref[0, :, sl].astype(F32), g_ref[0:1, :], cos, sin, bd,
                                           True).astype(qo_ref.dtype)
    t = pl.program_id(1) * ts + lax.broadcasted_iota(I32, (ts, LANES), 0)
    blk_onehot = ((t // NSA_BLOCK) == (lane % HEAD_DIM)).astype(F32)
    zeros = jnp.zeros((ts, LANES), F32)

    def spread(x, fill, out_ref, c):
        xr = pltpu.roll(x, HEAD_DIM, 1)
        base = 2 * c * 2 * LANES
        out_ref[0, :, base:base + LANES] = jnp.where(low_half, x, fill).astype(out_ref.dtype)
        out_ref[0, :, base + LANES:base + 2 * LANES] = jnp.where(low_half, fill, xr).astype(out_ref.dtype)
        out_ref[0, :, base + 2 * LANES:base + 3 * LANES] = jnp.where(low_half, xr, fill).astype(out_ref.dtype)
        out_ref[0, :, base + 3 * LANES:base + 4 * LANES] = jnp.where(low_half, fill, x).astype(out_ref.dtype)

    def dup(x, out_ref, c):
        xr = pltpu.roll(x, HEAD_DIM, 1)
        out_ref[0, :, 2 * c * LANES:(2 * c + 1) * LANES] = jnp.where(low_half, x, xr).astype(out_ref.dtype)
        out_ref[0, :, (2 * c + 1) * LANES:(2 * c + 2) * LANES] = jnp.where(low_half, xr, x).astype(out_ref.dtype)

    for c in range(NSA_GROUPS // 2):
        sl = slice(c * LANES, (c + 1) * LANES)
        ks = _head_norm_rope(ks_ref[0, :, sl].astype(F32), g_ref[1:2, :], cos, sin, bd, True)
        kw = _head_norm_rope(kw_ref[0, :, sl].astype(F32), g_ref[2:3, :], cos, sin, bd, True)
        spread(ks, blk_onehot, ks2_ref, c)
        spread(kw, zeros, kw2_ref, c)
        dup(vs_ref[0, :, sl].astype(F32), vs2_ref, c)
        dup(vw_ref[0, :, sl].astype(F32), vw2_ref, c)


def nsa_prep(main, gains, cos, sin, *, ts=512):
    b, s, _ = main.shape
    ts = min(ts, s)
    gw = NSA_GROUPS * HEAD_DIM

    def kv_spec(i):
        return pl.BlockSpec((1, ts, gw), lambda bi, si: (bi, si, i))

    return pl.pallas_call(
        _nsa_prep_kernel,
        grid=(b, s // ts),
        in_specs=[pl.BlockSpec((1, ts, D_MODEL), lambda bi, si: (bi, si, 0)),
                  kv_spec(6), kv_spec(7), kv_spec(8), kv_spec(9),
                  pl.BlockSpec((8, LANES), lambda bi, si: (0, 0)),
                  pl.BlockSpec((ts, LANES), lambda bi, si: (si, 0)),
                  pl.BlockSpec((ts, LANES), lambda bi, si: (si, 0))],
        out_specs=[pl.BlockSpec((1, ts, D_MODEL), lambda bi, si: (bi, si, 0)),
                   pl.BlockSpec((1, ts, NSA_GROUPS * 2 * LANES), lambda bi, si: (bi, si, 0)),
                   pl.BlockSpec((1, ts, NSA_GROUPS * LANES), lambda bi, si: (bi, si, 0)),
                   pl.BlockSpec((1, ts, NSA_GROUPS * 2 * LANES), lambda bi, si: (bi, si, 0)),
                   pl.BlockSpec((1, ts, NSA_GROUPS * LANES), lambda bi, si: (bi, si, 0))],
        out_shape=[jax.ShapeDtypeStruct((b, s, D_MODEL), BF16),
                   jax.ShapeDtypeStruct((b, s, NSA_GROUPS * 2 * LANES), BF16),
                   jax.ShapeDtypeStruct((b, s, NSA_GROUPS * LANES), BF16),
                   jax.ShapeDtypeStruct((b, s, NSA_GROUPS * 2 * LANES), BF16),
                   jax.ShapeDtypeStruct((b, s, NSA_GROUPS * LANES), BF16)],
        compiler_params=_cp("parallel", "parallel"),
        name="nsa_prep",
    )(main, main, main, main, main, gains, cos, sin)


def _nsa_compress_kernel(x_ref, pos_ref, w1_ref, w2_ref, g_ref, cos_ref, sin_ref, o_ref, *, is_key):
    x = (x_ref[...].astype(F32) + pos_ref[...]).astype(BF16)
    hid = _silu(_dot(x, w1_ref[...]))
    y = _dot(hid.astype(BF16), w2_ref[...])
    if is_key:
        ms = jnp.mean(y * y, axis=-1, keepdims=True)
        y = y * lax.rsqrt(ms + EPS) * g_ref[...]
        r = lax.broadcasted_iota(I32, (HEAD_DIM, HEAD_DIM), 0)
        c = lax.broadcasted_iota(I32, (HEAD_DIM, HEAD_DIM), 1)
        swap = (((r + HEAD_DIM // 2) % HEAD_DIM) == c).astype(BF16)
        y = y * cos_ref[...] + _dot_f32_exact_rhs(y, swap) * sin_ref[...]
    o_ref[...] = y


def nsa_compress(x, pos, w1, w2, gain, cos_blk, sin_blk, *, is_key):
    rows, k = x.shape
    nb = cos_blk.shape[0]
    hid = w1.shape[1]
    return pl.pallas_call(
        functools.partial(_nsa_compress_kernel, is_key=is_key),
        grid=(rows // nb,),
        in_specs=[pl.BlockSpec((nb, k), lambda i: (i, 0)),
                  pl.BlockSpec((1, k), lambda i: (0, 0)),
                  pl.BlockSpec((k, hid), lambda i: (0, 0)),
                  pl.BlockSpec((hid, HEAD_DIM), lambda i: (0, 0)),
                  pl.BlockSpec((1, HEAD_DIM), lambda i: (0, 0)),
                  pl.BlockSpec((nb, HEAD_DIM), lambda i: (0, 0)),
                  pl.BlockSpec((nb, HEAD_DIM), lambda i: (0, 0))],
        out_specs=pl.BlockSpec((nb, HEAD_DIM), lambda i: (i, 0)),
        out_shape=jax.ShapeDtypeStruct((rows, HEAD_DIM), F32),
        compiler_params=_cp("parallel"),
        name="nsa_compress",
    )(x, pos, w1, w2, gain, cos_blk, sin_blk)


def _nsa_cmp_kernel(q_ref, kc_ref, vc_ref, o_ref, qa_ref, *, n_sel):
    tq = q_ref.shape[1]
    lane = _lane_iota(tq)
    low_half = lane < HEAD_DIM
    blk = lane % HEAD_DIM
    t = pl.program_id(2) * tq + lax.broadcasted_iota(I32, (tq, LANES), 0)
    cmask = (blk + 1) * NSA_BLOCK <= t + 1
    kc = kc_ref[0, 0]
    vc = vc_ref[0, 0]
    imp = jnp.zeros((tq, LANES), F32)
    qblk = [q_ref[0, :, 0:LANES], q_ref[0, :, LANES:2 * LANES]]
    zero = jnp.zeros_like(qblk[0])
    outs = [None, None]
    for p in range(NSA_HPG):
        in_low = (p % 2) == 0
        qb = qblk[p // 2]
        qm = jnp.where(low_half, qb, zero) if in_low else jnp.where(low_half, zero, qb)
        s = jnp.where(cmask, _dot_nt(qm, kc), NEG_INF)
        m = jnp.max(s, axis=-1, keepdims=True)
        e = jnp.where(cmask, jnp.exp2(s - m), 0.0)
        den = jnp.maximum(0.5 * jnp.sum(e, axis=-1, keepdims=True), TINY)
        pc = e / den
        imp = imp + pc
        o = _dot(pc.astype(BF16), vc)
        prev = outs[p // 2]
        outs[p // 2] = o if prev is None else jnp.where(low_half, prev, o)
    o_ref[0, :, 0:LANES] = outs[0].astype(o_ref.dtype)
    o_ref[0, :, LANES:2 * LANES] = outs[1].astype(o_ref.dtype)

    cur = t // NSA_BLOCK
    forced = (blk == 0) | (blk == cur) | (blk == cur - 1)
    x = jnp.where(blk > cur, -1.0, jnp.where(forced, NSA_HPG + 1.0, imp))
    blk_f = blk.astype(F32)
    selected = jnp.zeros((tq, LANES), jnp.bool_)
    for _ in range(n_sel):
        mx = jnp.max(x, axis=-1, keepdims=True)
        first = jnp.min(jnp.where(x == mx, blk_f, float(LANES)), axis=-1, keepdims=True)
        hit = blk_f == first
        selected = selected | hit
        x = jnp.where(hit, -2.0, x)
    sel_bias = jnp.where(selected, 0.0, NEG_INF).astype(qa_ref.dtype)
    for p in range(NSA_HPG):
        qb = qblk[p // 2]
        qa = jnp.where(low_half, qb, sel_bias) if p % 2 == 0 else jnp.where(low_half, sel_bias, qb)
        qa_ref[0, :, p * LANES:(p + 1) * LANES] = qa


def nsa_compressed_attention(q, kc2, vc2, *, n_sel, tq=1024):
    b, s, _ = q.shape
    tq = min(tq, s)
    gq = NSA_HPG * HEAD_DIM
    return pl.pallas_call(
        functools.partial(_nsa_cmp_kernel, n_sel=n_sel),
        grid=(b, NSA_GROUPS, s // tq),
        in_specs=[pl.BlockSpec((1, tq, gq), lambda bi, g, i: (bi, i, g)),
                  pl.BlockSpec((1, 1, LANES, LANES), lambda bi, g, i: (bi, g, 0, 0)),
                  pl.BlockSpec((1, 1, LANES, LANES), lambda bi, g, i: (bi, g, 0, 0))],
        out_specs=[pl.BlockSpec((1, tq, gq), lambda bi, g, i: (bi, i, g)),
                   pl.BlockSpec((1, tq, 2 * gq), lambda bi, g, i: (bi, i, g))],
        out_shape=[jax.ShapeDtypeStruct((b, s, D_MODEL), BF16),
                   jax.ShapeDtypeStruct((b, s, 2 * D_MODEL), BF16)],
        compiler_params=_cp("parallel", "parallel", "parallel"),
        name="nsa_compressed_attention",
    )(q, kc2, vc2)


def _route(logits_t, rb):
    scores = _sigmoid(logits_t)
    sel = scores + rb
    rows = [sel[i:i + 1, :] for i in range(N_EXPERTS)]
    srows = [scores[i:i + 1, :] for i in range(N_EXPERTS)]
    best = grp = None
    for g in range(N_EXPERT_GROUPS):
        a, b, c, d = rows[4 * g:4 * g + 4]
        hi1, lo1, hi2, lo2 = jnp.maximum(a, b), jnp.minimum(a, b), jnp.maximum(c, d), jnp.minimum(c, d)
        gs = jnp.maximum(hi1, hi2) + jnp.maximum(jnp.minimum(hi1, hi2), jnp.maximum(lo1, lo2))
        if g == 0:
            best, grp = gs, jnp.zeros(gs.shape, I32)
        else:
            better = gs > best
            grp = jnp.where(better, g, grp)
            best = jnp.where(better, gs, best)

    def pick(vals, i):
        out = vals[i]
        for g in range(1, N_EXPERT_GROUPS):
            out = jnp.where(grp == g, vals[4 * g + i], out)
        return out

    v = [pick(rows, i) for i in range(EXPERTS_PER_GROUP)]
    w = [pick(srows, i) for i in range(EXPERTS_PER_GROUP)]
    l1, b1, w1 = jnp.zeros(grp.shape, I32), v[0], w[0]
    for i in range(1, EXPERTS_PER_GROUP):
        better = v[i] > b1
        l1 = jnp.where(better, i, l1)
        b1 = jnp.where(better, v[i], b1)
        w1 = jnp.where(better, w[i], w1)
    have = jnp.zeros(grp.shape, jnp.bool_)
    l2, b2, w2 = jnp.zeros(grp.shape, I32), jnp.zeros_like(b1), jnp.zeros_like(w1)
    for i in range(EXPERTS_PER_GROUP):
        valid = l1 != i
        better = valid & (jnp.logical_not(have) | (v[i] > b2))
        l2 = jnp.where(better, i, l2)
        b2 = jnp.where(better, v[i], b2)
        w2 = jnp.where(better, w[i], w2)
        have = have | valid
    wsum = w1 + w2
    return (grp * EXPERTS_PER_GROUP + l1, grp * EXPERTS_PER_GROUP + l2), (w1 / wsum, w2 / wsum)


def _outproj_router_kernel(*refs, mix):
    it = iter(refs)
    if mix == "plain":
        x = next(it)[0]
    elif mix == "nsa":
        o_refs = [next(it), next(it), next(it)]
        gl = _sigmoid(next(it)[0])
        g_hi = gl.astype(BF16)
        g_lo = (gl - g_hi.astype(F32)).astype(BF16)
        r = lax.broadcasted_iota(I32, (LANES, D_MODEL), 0)
        c = lax.broadcasted_iota(I32, (LANES, D_MODEL), 1) // HEAD_DIM
        x = None
        for i in range(3):
            expand = (r == c + i * N_HEADS).astype(BF16)
            term = (_dot(g_hi, expand) + _dot(g_lo, expand)) * o_refs[i][0].astype(F32)
            x = term if x is None else x + term
        x = x.astype(BF16)
    else:
        raise ValueError(mix)
    h_ref, mod_ref, w_ref, g2_ref, rw_ref, rb_ref = (next(it) for _ in range(6))
    ho_ref, u_ref, e_ref, wt_ref = (next(it) for _ in range(4))
    mod = mod_ref[0]
    h_new = h_ref[0] + mod[2:3, :] * _dot(x, w_ref[...])
    ho_ref[0] = h_new
    u = _modulated_norm(h_new, g2_ref[...], mod, 3, 4)
    _store_token_tiles(u_ref, u)
    logits_t = _dot_f32(rw_ref[...], u, dot=_dot_nt)
    eidx, wts = _route(logits_t, rb_ref[:, 0:1])
    for k in range(TOP_K):
        e_ref[0, k:k + 1, :] = eidx[k]
        wt_ref[0, k:k + 1, :] = wts[k]


def outproj_router(attn_inputs, h, mod, w_out, gain2, router_wt, router_b, *, mix, tm=512):
    b, s, d = h.shape
    tm = min(tm, s)
    row_spec = pl.BlockSpec((1, tm, d), lambda bi, si: (bi, si, 0))
    in_specs, args = [], []
    for a in attn_inputs:
        in_specs.append(pl.BlockSpec((1, tm, a.shape[2]), lambda bi, si: (bi, si, 0)))
        args.append(a)
    in_specs += [row_spec,
                 pl.BlockSpec((1, 6, d), lambda bi, si: (bi, 0, 0)),
                 pl.BlockSpec((d, d), lambda bi, si: (0, 0)),
                 pl.BlockSpec((1, d), lambda bi, si: (0, 0)),
                 pl.BlockSpec((N_EXPERTS, d), lambda bi, si: (0, 0)),
                 pl.BlockSpec((N_EXPERTS, LANES), lambda bi, si: (0, 0))]
    args += [h, mod, w_out, gain2.reshape(1, d), router_wt, router_b]
    return pl.pallas_call(
        functools.partial(_outproj_router_kernel, mix=mix),
        grid=(b, s // tm),
        in_specs=in_specs,
        out_specs=[row_spec, pl.BlockSpec((tm * ROW_TILE, LANES), lambda bi, si: (bi * (s // tm) + si, 0)),
                   pl.BlockSpec((1, TOP_K, tm), lambda bi, si: (bi, 0, si)),
                   pl.BlockSpec((1, TOP_K, tm), lambda bi, si: (bi, 0, si))],
        out_shape=[jax.ShapeDtypeStruct((b, s, d), F32), jax.ShapeDtypeStruct((b * s * ROW_TILE, LANES), F32),
                   jax.ShapeDtypeStruct((b, TOP_K, s), I32), jax.ShapeDtypeStruct((b, TOP_K, s), F32)],
        compiler_params=_cp("parallel", "parallel"),
        name="outproj_router_" + mix,
    )(*args)


ROW_TILE = D_MODEL // LANES
FFN_PIECES = 6
FFN_SLOTS = 3


def _store_token_tiles(ref, x):
    rows = x.shape[0]
    for c in range(ROW_TILE):
        ref[pl.ds(c, rows, stride=ROW_TILE), :] = x[:, c * LANES:(c + 1) * LANES]


def _load_token_tiles(ref, rows):
    return jnp.concatenate([ref[pl.ds(c, rows, stride=ROW_TILE), :] for c in range(ROW_TILE)], axis=1)


def _tile_copy(src_hbm, index, dst, r, sem):
    start = pl.multiple_of(index * ROW_TILE, ROW_TILE)
    return pltpu.make_async_copy(src_hbm.at[pl.ds(start, ROW_TILE), :], dst.at[pl.ds(r * ROW_TILE, ROW_TILE), :], sem)


def _gather_tiles_loop(src_hbm, index_of, dst, n_rows, sem):
    def one(r, carry):
        _tile_copy(src_hbm, index_of(r), dst, r, sem).start()
        return carry
    lax.fori_loop(0, n_rows, one, 0)


def _wait_tiles(src_hbm, dst, sem):
    pltpu.make_async_copy(src_hbm.at[pl.ds(0, dst.shape[0]), :], dst, sem).wait()


def _moe_ffn_kernel(be_ref, tok_cur, tok_next, tok_ahead, u_hbm, wg_ref, wu_ref, wd_ref, y_ref, xbuf, wg_sc, wu_sc,
                    wd_sc, sem):
    i = pl.program_id(0)
    n = pl.num_programs(0)
    rows = MOE_ROWS
    slot = i % FFN_SLOTS
    nxt = (i + 2) % FFN_SLOTS

    @pl.when(i == 0)
    def _():
        _gather_tiles_loop(u_hbm, lambda r: tok_cur[r], xbuf.at[0], rows, sem.at[0])
        _gather_tiles_loop(u_hbm, lambda r: tok_next[r], xbuf.at[1], rows, sem.at[1])

    @pl.when((i == 0) | (be_ref[i] != be_ref[jnp.maximum(i - 1, 0)]))
    def _():
        wg_sc[...] = wg_ref[0, 0].astype(BF16)
        wu_sc[...] = wu_ref[0, 0].astype(BF16)
        wd_sc[...] = wd_ref[0, 0].astype(BF16)

    _wait_tiles(u_hbm, xbuf.at[slot], sem.at[slot])

    per = rows // FFN_PIECES + 1
    issued = [0]

    def issue_some():
        lo, hi = issued[0], min(issued[0] + per, rows)
        for r in range(lo, hi):
            _tile_copy(u_hbm, tok_ahead[r], xbuf.at[nxt], r, sem.at[nxt]).start(priority=r % 2)
        issued[0] = hi

    x = _load_token_tiles(xbuf.at[slot], rows).astype(BF16)
    g = _dot(x, wg_sc[...])
    issue_some()
    u = _dot(x, wu_sc[...])
    issue_some()
    hid = (_silu(g) * u).astype(BF16)
    d = wd_sc.shape[1]
    n_out = FFN_PIECES - 2
    w = d // n_out
    for c in range(n_out):
        y = _dot(hid, wd_sc[:, c * w:(c + 1) * w])
        for j in range(w // LANES):
            y_ref[pl.ds(c * (w // LANES) + j, rows, stride=ROW_TILE), :] = y[:, j * LANES:(j + 1) * LANES]
        issue_some()
    assert issued[0] == rows

    @pl.when(i == n - 1)
    def _():
        for ahead in (1, 2):
            s = (i + ahead) % FFN_SLOTS
            _wait_tiles(u_hbm, xbuf.at[s], sem.at[s])


def moe_expert_ffn(u_tiles, row_token, blk_expert, w_gate, w_up, w_down, layer):
    r_total = row_token.shape[0]
    nblk = r_total // MOE_ROWS
    d, ff = w_gate.shape[2], w_gate.shape[3]
    smem_blk = functools.partial(pl.BlockSpec, (MOE_ROWS,), memory_space=pltpu.SMEM)
    return pl.pallas_call(
        _moe_ffn_kernel,
        grid_spec=pltpu.PrefetchScalarGridSpec(
            num_scalar_prefetch=1,
            grid=(nblk,),
            in_specs=[smem_blk(lambda i, be: (i,)),
                      smem_blk(lambda i, be: (jnp.minimum(i + 1, nblk - 1),)),
                      smem_blk(lambda i, be: (jnp.minimum(i + 2, nblk - 1),)),
                      pl.BlockSpec(memory_space=pl.ANY),
                      pl.BlockSpec((1, 1, d, ff), lambda i, be: (layer, be[i], 0, 0)),
                      pl.BlockSpec((1, 1, d, ff), lambda i, be: (layer, be[i], 0, 0)),
                      pl.BlockSpec((1, 1, ff, d), lambda i, be: (layer, be[i], 0, 0))],
            out_specs=pl.BlockSpec((MOE_ROWS * ROW_TILE, LANES), lambda i, be: (i, 0)),
            scratch_shapes=[pltpu.VMEM((FFN_SLOTS, MOE_ROWS * ROW_TILE, LANES), F32), pltpu.VMEM((d, ff), BF16),
                            pltpu.VMEM((d, ff), BF16), pltpu.VMEM((ff, d), BF16),
                            pltpu.SemaphoreType.DMA((FFN_SLOTS,))]),
        out_shape=jax.ShapeDtypeStruct((r_total * ROW_TILE, LANES), F32),
        compiler_params=_cp("arbitrary"),
        name="moe_expert_ffn",
    )(blk_expert, row_token, row_token, row_token, u_tiles, w_gate, w_up, w_down)


def _moe_combine_kernel(d_cur, d_next, ys_hbm, h_ref, mod_ref, w_ref, o_ref, buf, sem):
    i = pl.program_id(0)
    n = pl.num_programs(0)
    tm = h_ref.shape[0]
    slot = i % 2
    nxt = 1 - slot

    @pl.when(i == 0)
    def _():
        for k in range(TOP_K):
            _gather_tiles_loop(ys_hbm, lambda r, k=k: d_cur[k, r], buf.at[0, k], tm, sem.at[0])

    for r in range(tm):
        for k in range(TOP_K):
            _tile_copy(ys_hbm, d_next[k, r], buf.at[nxt, k], r, sem.at[nxt]).start(priority=k)

    for k in range(TOP_K):
        _wait_tiles(ys_hbm, buf.at[slot, k], sem.at[slot])
    w = w_ref[...]
    y = w[:, 0:1] * _load_token_tiles(buf.at[slot, 0], tm) + w[:, 1:2] * _load_token_tiles(buf.at[slot, 1], tm)
    o_ref[...] = h_ref[...] + mod_ref[0, 5:6, :] * y

    @pl.when(i == n - 1)
    def _():
        for k in range(TOP_K):
            _wait_tiles(ys_hbm, buf.at[nxt, k], sem.at[nxt])


def moe_combine(ys, dest, h, mod, wts, *, tm=256):
    b, s, d = h.shape
    t = b * s
    tm = min(tm, s)
    per_b = s // tm
    n = t // tm
    smem_blk = functools.partial(pl.BlockSpec, (TOP_K, tm), memory_space=pltpu.SMEM)
    out = pl.pallas_call(
        _moe_combine_kernel,
        grid=(n,),
        in_specs=[smem_blk(lambda i: (0, i)),
                  smem_blk(lambda i: (0, jnp.minimum(i + 1, n - 1))),
                  pl.BlockSpec(memory_space=pl.ANY),
                  pl.BlockSpec((tm, d), lambda i: (i, 0)),
                  pl.BlockSpec((1, 6, d), lambda i: (i // per_b, 0, 0)),
                  pl.BlockSpec((tm, TOP_K), lambda i: (i, 0))],
        out_specs=pl.BlockSpec((tm, d), lambda i: (i, 0)),
        out_shape=jax.ShapeDtypeStruct((t, d), F32),
        scratch_shapes=[pltpu.VMEM((2, TOP_K, tm * ROW_TILE, LANES), F32), pltpu.SemaphoreType.DMA((2,))],
        compiler_params=_cp("arbitrary"),
        name="moe_combine",
    )(dest, dest, ys, h.reshape(t, d), mod, wts)
    return out.reshape(b, s, d)


def moe_layer(h, u, eidx, wts, mod, w_gate, w_up, w_down, layer):
    b, s, d = h.shape
    t = b * s
    e_flat = eidx.transpose(0, 2, 1).reshape(-1)
    n_pairs = t * TOP_K
    r_total = n_pairs + N_EXPERTS * MOE_ROWS
    nblk = r_total // MOE_ROWS
    onehot = (e_flat[:, None] == jnp.arange(N_EXPERTS, dtype=I32)[None, :]).astype(I32)
    csum = jnp.cumsum(onehot, axis=0)
    counts = csum[-1]
    rank = jnp.take_along_axis(csum, e_flat[:, None], axis=1)[:, 0] - 1
    padded = (counts + MOE_ROWS - 1) // MOE_ROWS * MOE_ROWS
    pad_end = jnp.cumsum(padded)
    pad_start = pad_end - padded
    dest = (pad_start[e_flat] + rank).astype(I32)
    row_token = jnp.zeros((r_total,), I32).at[dest].set(jnp.arange(n_pairs, dtype=I32) // TOP_K)
    blk_expert = jnp.minimum(jnp.searchsorted(pad_end, jnp.arange(nblk, dtype=I32) * MOE_ROWS, side="right"),
                             N_EXPERTS - 1).astype(I32)
    ys = moe_expert_ffn(u, row_token, blk_expert, w_gate, w_up, w_down, layer)
    dest2 = dest.reshape(t, TOP_K).T
    w_tok = wts.transpose(0, 2, 1).reshape(t, TOP_K)
    return moe_combine(ys, dest2, h, mod, w_tok)


def _pad_cols(w, n):
    return jnp.pad(w, ((0, 0), (0, n - w.shape[1])))


def fox_attention(h, mod, gain, w_in, b_f, q_gain, k_gain, cos, sin):
    b, s, _ = h.shape
    n_main = 3 * D_MODEL
    main, tail = norm_proj(h, mod, gain, w_in[:, :n_main].astype(BF16),
                           _pad_cols(w_in[:, n_main:], LANES).astype(BF16))
    gains = jnp.stack([_tile_gain(q_gain, Q_SCALE), _tile_gain(k_gain)])
    qk = head_prep(main, gains, (0, 1), cos, sin, rope=False)
    tk = min(FLASH_TILE, s)
    cum = fox_cumulative_gate(tail, b_f)
    key_bias = cum.reshape(b, N_PAIRS, 2, s // tk, tk).transpose(0, 1, 3, 2, 4)
    return flash_attention(qk, qk, main, q_off=0, k_off=N_PAIRS, v_off=2 * N_PAIRS, key_bias=key_bias, tk=tk)


def diff_attention(h, mod, gain, w_in, q_gain, k_gain, lambdas, sub_gain, layer_idx, cos, sin):
    b, s, _ = h.shape
    main = norm_proj(h, mod, gain, w_in.astype(BF16))
    gains = jnp.stack([_tile_gain(q_gain, Q_SCALE), _tile_gain(k_gain)])
    qk = head_prep(main, gains, (0, 1), cos, sin, rope=True)
    lam_init = 0.8 - 0.6 * math.exp(-0.3 * layer_idx)
    lam = jnp.pad(lambdas.astype(F32), ((0, 0), (0, LANES - HEAD_DIM)))
    return flash_attention(qk, qk, main, q_off=0, k_off=N_PAIRS, v_off=2 * N_PAIRS, fin="diff", lam=lam,
                           sub_gain=sub_gain.astype(F32).reshape(1, LANES), lam_init=lam_init)


def dilated_attention(h, mod, gain, w_in, q_gain, k_gain, cos, sin):
    b, s, _ = h.shape
    ng = len(DIL_PAIRS)
    main = norm_proj(h, mod, gain, w_in.astype(BF16), tn=3 * D_MODEL)
    gq, gk = _tile_gain(q_gain, Q_SCALE), _tile_gain(k_gain)
    qk = head_prep(main, jnp.stack([gq, gk] * ng), tuple(3 * g + j for g in range(ng) for j in range(2)),
                   cos, sin, rope=True)
    return dilated_groups_attention(qk, main)


def nsa_attention(h, mod, gain, w_in, q_gain, k_gain, cmp_pos, cmp_w1, cmp_w2, cos, sin, cos_h, sin_h):
    b, s, _ = h.shape
    nblk = s // NSA_BLOCK
    n_main = D_MODEL + 6 * NSA_GROUPS * HEAD_DIM
    main, tail = norm_proj(h, mod, gain, w_in[:, :n_main].astype(BF16),
                           _pad_cols(w_in[:, n_main:], LANES).astype(BF16))
    gains = jnp.zeros((8, LANES), F32)
    gains = gains.at[0].set(jnp.tile(q_gain.astype(F32) * Q_SCALE, 2))
    gains = gains.at[1].set(jnp.tile(k_gain[1].astype(F32), 2)).at[2].set(jnp.tile(k_gain[2].astype(F32), 2))
    q, ks2, vs2, kw2, vw2 = nsa_prep(main, gains, cos, sin)

    def to_block_rows(col0):
        x = main[:, :, col0:col0 + NSA_GROUPS * HEAD_DIM].reshape(b, nblk, NSA_BLOCK, NSA_GROUPS, HEAD_DIM)
        return x.transpose(0, 3, 1, 2, 4).reshape(b * NSA_GROUPS * nblk, NSA_BLOCK * HEAD_DIM)

    cos_b = jnp.tile(cos_h[NSA_BLOCK - 1::NSA_BLOCK], (1, 2))
    sin_b = jnp.concatenate([-sin_h[NSA_BLOCK - 1::NSA_BLOCK], sin_h[NSA_BLOCK - 1::NSA_BLOCK]], axis=-1)
    kc = nsa_compress(to_block_rows(D_MODEL), cmp_pos[0].reshape(1, -1), cmp_w1[0].astype(BF16),
                      cmp_w2[0].astype(BF16), k_gain[0].astype(F32).reshape(1, HEAD_DIM), cos_b, sin_b, is_key=True)
    vc = nsa_compress(to_block_rows(D_MODEL + NSA_GROUPS * HEAD_DIM), cmp_pos[1].reshape(1, -1),
                      cmp_w1[1].astype(BF16), cmp_w2[1].astype(BF16),
                      k_gain[0].astype(F32).reshape(1, HEAD_DIM), cos_b, sin_b, is_key=False)
    kc = kc.reshape(b, NSA_GROUPS, nblk, HEAD_DIM)
    vc = vc.reshape(b, NSA_GROUPS, nblk, HEAD_DIM)
    pad_rows = ((0, 0), (0, 0), (0, HEAD_DIM - nblk), (0, 0))
    kc = jnp.pad(kc, pad_rows)
    vc = jnp.pad(vc, pad_rows)
    kc2 = jnp.tile(kc, (1, 1, 2, 2)).astype(BF16)
    vc2 = jnp.concatenate([jnp.tile(vc, (1, 1, 1, 2)), jnp.zeros_like(jnp.tile(vc, (1, 1, 1, 2)))],
                          axis=2).astype(BF16)
    o_cmp, q_aug = nsa_compressed_attention(q, kc2, vc2, n_sel=min(NSA_TOPN, nblk))
    common = dict(q_off=0, k_off=0, v_off=0, k_div=NSA_HPG // (2 * FLASH_PAIRS), mode="aug")
    o_sel = flash_attention(q_aug, ks2, vs2, **common)
    o_win = flash_attention(q_aug, kw2, vw2, window=NSA_WINDOW, **common)
    return [o_cmp, o_sel, o_win, tail]


def kernel(x, c, fox_w_in, fox_b_f, fox_q_gain, fox_k_gain, fox_w_out, nsa_w_in, nsa_q_gain, nsa_k_gain, nsa_cmp_pos, nsa_cmp_w1, nsa_cmp_w2, nsa_w_out, dil_w_in, dil_q_gain, dil_k_gain, dil_w_out, diff_w_in, diff_q_gain, diff_k_gain, diff_lambda, diff_sub_gain, diff_w_out, norm_gain, ada_w, ada_b, router_w, router_b, moe_w_gate, moe_w_up, moe_w_down):
    b, s, d = x.shape
    depth = norm_gain.shape[0]
    cos, sin, cos_h, sin_h = rope_lane_tables(s)
    mods = ada_modulation(c, ada_w, ada_b).reshape(depth, b, 6, d)
    router_wt = router_w.T.astype(F32)
    router_bb = jnp.broadcast_to(router_b.astype(F32)[:, None], (N_EXPERTS, LANES))
    h = x
    for i in range(depth):
        mod = mods[i]
        kind, j = i % 4, i // 4
        g1 = norm_gain[i, 0]
        if kind == 0:
            attn = [fox_attention(h, mod, g1, fox_w_in[j], fox_b_f[j], fox_q_gain[j], fox_k_gain[j], cos, sin)]
            w_out, mix = fox_w_out[j], "plain"
        elif kind == 1:
            attn = nsa_attention(h, mod, g1, nsa_w_in[j], nsa_q_gain[j], nsa_k_gain[j], nsa_cmp_pos[j],
                                 nsa_cmp_w1[j], nsa_cmp_w2[j], cos, sin, cos_h, sin_h)
            w_out, mix = nsa_w_out[j], "nsa"
        elif kind == 2:
            attn = [dilated_attention(h, mod, g1, dil_w_in[j], dil_q_gain[j], dil_k_gain[j], cos, sin)]
            w_out, mix = dil_w_out[j], "plain"
        else:
            attn = [diff_attention(h, mod, g1, diff_w_in[j], diff_q_gain[j], diff_k_gain[j], diff_lambda[j],
                                   diff_sub_gain[j], i, cos, sin)]
            w_out, mix = diff_w_out[j], "plain"
        h, u, eidx, wts = outproj_router(attn, h, mod, w_out.astype(BF16), norm_gain[i, 1], router_wt,
                                         router_bb, mix=mix)
        h = moe_layer(h, u, eidx, wts, mod, moe_w_gate, moe_w_up, moe_w_down, i)
    return h
```
